```python
import math
import jax, jax.numpy as jnp
from jax import lax
import numpy as np

D_MODEL = 2048
BATCH = 4
SEQ = 2048
DEPTH = 1
DEC_BATCH = 128
DEC_SEQ = 1
PAST_LEN = 16384
PAGE_SIZE = 128

N_META = 16
MLA_V = 128
MLA_NOPE = 128
MLA_ROPE = 64
MLA_HEADS = (D_MODEL // 2) // MLA_V
Q_LORA = D_MODEL // 4
KV_LORA = D_MODEL // 4
RET_DK = 128
RET_DV = 128
RET_HEADS = (D_MODEL - MLA_HEADS * MLA_V) // RET_DV
MIX_WIDTH = MLA_HEADS * MLA_V + RET_HEADS * RET_DV
RET_CHUNK = 128
Q_BLOCK = 128
N_IN = Q_LORA + KV_LORA + MLA_ROPE + RET_HEADS * (2 * RET_DK + 2 * RET_DV)
N_GROUPS = 8
EXPERTS_PER_GROUP = 8
N_EXPERTS = N_GROUPS * EXPERTS_PER_GROUP
TOP_K = 2
D_EXPERT = D_MODEL // 4
ROPE_BASE = 10000.0
EPS = 1e-6
NEG = -1e30

kernel_name = 'hymba_mla_retention_hmoe_step'


def rms_norm(x, g):
    xf = x.astype(jnp.float32)
    y = xf * lax.rsqrt(jnp.mean(xf * xf, axis=-1, keepdims=True) + EPS)
    return (y * g.astype(jnp.float32)).astype(x.dtype)


def head_norm(x, g):
    xf = x.astype(jnp.float32)
    mu = jnp.mean(xf, axis=-1, keepdims=True)
    var = jnp.mean(jnp.square(xf - mu), axis=-1, keepdims=True)
    y = ((xf - mu) * lax.rsqrt(var + EPS)).reshape(x.shape[:2] + (-1,))
    return y * g.astype(jnp.float32)


def apply_rope(x, pos):
    dim = x.shape[-1]
    half = dim // 2
    inv = jnp.exp(jnp.arange(half, dtype=jnp.float32) * (-2.0 * math.log(ROPE_BASE) / dim))
    ang = pos.astype(jnp.float32)[:, None] * inv[None, :]
    ang = ang.reshape((1, ang.shape[0]) + (1,) * (x.ndim - 3) + (half,))
    cos, sin = jnp.cos(ang), jnp.sin(ang)
    xf = x.astype(jnp.float32)
    x1, x2 = xf[..., :half], xf[..., half:]
    return jnp.concatenate([x1 * cos - x2 * sin, x1 * sin + x2 * cos], axis=-1).astype(x.dtype)


def retention_log_decay():
    return jnp.log1p(-jnp.exp2(-5.0 - jnp.arange(RET_HEADS, dtype=jnp.float32)))


def retention_chunk(state, q, k, v, log_g):
    c = q.shape[1]
    n = jnp.arange(c, dtype=jnp.float32)
    diff = n[:, None] - n[None, :]
    decay = jnp.where(diff >= 0, jnp.exp(log_g[:, None, None] * jnp.maximum(diff, 0.0)), 0.0)
    scores = jnp.einsum('bqhd,bkhd->bhqk', q, k) * decay
    out = jnp.einsum('bhqk,bkhv->bqhv', scores, v)
    q_decay = jnp.exp((n[:, None] + 1.0) * log_g[None, :])
    out = out + jnp.einsum('bqhd,bhdv->bqhv', q, state) * q_decay[None, :, :, None]
    k_decay = jnp.exp((c - 1.0 - n)[:, None] * log_g[None, :])
    new_state = (state * jnp.exp(c * log_g)[None, :, None, None]
                 + jnp.einsum('bkhd,kh,bkhv->bhdv', k, k_decay, v))
    return new_state, out


def retention_prompt(q, k, v, log_g):
    b, l = q.shape[:2]
    lead = (-N_META) % RET_CHUNK
    pad = lambda t: jnp.pad(t.astype(jnp.float32), ((0, 0), (lead, 0), (0, 0), (0, 0)))
    q, k, v = pad(q), pad(k), pad(v)
    total = lead + l
    n_full = total // RET_CHUNK
    main = n_full * RET_CHUNK
    to_chunks = lambda t: jnp.moveaxis(t[:, :main].reshape(b, n_full, RET_CHUNK, t.shape[2], t.shape[3]), 1, 0)
    state0 = jnp.zeros((b, RET_HEADS, RET_DK, RET_DV), jnp.float32)

    def step(s, qkv):
        qc, kc, vc = qkv
        return retention_chunk(s, qc, kc, vc, log_g)

    state, out = lax.scan(step, state0, (to_chunks(q), to_chunks(k), to_chunks(v)))
    out = jnp.moveaxis(out, 0, 1).reshape(b, main, RET_HEADS, RET_DV)
    if main < total:
        state, tail = retention_chunk(state, q[:, main:], k[:, main:], v[:, main:], log_g)
        out = jnp.concatenate([out, tail], axis=1)
    return out[:, lead:], state


def mixer_inputs(h, pos, w_in, q_norm_g, w_uq, w_uk, kv_norm_g):
    b, s = h.shape[:2]
    sizes = (Q_LORA, KV_LORA, MLA_ROPE, RET_HEADS * RET_DK, RET_HEADS * RET_DK,
             RET_HEADS * RET_DV, RET_HEADS * RET_DV)
    idx = [sum(sizes[:j]) for j in range(1, len(sizes))]
    c_q, c_kv, k_r, r_q, r_k, r_v, r_g = jnp.split(h @ w_in, idx, axis=-1)
    q = jnp.einsum('bsr,rhe->bshe', rms_norm(c_q, q_norm_g), w_uq)
    q_nope, q_rope = q[..., :MLA_NOPE], apply_rope(q[..., MLA_NOPE:], pos)
    q_lat = jnp.einsum('bshn,lhn->bshl', q_nope, w_uk)
    kv = rms_norm(c_kv, kv_norm_g)
    kr = apply_rope(k_r, pos)
    rq = apply_rope(r_q.reshape(b, s, RET_HEADS, RET_DK), pos)
    rk = apply_rope(r_k.reshape(b, s, RET_HEADS, RET_DK), pos) * (RET_DK ** -0.5)
    rv = r_v.reshape(b, s, RET_HEADS, RET_DV)
    return q_lat, q_rope, kv, kr, rq, rk, rv, r_g


def mla_scores(q_lat, q_rope, kv, kr):
    s = jnp.einsum('bqhl,bkl->bhqk', q_lat, kv) + jnp.einsum('bqhr,bkr->bhqk', q_rope, kr)
    return s.astype(jnp.float32) * ((MLA_NOPE + MLA_ROPE) ** -0.5)


def mla_prompt_attn(q_lat, q_rope, kv, kr):
    b, l = q_lat.shape[:2]
    nb = -(-l // Q_BLOCK)
    lp = nb * Q_BLOCK
    padq = lambda t: jnp.moveaxis(
        jnp.pad(t, ((0, 0), (0, lp - l), (0, 0), (0, 0))).reshape((b, nb, Q_BLOCK) + t.shape[2:]), 1, 0)
    k_pos = jnp.arange(l)

    def block(args):
        i, ql, qr = args
        q_pos = i * Q_BLOCK + jnp.arange(Q_BLOCK)
        s = jnp.where(k_pos[None, :] <= q_pos[:, None], mla_scores(ql, qr, kv, kr), NEG)
        p = jax.nn.softmax(s, axis=-1)
        return jnp.einsum('bhqk,bkl->bqhl', p.astype(kv.dtype), kv)

    o = lax.map(block, (jnp.arange(nb), padq(q_lat), padq(q_rope)))
    return jnp.moveaxis(o, 0, 1).reshape(b, lp, MLA_HEADS, KV_LORA)[:, :l]


def online_update(carry, s, kv):
    m, l, acc = carry
    m_new = jnp.maximum(m, jnp.max(s, axis=-1))
    a = jnp.exp(m - m_new)
    p = jnp.exp(s - m_new[..., None])
    acc = acc * a[..., None] + jnp.einsum('bhqk,bkl->bhql', p, kv.astype(jnp.float32))
    return (m_new, l * a + jnp.sum(p, axis=-1), acc)


def mla_sample_attn(q_lat, q_rope, kv_new, kr_new, pool_kv, pool_kr, page_table):
    b, s = q_lat.shape[:2]
    init = (jnp.full((b, MLA_HEADS, s), NEG, jnp.float32),
            jnp.zeros((b, MLA_HEADS, s), jnp.float32),
            jnp.zeros((b, MLA_HEADS, s, KV_LORA), jnp.float32))

    def page_step(carry, pages):
        kv, kr = pool_kv[pages], pool_kr[pages]
        return online_update(carry, mla_scores(q_lat, q_rope, kv, kr), kv), None

    carry, _ = lax.scan(page_step, init, page_table.T)
    causal = jnp.arange(s)[None, :] <= jnp.arange(s)[:, None]
    s_self = jnp.where(causal, mla_scores(q_lat, q_rope, kv_new, kr_new), NEG)
    m, l, acc = online_update(carry, s_self, kv_new)
    o = acc / l[..., None]
    return jnp.transpose(o, (0, 2, 1, 3)).astype(q_lat.dtype)


def mixer_output(o_lat, o_ret, r_g, w_uv, mla_out_g, ret_gn_g, w_o):
    b, s = o_lat.shape[:2]
    o_mla = jnp.einsum('bshl,lhv->bshv', o_lat, w_uv).reshape(b, s, -1)
    o_mla = rms_norm(o_mla, mla_out_g)
    o_r = (head_norm(o_ret, ret_gn_g) * jax.nn.silu(r_g.astype(jnp.float32))).astype(r_g.dtype)
    return jnp.concatenate([o_mla, o_r], axis=-1) @ w_o


def hier_moe(h, w_group, b_group, w_router, b_router, w_gate, w_up, w_down):
    shp = h.shape
    t = h.reshape(-1, shp[-1])
    g_prob = jax.nn.softmax((t @ w_group + b_group).astype(jnp.float32), axis=-1)
    g_w, g_idx = lax.top_k(g_prob, 1)
    e_logits = (t @ w_router + b_router).astype(jnp.float32).reshape(-1, N_GROUPS, EXPERTS_PER_GROUP)
    e_logits = jnp.take_along_axis(e_logits, g_idx[:, :, None], axis=1)[:, 0]
    e_w, e_idx = lax.top_k(jax.nn.softmax(e_logits, axis=-1), TOP_K)
    e_w = e_w / jnp.sum(e_w, axis=-1, keepdims=True) * g_w
    gates = jnp.sum(jax.nn.one_hot(g_idx * EXPERTS_PER_GROUP + e_idx, N_EXPERTS, dtype=jnp.float32)
                    * e_w[..., None], axis=1)
    hg = jnp.einsum('td,edf->tef', t, w_gate)
    hu = jnp.einsum('td,edf->tef', t, w_up)
    a = jax.nn.silu(hg) * hu * gates[:, :, None].astype(t.dtype)
    return jnp.einsum('tef,efd->td', a, w_down).reshape(shp)


def setup_inputs(seed: int = 0) -> dict:
    key = jax.random.key(seed)
    ks = jax.random.split(key, 32)
    f32 = jnp.float32
    n_pages = PAST_LEN // PAGE_SIZE
    n_used = DEC_BATCH * n_pages
    n_pool = n_used + max(1, n_used // 4)
    nrm = lambda k, shape, scale: jax.random.normal(k, shape, f32) * scale
    gain = lambda k, shape: 1.0 + 0.02 * jax.random.normal(k, shape, f32)
    page_table = jax.random.permutation(ks[5], n_pool)[:n_used].reshape(DEC_BATCH, n_pages).astype(jnp.int32)
    return {
        'x_prompt': nrm(ks[0], (BATCH, SEQ, D_MODEL), 1.0),
        'x_sample': nrm(ks[1], (DEC_BATCH, DEC_SEQ, D_MODEL), 1.0),
        'cache_kv_latent': nrm(ks[2], (DEPTH, n_pool, PAGE_SIZE, KV_LORA), 1.0),
        'cache_k_rope': nrm(ks[3], (DEPTH, n_pool, PAGE_SIZE, MLA_ROPE), 1.0),
        'state_retention': nrm(ks[4], (DEPTH, DEC_BATCH, RET_HEADS, RET_DK, RET_DV), 0.1),
        'page_table': page_table,
        'meta_tokens': nrm(ks[6], (N_META, D_MODEL), 1.0),
        'ln_mix_g': gain(ks[7], (DEPTH, D_MODEL)),
        'w_in': nrm(ks[8], (DEPTH, D_MODEL, N_IN), D_MODEL ** -0.5),
        'q_norm_g': gain(ks[9], (DEPTH, Q_LORA)),
        'w_uq': nrm(ks[10], (DEPTH, Q_LORA, MLA_HEADS, MLA_NOPE + MLA_ROPE), Q_LORA ** -0.5),
        'w_uk': nrm(ks[11], (DEPTH, KV_LORA, MLA_HEADS, MLA_NOPE), KV_LORA ** -0.5),
        'kv_norm_g': gain(ks[12], (DEPTH, KV_LORA)),
        'w_uv': nrm(ks[13], (DEPTH, KV_LORA, MLA_HEADS, MLA_V), KV_LORA ** -0.5),
        'mla_out_g': gain(ks[14], (DEPTH, MLA_HEADS * MLA_V)),
        'ret_gn_g': gain(ks[15], (DEPTH, RET_HEADS * RET_DV)),
        'w_o': nrm(ks[16], (DEPTH, MIX_WIDTH, D_MODEL), MIX_WIDTH ** -0.5),
        'ln_ffn_g': gain(ks[17], (DEPTH, D_MODEL)),
        'w_group': nrm(ks[18], (DEPTH, D_MODEL, N_GROUPS), D_MODEL ** -0.5),
        'b_group': nrm(ks[19], (DEPTH, N_GROUPS), 0.01),
        'w_router': nrm(ks[20], (DEPTH, D_MODEL, N_EXPERTS), D_MODEL ** -0.5),
        'b_router': nrm(ks[21], (DEPTH, N_EXPERTS), 0.01),
        'w_gate': nrm(ks[22], (DEPTH, N_EXPERTS, D_MODEL, D_EXPERT), D_MODEL ** -0.5),
        'w_up': nrm(ks[23], (DEPTH, N_EXPERTS, D_MODEL, D_EXPERT), D_MODEL ** -0.5),
        'w_down': nrm(ks[24], (DEPTH, N_EXPERTS, D_EXPERT, D_MODEL), D_EXPERT ** -0.5),
        'final_g': gain(ks[25], (D_MODEL,)),
    }


def reference(x_prompt, x_sample, cache_kv_latent, cache_k_rope, state_retention, page_table,
              meta_tokens, ln_mix_g, w_in, q_norm_g, w_uq, w_uk, kv_norm_g, w_uv, mla_out_g,
              ret_gn_g, w_o, ln_ffn_g, w_group, b_group, w_router, b_router, w_gate, w_up,
              w_down, final_g):
    b = x_prompt.shape[0]
    xp = jnp.concatenate([jnp.broadcast_to(meta_tokens[None].astype(x_prompt.dtype),
                                           (b, N_META, x_prompt.shape[2])), x_prompt], axis=1)
    xs = x_sample
    past_len = page_table.shape[1] * cache_kv_latent.shape[2]
    pos_p = jnp.arange(xp.shape[1])
    pos_s = past_len + jnp.arange(xs.shape[1])
    log_g = retention_log_decay()
    kv_p_l, kr_p_l, st_p_l, kv_s_l, kr_s_l, st_s_l = [], [], [], [], [], []
    for i in range(DEPTH):
        qlp, qrp, kvp, krp, rqp, rkp, rvp, rgp = mixer_inputs(
            rms_norm(xp, ln_mix_g[i]), pos_p, w_in[i], q_norm_g[i], w_uq[i], w_uk[i], kv_norm_g[i])
        o_lat_p = mla_prompt_attn(qlp, qrp, kvp, krp)
        o_ret_p, st_p = retention_prompt(rqp, rkp, rvp, log_g)
        xp = xp + mixer_output(o_lat_p, o_ret_p, rgp, w_uv[i], mla_out_g[i], ret_gn_g[i], w_o[i])
        qls, qrs, kvs, krs, rqs, rks, rvs, rgs = mixer_inputs(
            rms_norm(xs, ln_mix_g[i]), pos_s, w_in[i], q_norm_g[i], w_uq[i], w_uk[i], kv_norm_g[i])
        o_lat_s = mla_sample_attn(qls, qrs, kvs, krs, cache_kv_latent[i], cache_k_rope[i], page_table)
        st_s, o_ret_s = retention_chunk(state_retention[i].astype(jnp.float32), rqs.astype(jnp.float32),
                                        rks.astype(jnp.float32), rvs.astype(jnp.float32), log_g)
        xs = xs + mixer_output(o_lat_s, o_ret_s, rgs, w_uv[i], mla_out_g[i], ret_gn_g[i], w_o[i])
        xp = xp + hier_moe(rms_norm(xp, ln_ffn_g[i]), w_group[i], b_group[i], w_router[i], b_router[i],
                           w_gate[i], w_up[i], w_down[i])
        xs = xs + hier_moe(rms_norm(xs, ln_ffn_g[i]), w_group[i], b_group[i], w_router[i], b_router[i],
                           w_gate[i], w_up[i], w_down[i])
        kv_p_l.append(kvp)
        kr_p_l.append(krp)
        st_p_l.append(st_p.astype(state_retention.dtype))
        kv_s_l.append(kvs)
        kr_s_l.append(krs)
        st_s_l.append(st_s.astype(state_retention.dtype))
    y_prompt = rms_norm(xp, final_g)[:, N_META:]
    y_sample = rms_norm(xs, final_g)
    return (y_prompt, y_sample, jnp.stack(kv_p_l), jnp.stack(kr_p_l), jnp.stack(st_p_l),
            jnp.stack(kv_s_l), jnp.stack(kr_s_l), jnp.stack(st_s_l))
```

```python
import functools
import math

import jax
import jax.numpy as jnp
from jax import lax
from jax.experimental import pallas as pl
from jax.experimental.pallas import tpu as pltpu

F32 = jnp.float32
BF16 = jnp.bfloat16

N_META = 16
ROPE_BASE = 10000.0
EPS = 1e-6
NEG = -1e30
EXPERTS_PER_GROUP = 8
LANES = 128
ROW_TILE = 128
VMEM_LIMIT = 56 * 1024 * 1024


def _cparams(sem, vmem=VMEM_LIMIT):
    return pltpu.CompilerParams(dimension_semantics=sem, vmem_limit_bytes=vmem)


def _rms(x, g):
    return x * lax.rsqrt(jnp.mean(x * x, axis=-1, keepdims=True) + EPS) * g


def _dot(a, b):
    return jnp.dot(a, b, preferred_element_type=F32)


def _dot_nt(a, b):
    return lax.dot_general(a, b, (((1,), (1,)), ((), ())), preferred_element_type=F32)


def _dot_tn(a, b):
    return lax.dot_general(a, b, (((0,), (0,)), ((), ())), preferred_element_type=F32)


def _in_a_kernel(x_ref, g_ref, w_ref, qg_ref, kvg_ref, cs_ref, sn_ref,
                 cq_ref, kv_ref, kvb_ref, kr_ref, krb_ref, *, ql, kvl, rope):
    h = _rms(x_ref[...], g_ref[...]).astype(BF16)
    c = _dot(h, w_ref[...])
    cq_ref[...] = _rms(c[:, :ql], qg_ref[...]).astype(BF16)
    kv = _rms(c[:, ql:ql + kvl], kvg_ref[...])
    kv_ref[...] = kv
    kvb_ref[...] = kv.astype(BF16)
    a = c[:, ql + kvl:ql + kvl + rope]
    b = c[:, ql + kvl + rope:]
    kr = a * cs_ref[...] + b * sn_ref[...]
    kr_ref[...] = kr
    krb_ref[...] = kr.astype(BF16)


def _in_a(x, g, w_a, qg, kvg, cs, sn, *, tm, ql, kvl, rope):
    rows, d = x.shape
    n = w_a.shape[1]
    row = lambda i: (i, 0)
    fix = lambda i: (0, 0)
    return pl.pallas_call(
        functools.partial(_in_a_kernel, ql=ql, kvl=kvl, rope=rope),
        grid=(rows // tm,),
        in_specs=[pl.BlockSpec((tm, d), row), pl.BlockSpec((1, d), fix), pl.BlockSpec((d, n), fix),
                  pl.BlockSpec((1, ql), fix), pl.BlockSpec((1, kvl), fix),
                  pl.BlockSpec((tm, rope), row), pl.BlockSpec((tm, rope), row)],
        out_specs=[pl.BlockSpec((tm, ql), row), pl.BlockSpec((tm, kvl), row), pl.BlockSpec((tm, kvl), row),
                   pl.BlockSpec((tm, rope), row), pl.BlockSpec((tm, rope), row)],
        out_shape=[jax.ShapeDtypeStruct((rows, ql), BF16), jax.ShapeDtypeStruct((rows, kvl), F32),
                   jax.ShapeDtypeStruct((rows, kvl), BF16), jax.ShapeDtypeStruct((rows, rope), F32),
                   jax.ShapeDtypeStruct((rows, rope), BF16)],
        compiler_params=_cparams(("parallel",)),
        name="in_proj_a",
    )(x, g, w_a, qg, kvg, cs, sn)


def _in_b_kernel(x_ref, g_ref, w_ref, cos_ref, sin_ref, out_ref, h_scr, *, heads, dk):
    j = pl.program_id(1)

    @pl.when(j == 0)
    def _():
        h_scr[...] = _rms(x_ref[...], g_ref[...]).astype(BF16)

    c = _dot(h_scr[...], w_ref[...])

    @pl.when(j < 2)
    def _():
        scale = jnp.where(j == 1, dk ** -0.5, 1.0).astype(F32)
        cos = cos_ref[...]
        sin = sin_ref[...]
        for hh in range(heads):
            blk = c[:, hh * dk:(hh + 1) * dk]
            r = blk * cos + pltpu.roll(blk, dk // 2, 1) * sin
            out_ref[:, hh * dk:(hh + 1) * dk] = (r * scale).astype(BF16)

    @pl.when(j >= 2)
    def _():
        out_ref[...] = c.astype(BF16)


def _in_b(x, g, w_b, cos, sin, *, tm, heads, dk):
    rows, d = x.shape
    n = heads * dk
    return pl.pallas_call(
        functools.partial(_in_b_kernel, heads=heads, dk=dk),
        grid=(rows // tm, 4),
        in_specs=[pl.BlockSpec((tm, d), lambda i, j: (i, 0)), pl.BlockSpec((1, d), lambda i, j: (0, 0)),
                  pl.BlockSpec((d, n), lambda i, j: (0, j)),
                  pl.BlockSpec((tm, dk), lambda i, j: (i, 0)), pl.BlockSpec((tm, dk), lambda i, j: (i, 0))],
        out_specs=pl.BlockSpec((tm, n), lambda i, j: (i, j)),
        out_shape=jax.ShapeDtypeStruct((rows, 4 * n), BF16),
        scratch_shapes=[pltpu.VMEM((tm, d), BF16)],
        compiler_params=_cparams(("parallel", "arbitrary")),
        name="in_proj_b",
    )(x, g, w_b, cos, sin)


def _q_kernel(cq_ref, wuq_ref, wukt_ref, cos_ref, sin_ref, ql_ref, qr_ref, *, heads, nope, rope, scale):
    qf = _dot(cq_ref[...], wuq_ref[...])
    o1 = heads * nope
    o2 = o1 + heads * rope
    qr = (qf[:, o1:o2] * cos_ref[...] + qf[:, o2:] * sin_ref[...]) * scale
    for hh in range(heads):
        qn = qf[:, hh * nope:(hh + 1) * nope].astype(BF16)
        ql_ref[hh] = (_dot(qn, wukt_ref[hh]) * scale).astype(BF16)
        qr_ref[hh] = qr[:, hh * rope:(hh + 1) * rope].astype(BF16)


def _q_proj(cq, wuq, wukt, cos, sin, *, tm, heads, nope, rope, scale):
    rows, ql = cq.shape
    kvl = wukt.shape[2]
    nq = wuq.shape[1]
    return pl.pallas_call(
        functools.partial(_q_kernel, heads=heads, nope=nope, rope=rope, scale=scale),
        grid=(rows // tm,),
        in_specs=[pl.BlockSpec((tm, ql), lambda i: (i, 0)), pl.BlockSpec((ql, nq), lambda i: (0, 0)),
                  pl.BlockSpec((heads, nope, kvl), lambda i: (0, 0, 0)),
                  pl.BlockSpec((tm, heads * rope), lambda i: (i, 0)),
                  pl.BlockSpec((tm, heads * rope), lambda i: (i, 0))],
        out_specs=[pl.BlockSpec((heads, tm, kvl), lambda i: (0, i, 0)),
                   pl.BlockSpec((heads, tm, rope), lambda i: (0, i, 0))],
        out_shape=[jax.ShapeDtypeStruct((heads, rows, kvl), BF16),
                   jax.ShapeDtypeStruct((heads, rows, rope), BF16)],
        compiler_params=_cparams(("parallel",)),
        name="q_proj",
    )(cq, wuq, wukt, cos, sin)


def _uv_norm(o_heads, wuv_ref, g):
    parts = [_dot(o.astype(BF16), wuv_ref[hh]) for hh, o in enumerate(o_heads)]
    return _rms(jnp.concatenate(parts, axis=1), g).astype(BF16)


def _widen(x, n):
    if n <= LANES:
        return x[:, :n]
    return jnp.concatenate([x] * (n // LANES), axis=1)


def _att_kernel(ql_ref, qr_ref, kv_ref, kr_ref, kvm_ref, krm_ref, wuv_ref, g_ref, o_ref,
                m_scr, l_scr, acc_scr, *, heads, tq, tk, nk):
    qi = pl.program_id(1)
    kj = pl.program_id(2)
    kvl = ql_ref.shape[2]
    rows = heads * tq
    ql = ql_ref[...].reshape(rows, kvl)
    qr = qr_ref[...].reshape(rows, qr_ref.shape[2])

    def update(s, kv, first):
        m_cur = jnp.max(s, axis=1, keepdims=True)
        if first:
            m_new = jnp.broadcast_to(m_cur, (rows, LANES))
        else:
            m_prev = m_scr[...]
            m_new = jnp.maximum(m_prev, m_cur)
        p = jnp.exp(s - _widen(m_new, s.shape[1]))
        p_sum = jnp.sum(p, axis=1, keepdims=True)
        pv = _dot(p.astype(BF16), kv)
        if first:
            l_scr[...] = jnp.broadcast_to(p_sum, (rows, LANES))
            acc_scr[...] = pv
        else:
            alpha = jnp.exp(m_prev - m_new)
            l_scr[...] = alpha * l_scr[...] + p_sum
            acc_scr[...] = acc_scr[...] * _widen(alpha, kvl) + pv
        m_scr[...] = m_new

    @pl.when(kj == 0)
    def _():
        kvm = kvm_ref[...]
        update(_dot_nt(ql, kvm) + _dot_nt(qr, krm_ref[...]), kvm, True)

    last = (qi * tq + tq - 1) // tk

    @pl.when(kj <= last)
    def _():
        kv = kv_ref[...]
        s = _dot_nt(ql, kv) + _dot_nt(qr, kr_ref[...])
        qpos = qi * tq + lax.broadcasted_iota(jnp.int32, (rows, tk), 0) % tq
        kpos = kj * tk + lax.broadcasted_iota(jnp.int32, (rows, tk), 1)
        update(jnp.where(kpos <= qpos, s, NEG), kv, False)

    @pl.when(kj == nk - 1)
    def _():
        o = acc_scr[...] / _widen(l_scr[...], kvl)
        o_ref[...] = _uv_norm([o[hh * tq:(hh + 1) * tq] for hh in range(heads)], wuv_ref, g_ref[...])


def _attention(ql, qr, kvb, krb, kvm, krm, wuv, g, *, batch, seq, tq, tk):
    heads, rows, kvl = ql.shape
    rope = qr.shape[2]
    vdim = wuv.shape[2]
    nq, nk = seq // tq, seq // tk
    nmeta = kvm.shape[0]

    def kidx(b, qi, kj):
        return (b * nk + jnp.minimum(kj, (qi * tq + tq - 1) // tk), 0)

    return pl.pallas_call(
        functools.partial(_att_kernel, heads=heads, tq=tq, tk=tk, nk=nk),
        grid=(batch, nq, nk),
        in_specs=[pl.BlockSpec((heads, tq, kvl), lambda b, qi, kj: (0, b * nq + qi, 0)),
                  pl.BlockSpec((heads, tq, rope), lambda b, qi, kj: (0, b * nq + qi, 0)),
                  pl.BlockSpec((tk, kvl), kidx), pl.BlockSpec((tk, rope), kidx),
                  pl.BlockSpec((nmeta, kvl), lambda b, qi, kj: (0, 0)),
                  pl.BlockSpec((nmeta, rope), lambda b, qi, kj: (0, 0)),
                  pl.BlockSpec((heads, kvl, vdim), lambda b, qi, kj: (0, 0, 0)),
                  pl.BlockSpec((1, heads * vdim), lambda b, qi, kj: (0, 0))],
        out_specs=pl.BlockSpec((tq, heads * vdim), lambda b, qi, kj: (b * nq + qi, 0)),
        out_shape=jax.ShapeDtypeStruct((rows, heads * vdim), BF16),
        scratch_shapes=[pltpu.VMEM((heads * tq, LANES), F32), pltpu.VMEM((heads * tq, LANES), F32),
                        pltpu.VMEM((heads * tq, kvl), F32)],
        compiler_params=_cparams(("parallel", "parallel", "arbitrary")),
        name="mla_prompt_attention",
    )(ql, qr, kvb, krb, kvm, krm, wuv, g)


def _dec_kernel(pt_ref, ql_ref, qr_ref, kvn_ref, krn_ref, ckv_hbm, ckr_hbm, o_ref,
                kvbuf, krbuf, sem, *, nchunk, gpages, nbuf, total):
    b = pl.program_id(0)
    page = kvbuf.shape[2]
    kvl = kvbuf.shape[3]
    rope = krbuf.shape[3]

    def copies(g, slot):
        bb = g // nchunk
        c0 = (g % nchunk) * gpages
        out = []
        for p in range(gpages):
            pg = pt_ref[bb, c0 + p]
            out.append(pltpu.make_async_copy(ckv_hbm.at[pg], kvbuf.at[slot, p], sem.at[slot]))
            out.append(pltpu.make_async_copy(ckr_hbm.at[pg], krbuf.at[slot, p], sem.at[slot]))
        return out

    def start(g, slot):
        for cp in copies(g, slot):
            cp.start()

    @pl.when(b == 0)
    def _():
        for g0 in range(min(nbuf - 1, total)):
            start(g0, g0 % nbuf)

    ql = ql_ref[0].astype(F32)
    qr = qr_ref[0].astype(F32)
    kvn = kvn_ref[0]
    krn = krn_ref[0]
    heads = ql.shape[0]
    m0 = jnp.sum(ql * kvn, axis=1, keepdims=True) + jnp.sum(qr * krn, axis=1, keepdims=True)
    l0 = jnp.ones((heads, 1), F32)
    acc0 = jnp.broadcast_to(kvn, (heads, kvl))

    def body(c, carry):
        m, l, acc = carry
        g = b * nchunk + c
        slot = g % nbuf
        for cp in copies(g, slot):
            cp.wait()
        nxt = g + nbuf - 1

        @pl.when(nxt < total)
        def _():
            start(nxt, nxt % nbuf)

        kvc = kvbuf[slot].reshape(gpages * page, kvl)
        krc = krbuf[slot].reshape(gpages * page, rope)
        s = _dot_nt(ql, kvc) + _dot_nt(qr, krc)
        m_new = jnp.maximum(m, jnp.max(s, axis=1, keepdims=True))
        alpha = jnp.exp(m - m_new)
        p = jnp.exp(s - m_new)
        l = l * alpha + jnp.sum(p, axis=1, keepdims=True)
        acc = acc * alpha + _dot(p, kvc)
        return m_new, l, acc

    m, l, acc = lax.fori_loop(0, nchunk, body, (m0, l0, acc0))
    o_ref[0] = acc / l


def _decode_attention(page_table, ql_s, qr_s, kvn, krn, cache_kv, cache_kr, *, gpages, nbuf):
    db, npages = page_table.shape
    heads, kvl = ql_s.shape[1:]
    rope = qr_s.shape[2]
    page = cache_kv.shape[1]
    nchunk = npages // gpages
    total = db * nchunk
    grid_spec = pltpu.PrefetchScalarGridSpec(
        num_scalar_prefetch=1,
        grid=(db,),
        in_specs=[pl.BlockSpec((1, heads, kvl), lambda b, pt: (b, 0, 0)),
                  pl.BlockSpec((1, heads, rope), lambda b, pt: (b, 0, 0)),
                  pl.BlockSpec((1, 1, kvl), lambda b, pt: (b, 0, 0)),
                  pl.BlockSpec((1, 1, rope), lambda b, pt: (b, 0, 0)),
                  pl.BlockSpec(memory_space=pl.ANY), pl.BlockSpec(memory_space=pl.ANY)],
        out_specs=pl.BlockSpec((1, heads, kvl), lambda b, pt: (b, 0, 0)),
        scratch_shapes=[pltpu.VMEM((nbuf, gpages, page, kvl), F32),
                        pltpu.VMEM((nbuf, gpages, page, rope), F32),
                        pltpu.SemaphoreType.DMA((nbuf,))],
    )
    return pl.pallas_call(
        functools.partial(_dec_kernel, nchunk=nchunk, gpages=gpages, nbuf=nbuf, total=total),
        grid_spec=grid_spec,
        out_shape=jax.ShapeDtypeStruct((db, heads, kvl), F32),
        compiler_params=_cparams(("arbitrary",)),
        name="mla_decode_attention",
    )(page_table, ql_s, qr_s, kvn, krn, cache_kv, cache_kr)


def _dec_out_kernel(o_ref, wuv_ref, g_ref, out_ref):
    heads = o_ref.shape[0]
    out_ref[...] = _uv_norm([o_ref[hh] for hh in range(heads)], wuv_ref, g_ref[...])


def _decode_out(o_hm, wuv, g):
    heads, rows, kvl = o_hm.shape
    vdim = wuv.shape[2]
    return pl.pallas_call(
        _dec_out_kernel,
        out_shape=jax.ShapeDtypeStruct((rows, heads * vdim), BF16),
        name="mla_decode_out",
    )(o_hm, wuv, g)


def _head_norm_gate(o, gn, rg):
    mu = jnp.mean(o, axis=-1, keepdims=True)
    var = jnp.mean(jnp.square(o - mu), axis=-1, keepdims=True)
    y = (o - mu) * lax.rsqrt(var + EPS) * gn
    return y * jax.nn.silu(rg)


def _ret_kernel(rq_ref, rk_ref, rv_ref, rg_ref, rkm_ref, rvm_ref, dec_ref, qd_ref, kd_ref, kdm_ref, gc_ref,
                gn_ref, o_ref, st_ref, *, chunk, nchunks):
    dec = dec_ref[0]
    qd = qd_ref[0]
    kd = kd_ref[0]
    gc = gc_ref[0]
    gn = gn_ref[...]
    s0 = _dot_tn((rkm_ref[...].astype(F32) * kdm_ref[0]).astype(BF16), rvm_ref[...])

    def body(c, state):
        r0 = pl.multiple_of(c * chunk, chunk)
        q = rq_ref[pl.ds(r0, chunk), :]
        k = rk_ref[pl.ds(r0, chunk), :]
        v = rv_ref[pl.ds(r0, chunk), :]
        rg = rg_ref[pl.ds(r0, chunk), :].astype(F32)
        scores = _dot_nt(q, k) * dec
        o = _dot(scores.astype(BF16), v) + _dot(q, state.astype(BF16)) * qd
        new_state = state * gc + _dot_tn((k.astype(F32) * kd).astype(BF16), v)
        o_ref[pl.ds(r0, chunk), :] = _head_norm_gate(o, gn, rg).astype(BF16)
        return new_state

    st_ref[0, 0, 0] = lax.fori_loop(0, nchunks, body, s0)


def _retention_prompt(rb, rkm, rvm, tabs, gn, *, batch, seq, heads, dk, chunk):
    dec, qd, kd, kdm, gc = tabs
    nmeta = rkm.shape[0]
    col = lambda off: (lambda b, h: (b, off * heads + h))
    tab = lambda b, h: (h, 0, 0)
    return pl.pallas_call(
        functools.partial(_ret_kernel, chunk=chunk, nchunks=seq // chunk),
        grid=(batch, heads),
        in_specs=[pl.BlockSpec((seq, dk), col(0)), pl.BlockSpec((seq, dk), col(1)),
                  pl.BlockSpec((seq, dk), col(2)), pl.BlockSpec((seq, dk), col(3)),
                  pl.BlockSpec((nmeta, dk), lambda b, h: (0, h)), pl.BlockSpec((nmeta, dk), lambda b, h: (0, h)),
                  pl.BlockSpec((1, chunk, chunk), tab), pl.BlockSpec((1, chunk, dk), tab),
                  pl.BlockSpec((1, chunk, dk), tab), pl.BlockSpec((1, nmeta, dk), tab),
                  pl.BlockSpec((1, dk, dk), tab),
                  pl.BlockSpec((1, dk), lambda b, h: (0, h))],
        out_specs=[pl.BlockSpec((seq, dk), lambda b, h: (b, h)),
                   pl.BlockSpec((1, 1, 1, dk, dk), lambda b, h: (0, b, h, 0, 0))],
        out_shape=[jax.ShapeDtypeStruct((batch * seq, heads * dk), BF16),
                   jax.ShapeDtypeStruct((1, batch, heads, dk, dk), F32)],
        compiler_params=_cparams(("parallel", "parallel")),
        name="retention_prompt",
    )(rb, rb, rb, rb, rkm, rvm, dec, qd, kd, kdm, gc, gn)


def _ret_s_kernel(q_ref, k_ref, v_ref, g_ref, gam_ref, gn_ref, st_ref, o_ref, ns_ref, *, heads, dk, bs):
    rowid = lax.broadcasted_iota(jnp.int32, (8, dk), 0)
    for hh in range(heads):
        sl = slice(hh * dk, (hh + 1) * dk)
        gam = gam_ref[hh]
        outs = []
        for i in range(bs):
            q = q_ref[i:i + 1, sl]
            k = k_ref[i:i + 1, sl]
            v = v_ref[i:i + 1, sl]
            state = st_ref[0, i, hh]
            k8 = jnp.where(rowid == 0, jnp.broadcast_to(k, (8, dk)), 0.0).astype(BF16)
            v8 = jnp.broadcast_to(v, (8, dk)).astype(BF16)
            ns_ref[0, i, hh] = state * gam[0:1, :] + _dot_tn(k8, v8)
            q8 = jnp.broadcast_to(q, (8, dk)).astype(BF16)
            qs = _dot(q8, state.astype(BF16))[0:1, :]
            qk = jnp.sum(q.astype(BF16).astype(F32) * k.astype(BF16).astype(F32), axis=1, keepdims=True)
            outs.append(qk * v.astype(BF16).astype(F32) + qs * gam[0:1, :])
        o = jnp.concatenate(outs, axis=0)
        o_ref[:, sl] = _head_norm_gate(o, gn_ref[:, sl], g_ref[:, sl]).astype(BF16)


def _retention_sample(rq, rk, rv, rg, gam, gn, state, *, heads, dk, bs):
    db = rq.shape[0]
    n = heads * dk
    row = lambda i: (i, 0)
    st_spec = pl.BlockSpec((1, bs, heads, dk, dk), lambda i: (0, i, 0, 0, 0))
    return pl.pallas_call(
        functools.partial(_ret_s_kernel, heads=heads, dk=dk, bs=bs),
        grid=(db // bs,),
        in_specs=[pl.BlockSpec((bs, n), row), pl.BlockSpec((bs, n), row), pl.BlockSpec((bs, n), row),
                  pl.BlockSpec((bs, n), row), pl.BlockSpec((heads, 8, dk), lambda i: (0, 0, 0)),
                  pl.BlockSpec((1, n), lambda i: (0, 0)), st_spec],
        out_specs=[pl.BlockSpec((bs, n), row), st_spec],
        out_shape=[jax.ShapeDtypeStruct((db, n), BF16), jax.ShapeDtypeStruct(state.shape, F32)],
        compiler_params=_cparams(("parallel",)),
        name="retention_sample",
    )(rq, rk, rv, rg, gam, gn, state)


def _out_kernel(om_ref, or_ref, x_ref, wo1_ref, wo2_ref, g_ref, wg_ref, bg_ref, wr_ref, br_ref,
                x1_ref, h2_ref, rt_ref, *, ngroups, nexperts):
    x1 = x_ref[...] + _dot(om_ref[...], wo1_ref[...]) + _dot(or_ref[...], wo2_ref[...])
    x1_ref[...] = x1
    hb = _rms(x1, g_ref[...]).astype(BF16)
    h2_ref[...] = hb.astype(F32)
    tm = x1.shape[0]
    lane = lax.broadcasted_iota(jnp.int32, (tm, LANES), 1)
    gl = jnp.where(lane < ngroups, _dot(hb, wg_ref[...]) + bg_ref[...], NEG)
    gmax = jnp.max(gl, axis=1, keepdims=True)
    gsum = jnp.sum(jnp.exp(gl - gmax), axis=1, keepdims=True)
    g_w = 1.0 / gsum
    g_idx = jnp.min(jnp.where(gl == gmax, lane, LANES), axis=1, keepdims=True)
    epg = nexperts // ngroups
    in_group = (lane >= g_idx * epg) & (lane < (g_idx + 1) * epg)
    el = jnp.where(in_group, _dot(hb, wr_ref[...]) + br_ref[...], NEG)
    emax = jnp.max(el, axis=1, keepdims=True)
    esum = jnp.sum(jnp.exp(el - emax), axis=1, keepdims=True)
    idx1 = jnp.min(jnp.where(el == emax, lane, LANES), axis=1, keepdims=True)
    el2 = jnp.where(lane == idx1, NEG, el)
    emax2 = jnp.max(el2, axis=1, keepdims=True)
    idx2 = jnp.min(jnp.where(el2 == emax2, lane, LANES), axis=1, keepdims=True)
    p1 = 1.0 / esum
    p2 = jnp.exp(emax2 - emax) / esum
    w1 = p1 / (p1 + p2) * g_w
    w2 = p2 / (p1 + p2) * g_w
    rt_ref[...] = jnp.where(lane == 0, w1, jnp.where(lane == 1, w2, jnp.where(
        lane == 2, idx1.astype(F32), jnp.where(lane == 3, idx2.astype(F32), 0.0))))


def _out_proj(om, orr, x, wo1, wo2, g, wg, bg, wr, br, *, tm, ngroups, nexperts):
    rows, d = x.shape
    n1, n2 = om.shape[1], orr.shape[1]
    row = lambda i: (i, 0)
    fix = lambda i: (0, 0)
    return pl.pallas_call(
        functools.partial(_out_kernel, ngroups=ngroups, nexperts=nexperts),
        grid=(rows // tm,),
        in_specs=[pl.BlockSpec((tm, n1), row), pl.BlockSpec((tm, n2), row), pl.BlockSpec((tm, d), row),
                  pl.BlockSpec((n1, d), fix), pl.BlockSpec((n2, d), fix), pl.BlockSpec((1, d), fix),
                  pl.BlockSpec((d, LANES), fix), pl.BlockSpec((1, LANES), fix),
                  pl.BlockSpec((d, LANES), fix), pl.BlockSpec((1, LANES), fix)],
        out_specs=[pl.BlockSpec((tm, d), row), pl.BlockSpec((tm, d), row), pl.BlockSpec((tm, LANES), row)],
        out_shape=[jax.ShapeDtypeStruct((rows, d), F32), jax.ShapeDtypeStruct((rows, d), F32),
                   jax.ShapeDtypeStruct((rows, LANES), F32)],
        compiler_params=_cparams(("parallel",)),
        name="out_proj_router",
    )(om, orr, x, wo1, wo2, g, wg, bg, wr, br)


def _moe_kernel(te_ref, nu_ref, rtok_ref, h2a_hbm, h2b_hbm, gate_ref, wg_ref, wu_ref, wd_ref, ys_ref,
                xbuf, sem, *, tm, nmain):
    i = pl.program_id(0)
    nused = nu_ref[0]

    def row_copy(src_hbm, tok, slot, r):
        return pltpu.make_async_copy(src_hbm.at[pl.ds(tok, 1), :], xbuf.at[slot, pl.ds(r, 1), :], sem.at[slot])

    def start(tile, slot):
        def body(r, carry):
            tok = rtok_ref[tile * tm + r]

            @pl.when(tok < nmain)
            def _():
                row_copy(h2a_hbm, tok, slot, r).start()

            @pl.when(tok >= nmain)
            def _():
                row_copy(h2b_hbm, tok - nmain, slot, r).start()
            return carry
        lax.fori_loop(0, tm, body, 0)

    @pl.when(i == 0)
    def _():
        start(0, 0)

    @pl.when(i + 1 < nused)
    def _():
        start(i + 1, (i + 1) % 2)

    @pl.when(i < nused)
    def _():
        slot = i % 2

        def wbody(r, carry):
            row_copy(h2a_hbm, 0, slot, r).wait()
            return carry
        lax.fori_loop(0, tm, wbody, 0)
        x = xbuf[slot]
        hg = _dot(x, wg_ref[0])
        hu = _dot(x, wu_ref[0])
        a = jax.nn.silu(hg) * hu * gate_ref[...]
        ys_ref[...] = _dot(a, wd_ref[0])

    @pl.when(i >= nused)
    def _():
        ys_ref[...] = jnp.zeros_like(ys_ref)


def _moe(tile_expert, nused, row_tok, h2_main, h2_aux, row_gate, w_gate, w_up, w_down, *, tm):
    ntiles = tile_expert.shape[0]
    d = h2_main.shape[1]
    f = w_gate.shape[2]
    grid_spec = pltpu.PrefetchScalarGridSpec(
        num_scalar_prefetch=3,
        grid=(ntiles,),
        in_specs=[pl.BlockSpec(memory_space=pl.ANY), pl.BlockSpec(memory_space=pl.ANY),
                  pl.BlockSpec((tm, 1), lambda i, te, nu, rt: (i, 0)),
                  pl.BlockSpec((1, d, f), lambda i, te, nu, rt: (te[i], 0, 0)),
                  pl.BlockSpec((1, d, f), lambda i, te, nu, rt: (te[i], 0, 0)),
                  pl.BlockSpec((1, f, d), lambda i, te, nu, rt: (te[i], 0, 0))],
        out_specs=pl.BlockSpec((tm, d), lambda i, te, nu, rt: (i, 0)),
        scratch_shapes=[pltpu.VMEM((2, tm, d), F32), pltpu.SemaphoreType.DMA((2,))],
    )
    return pl.pallas_call(
        functools.partial(_moe_kernel, tm=tm, nmain=h2_main.shape[0]),
        grid_spec=grid_spec,
        out_shape=jax.ShapeDtypeStruct((ntiles * tm, d), F32),
        compiler_params=_cparams(("arbitrary",)),
        name="moe_experts",
    )(tile_expert, nused, row_tok, h2_main, h2_aux, row_gate, w_gate, w_up, w_down)


def _comb_kernel(pos_ref, x1_ref, ys_hbm, g_ref, y_ref, ybuf, sem, *, tm, base, ntiles):
    i = pl.program_id(0)

    def start(tile, slot):
        def body(r, carry):
            p = (base + tile * tm + r) * 2
            pltpu.make_async_copy(ys_hbm.at[pl.ds(pos_ref[p], 1), :], ybuf.at[slot, 0, pl.ds(r, 1), :],
                                  sem.at[slot]).start()
            pltpu.make_async_copy(ys_hbm.at[pl.ds(pos_ref[p + 1], 1), :], ybuf.at[slot, 1, pl.ds(r, 1), :],
                                  sem.at[slot]).start()
            return carry
        lax.fori_loop(0, tm, body, 0)

    @pl.when(i == 0)
    def _():
        start(0, 0)

    @pl.when(i + 1 < ntiles)
    def _():
        start(i + 1, (i + 1) % 2)

    slot = i % 2

    def wbody(r, carry):
        pltpu.make_async_copy(ys_hbm.at[pl.ds(0, 1), :], ybuf.at[slot, 0, pl.ds(r, 1), :], sem.at[slot]).wait()
        pltpu.make_async_copy(ys_hbm.at[pl.ds(0, 1), :], ybuf.at[slot, 1, pl.ds(r, 1), :], sem.at[slot]).wait()
        return carry
    lax.fori_loop(0, tm, wbody, 0)
    x = x1_ref[...] + (ybuf[slot, 0] + ybuf[slot, 1])
    y_ref[...] = _rms(x, g_ref[...])


def _combine(pos, x1, ys, g, *, tm, base, rows):
    d = x1.shape[1]
    ntiles = rows // tm
    grid_spec = pltpu.PrefetchScalarGridSpec(
        num_scalar_prefetch=1,
        grid=(ntiles,),
        in_specs=[pl.BlockSpec((tm, d), lambda i, pos: (i, 0)), pl.BlockSpec(memory_space=pl.ANY),
                  pl.BlockSpec((1, d), lambda i, pos: (0, 0))],
        out_specs=pl.BlockSpec((tm, d), lambda i, pos: (i, 0)),
        scratch_shapes=[pltpu.VMEM((2, 2, tm, d), F32), pltpu.SemaphoreType.DMA((2,))],
    )
    return pl.pallas_call(
        functools.partial(_comb_kernel, tm=tm, base=base, ntiles=ntiles),
        grid_spec=grid_spec,
        out_shape=jax.ShapeDtypeStruct((rows, d), F32),
        compiler_params=_cparams(("arbitrary",)),
        name="moe_combine_norm",
    )(pos, x1, ys, g)


def _rope_tables(pos, dim):
    half = dim // 2
    inv = jnp.exp(jnp.arange(half, dtype=F32) * (-2.0 * math.log(ROPE_BASE) / dim))
    ang = pos.astype(F32)[:, None] * inv[None, :]
    return jnp.cos(ang), jnp.sin(ang)


def _rot_cols(w):
    half = w.shape[-1] // 2
    return jnp.concatenate([-w[..., half:], w[..., :half]], axis=-1)


def _pick_tile(rows, target):
    t = min(rows, target)
    while rows % t:
        t //= 2
    return t


def _dispatch(route, ntok, nexperts, tm, ntiles):
    ids = route[:, 2:4].astype(jnp.int32).reshape(-1)
    wts = route[:, 0:2].reshape(-1)
    npair = 2 * ntok
    order = jnp.argsort(ids, stable=True).astype(jnp.int32)
    sorted_ids = ids[order]
    ex = jnp.arange(nexperts, dtype=jnp.int32)
    starts = jnp.searchsorted(sorted_ids, ex, side="left").astype(jnp.int32)
    ends = jnp.searchsorted(sorted_ids, ex, side="right").astype(jnp.int32)
    counts = ends - starts
    pcounts = (counts + tm - 1) // tm * tm
    pend = jnp.cumsum(pcounts)
    poff = pend - pcounts
    nused = (pend[-1] // tm).astype(jnp.int32)
    tiles = jnp.arange(ntiles, dtype=jnp.int32)
    tile_expert = jnp.minimum(jnp.searchsorted(pend, tiles * tm, side="right"), nexperts - 1).astype(jnp.int32)
    rows = jnp.arange(ntiles * tm, dtype=jnp.int32)
    re = tile_expert[rows // tm]
    j = rows - poff[re]
    valid = (j < counts[re]) & (rows // tm < nused)
    src = order[jnp.clip(starts[re] + j, 0, npair - 1)]
    row_tok = jnp.where(valid, src // 2, 0).astype(jnp.int32)
    row_gate = jnp.where(valid, wts[src], 0.0).astype(F32)[:, None]
    k = jnp.arange(npair, dtype=jnp.int32)
    pos_sorted = poff[sorted_ids] + (k - starts[sorted_ids])
    pos = jnp.zeros((npair,), jnp.int32).at[order].set(pos_sorted.astype(jnp.int32))
    return tile_expert, nused.reshape(1), row_tok, row_gate, pos


def kernel(x_prompt, x_sample, cache_kv_latent, cache_k_rope, state_retention, page_table, meta_tokens,
           ln_mix_g, w_in, q_norm_g, w_uq, w_uk, kv_norm_g, w_uv, mla_out_g, ret_gn_g, w_o, ln_ffn_g,
           w_group, b_group, w_router, b_router, w_gate, w_up, w_down, final_g):
    batch, seq, d = x_prompt.shape
    db, dec_seq, _ = x_sample.shape
    depth = w_in.shape[0]
    assert depth == 1 and dec_seq == 1
    ql = q_norm_g.shape[1]
    kvl = kv_norm_g.shape[1]
    mh, nr = w_uq.shape[2], w_uq.shape[3]
    nope = w_uk.shape[3]
    rope = nr - nope
    vdim = w_uv.shape[3]
    rdim = ret_gn_g.shape[1]
    dk = state_retention.shape[3]
    rh = rdim // dk
    ngroups = w_group.shape[2]
    nexperts = w_router.shape[2]
    page = cache_kv_latent.shape[2]
    npages = page_table.shape[1]
    past_len = npages * page
    chunk = ROW_TILE
    nmain = batch * seq
    naux = -(-(db + N_META) // LANES) * LANES
    ts = -(-db // ROW_TILE) * ROW_TILE
    assert seq % 256 == 0 and ts <= naux and db % 8 == 0 and rh == mh

    wi = w_in[0]
    o_kr = ql + kvl
    o_ret = o_kr + rope
    w_a = jnp.concatenate([wi[:, :o_ret], _rot_cols(wi[:, o_kr:o_ret])], axis=1).astype(BF16)
    w_b = wi[:, o_ret:].astype(BF16)
    wq = w_uq[0]
    wq_rope = wq[:, :, nope:]
    wuq = jnp.concatenate([wq[:, :, :nope].reshape(ql, mh * nope), wq_rope.reshape(ql, mh * rope),
                           _rot_cols(wq_rope).reshape(ql, mh * rope)], axis=1).astype(BF16)
    wukt = jnp.transpose(w_uk[0], (1, 2, 0)).astype(BF16)
    wuv = jnp.transpose(w_uv[0], (1, 0, 2)).astype(BF16)
    wo = w_o[0].astype(BF16)
    wo1, wo2 = wo[:mh * vdim], wo[mh * vdim:]
    wg_pad = jnp.zeros((d, LANES), F32).at[:, :ngroups].set(w_group[0]).astype(BF16)
    bg_pad = jnp.zeros((1, LANES), F32).at[0, :ngroups].set(b_group[0])
    wr_pad = jnp.zeros((d, LANES), F32).at[:, :nexperts].set(w_router[0]).astype(BF16)
    br_pad = jnp.zeros((1, LANES), F32).at[0, :nexperts].set(b_router[0])

    pos_main = N_META + jnp.arange(nmain, dtype=jnp.int32) % seq
    pos_aux = jnp.concatenate([jnp.full((db,), past_len, jnp.int32), jnp.arange(N_META, dtype=jnp.int32),
                               jnp.zeros((naux - db - N_META,), jnp.int32)])

    def tables(pos):
        c64, s64 = _rope_tables(pos, rope)
        c128, s128 = _rope_tables(pos, dk)
        cs = jnp.concatenate([c64, c64], axis=1)
        sn = jnp.concatenate([s64, s64], axis=1)
        return (cs, sn, jnp.tile(cs, (1, mh)), jnp.tile(sn, (1, mh)),
                jnp.concatenate([c128, c128], axis=1), jnp.concatenate([-s128, s128], axis=1))

    log_g = jnp.log1p(-jnp.exp2(-5.0 - jnp.arange(rh, dtype=F32)))
    n = jnp.arange(chunk, dtype=F32)
    diff = n[:, None] - n[None, :]
    t_dec = jnp.where(diff >= 0, jnp.exp(log_g[:, None, None] * jnp.maximum(diff, 0.0)), 0.0)
    t_qd = jnp.broadcast_to(jnp.exp((n[None, :] + 1.0) * log_g[:, None])[:, :, None], (rh, chunk, dk))
    k_decay = jnp.exp((chunk - 1.0 - n)[None, :] * log_g[:, None])
    t_kd = jnp.broadcast_to(k_decay[:, :, None], (rh, chunk, dk))
    t_kdm = jnp.broadcast_to(k_decay[:, chunk - N_META:, None], (rh, N_META, dk))
    t_gc = jnp.broadcast_to(jnp.exp(chunk * log_g)[:, None, None], (rh, dk, dk))
    t_gam = jnp.broadcast_to(jnp.exp(log_g)[:, None, None], (rh, 8, dk))

    x_main = x_prompt.reshape(nmain, d)
    x_aux = jnp.concatenate([x_sample.reshape(db, d), meta_tokens.astype(F32),
                             jnp.zeros((naux - db - N_META, d), F32)], axis=0)
    g_mix = ln_mix_g[0][None]
    scale = float(nope + rope) ** -0.5
    tm_main = _pick_tile(nmain, 512)
    tm_aux = _pick_tile(naux, 512)

    def token_front(x, pos, tm):
        cs, sn, cs8, sn8, cos_r, sin_r = tables(pos)
        cq, kv, kvb, kr, krb = _in_a(x, g_mix, w_a, q_norm_g, kv_norm_g, cs, sn,
                                     tm=tm, ql=ql, kvl=kvl, rope=rope)
        rb = _in_b(x, g_mix, w_b, cos_r, sin_r, tm=tm, heads=rh, dk=dk)
        q_lat, q_rope = _q_proj(cq, wuq, wukt, cs8, sn8, tm=tm, heads=mh, nope=nope, rope=rope, scale=scale)
        return kv, kvb, kr, krb, rb, q_lat, q_rope

    kv_m, kvb_m, kr_m, krb_m, rb_m, ql_m, qr_m = token_front(x_main, pos_main, tm_main)
    kv_a, kvb_a, kr_a, krb_a, rb_a, ql_a, qr_a = token_front(x_aux, pos_aux, tm_aux)
    meta = slice(db, db + N_META)

    om_m = _attention(ql_m, qr_m, kvb_m, krb_m, kvb_a[meta], krb_a[meta], wuv, mla_out_g,
                      batch=batch, seq=seq, tq=ROW_TILE, tk=256)
    rdk = rh * dk
    or_m, st_p = _retention_prompt(rb_m, rb_a[meta, rdk:2 * rdk], rb_a[meta, 2 * rdk:3 * rdk],
                                   (t_dec, t_qd, t_kd, t_kdm, t_gc), ret_gn_g,
                                   batch=batch, seq=seq, heads=rh, dk=dk, chunk=chunk)

    ql_s = jnp.transpose(ql_a[:, :db], (1, 0, 2))
    qr_s = jnp.transpose(qr_a[:, :db], (1, 0, 2))
    o_s = _decode_attention(page_table, ql_s, qr_s, kv_a[:db, None, :], kr_a[:db, None, :],
                            cache_kv_latent[0], cache_k_rope[0], gpages=min(8, npages), nbuf=3)
    om_s = _decode_out(jnp.transpose(o_s, (1, 0, 2)), wuv, mla_out_g)
    rs = rb_a[:db].astype(F32)
    or_s, st_s = _retention_sample(rs[:, :rdk], rs[:, rdk:2 * rdk], rs[:, 2 * rdk:3 * rdk], rs[:, 3 * rdk:],
                                   t_gam, ret_gn_g, state_retention, heads=rh, dk=dk, bs=8)
    pad_rows = lambda a: jnp.concatenate([a, jnp.zeros((naux - db, a.shape[1]), a.dtype)], axis=0)
    om_a, or_a = pad_rows(om_s), pad_rows(or_s)

    g_ffn = ln_ffn_g[0][None]
    outp = functools.partial(_out_proj, ngroups=ngroups, nexperts=nexperts)
    x1_m, h2_m, rt_m = outp(om_m, or_m, x_main, wo1, wo2, g_ffn, wg_pad, bg_pad, wr_pad, br_pad,
                            tm=_pick_tile(nmain, 256))
    x1_a, h2_a, rt_a = outp(om_a, or_a, x_aux, wo1, wo2, g_ffn, wg_pad, bg_pad, wr_pad, br_pad,
                            tm=_pick_tile(naux, 256))

    ntok = nmain + ts
    route = jnp.concatenate([rt_m[:, :4], rt_a[:ts, :4]], axis=0)
    tm_e = ROW_TILE
    ntiles = -(-(2 * ntok + nexperts * (tm_e - 1)) // tm_e)
    tile_expert, nused, row_tok, row_gate, pos = _dispatch(route, ntok, nexperts, tm_e, ntiles)
    ys = _moe(tile_expert, nused, row_tok, h2_m, h2_a, row_gate, w_gate[0], w_up[0], w_down[0], tm=tm_e)
    fg = final_g[None]
    y_m = _combine(pos, x1_m, ys, fg, tm=ROW_TILE, base=0, rows=nmain)
    y_a = _combine(pos, x1_a, ys, fg, tm=ROW_TILE, base=nmain, rows=ts)

    def with_meta(a_main, a_aux):
        w = a_main.shape[1]
        m = jnp.broadcast_to(a_aux[meta][None], (batch, N_META, w))
        return jnp.concatenate([m, a_main.reshape(batch, seq, w)], axis=1)[None]

    return (y_m.reshape(batch, seq, d), y_a[:db].reshape(db, 1, d),
            with_meta(kv_m, kv_a), with_meta(kr_m, kr_a), st_p,
            kv_a[:db].reshape(1, db, 1, kvl), kr_a[:db].reshape(1, db, 1, rope), st_s)
```

```python
import functools
import math

import jax
import jax.numpy as jnp
from jax import lax
from jax.experimental import pallas as pl
from jax.experimental.pallas import tpu as pltpu

F32 = jnp.float32
BF16 = jnp.bfloat16

N_META = 16
ROPE_BASE = 10000.0
EPS = 1e-6
NEG = -1e30
EXPERTS_PER_GROUP = 8
LANES = 128
ROW_TILE = 128
VMEM_LIMIT = 56 * 1024 * 1024


def _cparams(sem, vmem=VMEM_LIMIT):
    return pltpu.CompilerParams(dimension_semantics=sem, vmem_limit_bytes=vmem)


def _rms(x, g):
    return x * lax.rsqrt(jnp.mean(x * x, axis=-1, keepdims=True) + EPS) * g


def _dot(a, b):
    return jnp.dot(a, b, preferred_element_type=F32)


def _dot_nt(a, b):
    return lax.dot_general(a, b, (((1,), (1,)), ((), ())), preferred_element_type=F32)


def _dot_tn(a, b):
    return lax.dot_general(a, b, (((0,), (0,)), ((), ())), preferred_element_type=F32)


def _in_a_kernel(x_ref, g_ref, w_ref, qg_ref, kvg_ref, cs_ref, sn_ref,
                 cq_ref, kv_ref, kvb_ref, kr_ref, krb_ref, *, ql, kvl, rope):
    h = _rms(x_ref[...], g_ref[...]).astype(BF16)
    c = _dot(h, w_ref[...])
    cq_ref[...] = _rms(c[:, :ql], qg_ref[...]).astype(BF16)
    kv = _rms(c[:, ql:ql + kvl], kvg_ref[...])
    kv_ref[...] = kv
    kvb_ref[...] = kv.astype(BF16)
    a = c[:, ql + kvl:ql + kvl + rope]
    b = c[:, ql + kvl + rope:]
    kr = a * cs_ref[...] + b * sn_ref[...]
    kr_ref[...] = kr
    krb_ref[...] = kr.astype(BF16)


def _in_a(x, g, w_a, qg, kvg, cs, sn, *, tm, ql, kvl, rope):
    rows, d = x.shape
    n = w_a.shape[1]
    row = lambda i: (i, 0)
    fix = lambda i: (0, 0)
    return pl.pallas_call(
        functools.partial(_in_a_kernel, ql=ql, kvl=kvl, rope=rope),
        grid=(rows // tm,),
        in_specs=[pl.BlockSpec((tm, d), row), pl.BlockSpec((1, d), fix), pl.BlockSpec((d, n), fix),
                  pl.BlockSpec((1, ql), fix), pl.BlockSpec((1, kvl), fix),
                  pl.BlockSpec((tm, rope), row), pl.BlockSpec((tm, rope), row)],
        out_specs=[pl.BlockSpec((tm, ql), row), pl.BlockSpec((tm, kvl), row), pl.BlockSpec((tm, kvl), row),
                   pl.BlockSpec((tm, rope), row), pl.BlockSpec((tm, rope), row)],
        out_shape=[jax.ShapeDtypeStruct((rows, ql), BF16), jax.ShapeDtypeStruct((rows, kvl), F32),
                   jax.ShapeDtypeStruct((rows, kvl), BF16), jax.ShapeDtypeStruct((rows, rope), F32),
                   jax.ShapeDtypeStruct((rows, rope), BF16)],
        compiler_params=_cparams(("parallel",)),
        name="in_proj_a",
    )(x, g, w_a, qg, kvg, cs, sn)


def _in_b_kernel(x_ref, g_ref, w_ref, cos_ref, sin_ref, out_ref, h_scr, *, heads, dk):
    j = pl.program_id(1)

    @pl.when(j == 0)
    def _():
        h_scr[...] = _rms(x_ref[...], g_ref[...]).astype(BF16)

    c = _dot(h_scr[...], w_ref[...])

    @pl.when(j < 2)
    def _():
        scale = jnp.where(j == 1, dk ** -0.5, 1.0).astype(F32)
        cos = cos_ref[...]
        sin = sin_ref[...]
        for hh in range(heads):
            blk = c[:, hh * dk:(hh + 1) * dk]
            r = blk * cos + pltpu.roll(blk, dk // 2, 1) * sin
            out_ref[:, hh * dk:(hh + 1) * dk] = (r * scale).astype(BF16)

    @pl.when(j >= 2)
    def _():
        out_ref[...] = c.astype(BF16)


def _in_b(x, g, w_b, cos, sin, *, tm, heads, dk):
    rows, d = x.shape
    n = heads * dk
    return pl.pallas_call(
        functools.partial(_in_b_kernel, heads=heads, dk=dk),
        grid=(rows // tm, 4),
        in_specs=[pl.BlockSpec((tm, d), lambda i, j: (i, 0)), pl.BlockSpec((1, d), lambda i, j: (0, 0)),
                  pl.BlockSpec((d, n), lambda i, j: (0, j)),
                  pl.BlockSpec((tm, dk), lambda i, j: (i, 0)), pl.BlockSpec((tm, dk), lambda i, j: (i, 0))],
        out_specs=pl.BlockSpec((tm, n), lambda i, j: (i, j)),
        out_shape=jax.ShapeDtypeStruct((rows, 4 * n), BF16),
        scratch_shapes=[pltpu.VMEM((tm, d), BF16)],
        compiler_params=_cparams(("parallel", "arbitrary")),
        name="in_proj_b",
    )(x, g, w_b, cos, sin)


def _q_kernel(cq_ref, wuq_ref, wukt_ref, cos_ref, sin_ref, ql_ref, qr_ref, *, heads, nope, rope, scale):
    qf = _dot(cq_ref[...], wuq_ref[...])
    o1 = heads * nope
    o2 = o1 + heads * rope
    qr = (qf[:, o1:o2] * cos_ref[...] + qf[:, o2:] * sin_ref[...]) * scale
    for hh in range(heads):
        qn = qf[:, hh * nope:(hh + 1) * nope].astype(BF16)
        ql_ref[hh] = (_dot(qn, wukt_ref[hh]) * scale).astype(BF16)
        qr_ref[hh] = qr[:, hh * rope:(hh + 1) * rope].astype(BF16)


def _q_proj(cq, wuq, wukt, cos, sin, *, tm, heads, nope, rope, scale):
    rows, ql = cq.shape
    kvl = wukt.shape[2]
    nq = wuq.shape[1]
    return pl.pallas_call(
        functools.partial(_q_kernel, heads=heads, nope=nope, rope=rope, scale=scale),
        grid=(rows // tm,),
        in_specs=[pl.BlockSpec((tm, ql), lambda i: (i, 0)), pl.BlockSpec((ql, nq), lambda i: (0, 0)),
                  pl.BlockSpec((heads, nope, kvl), lambda i: (0, 0, 0)),
                  pl.BlockSpec((tm, heads * rope), lambda i: (i, 0)),
                  pl.BlockSpec((tm, heads * rope), lambda i: (i, 0))],
        out_specs=[pl.BlockSpec((heads, tm, kvl), lambda i: (0, i, 0)),
                   pl.BlockSpec((heads, tm, rope), lambda i: (0, i, 0))],
        out_shape=[jax.ShapeDtypeStruct((heads, rows, kvl), BF16),
                   jax.ShapeDtypeStruct((heads, rows, rope), BF16)],
        compiler_params=_cparams(("parallel",)),
        name="q_proj",
    )(cq, wuq, wukt, cos, sin)


def _uv_norm(o_heads, wuv_ref, g):
    parts = [_dot(o.astype(BF16), wuv_ref[hh]) for hh, o in enumerate(o_heads)]
    return _rms(jnp.concatenate(parts, axis=1), g).astype(BF16)


def _widen(x, n):
    if n <= LANES:
        return x[:, :n]
    return jnp.concatenate([x] * (n // LANES), axis=1)


def _att_kernel(ql_ref, qr_ref, kv_ref, kr_ref, kvm_ref, krm_ref, wuv_ref, g_ref, o_ref,
                m_scr, l_scr, acc_scr, *, heads, tq, tk, nk):
    qi = pl.program_id(1)
    kj = pl.program_id(2)
    kvl = ql_ref.shape[2]
    rows = heads * tq
    ql = ql_ref[...].reshape(rows, kvl)
    qr = qr_ref[...].reshape(rows, qr_ref.shape[2])

    def update(s, kv, first):
        m_cur = jnp.max(s, axis=1, keepdims=True)
        if first:
            m_new = jnp.broadcast_to(m_cur, (rows, LANES))
        else:
            m_prev = m_scr[...]
            m_new = jnp.maximum(m_prev, m_cur)
        p = jnp.exp(s - _widen(m_new, s.shape[1]))
        p_sum = jnp.sum(p, axis=1, keepdims=True)
        pv = _dot(p.astype(BF16), kv)
        if first:
            l_scr[...] = jnp.broadcast_to(p_sum, (rows, LANES))
            acc_scr[...] = pv
        else:
            alpha = jnp.exp(m_prev - m_new)
            l_scr[...] = alpha * l_scr[...] + p_sum
            acc_scr[...] = acc_scr[...] * _widen(alpha, kvl) + pv
        m_scr[...] = m_new

    @pl.when(kj == 0)
    def _():
        kvm = kvm_ref[...]
        update(_dot_nt(ql, kvm) + _dot_nt(qr, krm_ref[...]), kvm, True)

    last = (qi * tq + tq - 1) // tk

    @pl.when(kj < last)
    def _():
        kv = kv_ref[...]
        update(_dot_nt(ql, kv) + _dot_nt(qr, kr_ref[...]), kv, False)

    @pl.when(kj == last)
    def _():
        kv = kv_ref[...]
        s = _dot_nt(ql, kv) + _dot_nt(qr, kr_ref[...])
        qpos = qi * tq + lax.broadcasted_iota(jnp.int32, (rows, tk), 0) % tq
        kpos = kj * tk + lax.broadcasted_iota(jnp.int32, (rows, tk), 1)
        update(jnp.where(kpos <= qpos, s, NEG), kv, False)

    @pl.when(kj == nk - 1)
    def _():
        o = acc_scr[...] / _widen(l_scr[...], kvl)
        o_ref[...] = _uv_norm([o[hh * tq:(hh + 1) * tq] for hh in range(heads)], wuv_ref, g_ref[...])


def _attention(ql, qr, kvb, krb, kvm, krm, wuv, g, *, batch, seq, tq, tk):
    heads, rows, kvl = ql.shape
    rope = qr.shape[2]
    vdim = wuv.shape[2]
    nq, nk = seq // tq, seq // tk
    nmeta = kvm.shape[0]

    def kidx(b, qi, kj):
        return (b * nk + jnp.minimum(kj, (qi * tq + tq - 1) // tk), 0)

    return pl.pallas_call(
        functools.partial(_att_kernel, heads=heads, tq=tq, tk=tk, nk=nk),
        grid=(batch, nq, nk),
        in_specs=[pl.BlockSpec((heads, tq, kvl), lambda b, qi, kj: (0, b * nq + qi, 0)),
                  pl.BlockSpec((heads, tq, rope), lambda b, qi, kj: (0, b * nq + qi, 0)),
                  pl.BlockSpec((tk, kvl), kidx), pl.BlockSpec((tk, rope), kidx),
                  pl.BlockSpec((nmeta, kvl), lambda b, qi, kj: (0, 0)),
                  pl.BlockSpec((nmeta, rope), lambda b, qi, kj: (0, 0)),
                  pl.BlockSpec((heads, kvl, vdim), lambda b, qi, kj: (0, 0, 0)),
                  pl.BlockSpec((1, heads * vdim), lambda b, qi, kj: (0, 0))],
        out_specs=pl.BlockSpec((tq, heads * vdim), lambda b, qi, kj: (b * nq + qi, 0)),
        out_shape=jax.ShapeDtypeStruct((rows, heads * vdim), BF16),
        scratch_shapes=[pltpu.VMEM((heads * tq, LANES), F32), pltpu.VMEM((heads * tq, LANES), F32),
                        pltpu.VMEM((heads * tq, kvl), F32)],
        compiler_params=_cparams(("parallel", "parallel", "arbitrary")),
        name="mla_prompt_attention",
    )(ql, qr, kvb, krb, kvm, krm, wuv, g)


def _dec_kernel(pt_ref, ql_ref, qr_ref, kvn_ref, krn_ref, ckv_hbm, ckr_hbm, o_ref,
                kvbuf, krbuf, sem, *, nchunk, gpages, nbuf, total):
    b = pl.program_id(0)
    page = kvbuf.shape[2]
    kvl = kvbuf.shape[3]

    def copies(g, slot):
        bb = g // nchunk
        c0 = (g % nchunk) * gpages
        out = []
        for p in range(gpages):
            pg = pt_ref[bb, c0 + p]
            out.append(pltpu.make_async_copy(ckv_hbm.at[pg], kvbuf.at[slot, p], sem.at[slot]))
            out.append(pltpu.make_async_copy(ckr_hbm.at[pg], krbuf.at[slot, p], sem.at[slot]))
        return out

    def start(g, slot):
        for cp in copies(g, slot):
            cp.start()

    @pl.when(b == 0)
    def _():
        for g0 in range(min(nbuf - 1, total)):
            start(g0, g0 % nbuf)

    ql = ql_ref[0].astype(F32)
    qr = qr_ref[0].astype(F32)
    kvn = kvn_ref[0]
    krn = krn_ref[0]
    heads = ql.shape[0]
    m0 = jnp.sum(ql * kvn, axis=1, keepdims=True) + jnp.sum(qr * krn, axis=1, keepdims=True)
    l0 = jnp.ones((heads, 1), F32)
    acc0 = jnp.broadcast_to(kvn, (heads, kvl))

    def body(c, carry):
        m, l, acc = carry
        g = b * nchunk + c
        slot = g % nbuf
        for cp in copies(g, slot):
            cp.wait()
        nxt = g + nbuf - 1

        @pl.when(nxt < total)
        def _():
            start(nxt, nxt % nbuf)

        kvc = kvbuf[slot].reshape(gpages * page, kvl)
        krt = jnp.concatenate([krbuf[slot, p] for p in range(gpages)], axis=1)
        s = _dot_nt(ql, kvc) + _dot(qr, krt)
        m_new = jnp.maximum(m, jnp.max(s, axis=1, keepdims=True))
        alpha = jnp.exp(m - m_new)
        p = jnp.exp(s - m_new)
        l = l * alpha + jnp.sum(p, axis=1, keepdims=True)
        acc = acc * alpha + _dot(p, kvc)
        return m_new, l, acc

    m, l, acc = lax.fori_loop(0, nchunk, body, (m0, l0, acc0))
    o_ref[0] = acc / l


def _decode_attention(page_table, ql_s, qr_s, kvn, krn, cache_kv, cache_krt, *, gpages, nbuf):
    db, npages = page_table.shape
    heads, kvl = ql_s.shape[1:]
    rope = qr_s.shape[2]
    page = cache_kv.shape[1]
    nchunk = npages // gpages
    total = db * nchunk
    grid_spec = pltpu.PrefetchScalarGridSpec(
        num_scalar_prefetch=1,
        grid=(db,),
        in_specs=[pl.BlockSpec((1, heads, kvl), lambda b, pt: (b, 0, 0)),
                  pl.BlockSpec((1, heads, rope), lambda b, pt: (b, 0, 0)),
                  pl.BlockSpec((1, 1, kvl), lambda b, pt: (b, 0, 0)),
                  pl.BlockSpec((1, 1, rope), lambda b, pt: (b, 0, 0)),
                  pl.BlockSpec(memory_space=pl.ANY), pl.BlockSpec(memory_space=pl.ANY)],
        out_specs=pl.BlockSpec((1, heads, kvl), lambda b, pt: (b, 0, 0)),
        scratch_shapes=[pltpu.VMEM((nbuf, gpages, page, kvl), F32),
                        pltpu.VMEM((nbuf, gpages, rope, page), F32),
                        pltpu.SemaphoreType.DMA((nbuf,))],
    )
    return pl.pallas_call(
        functools.partial(_dec_kernel, nchunk=nchunk, gpages=gpages, nbuf=nbuf, total=total),
        grid_spec=grid_spec,
        out_shape=jax.ShapeDtypeStruct((db, heads, kvl), F32),
        compiler_params=_cparams(("arbitrary",)),
        name="mla_decode_attention",
    )(page_table, ql_s, qr_s, kvn, krn, cache_kv, cache_krt)


def _dec_out_kernel(o_ref, wuv_ref, g_ref, out_ref):
    heads = o_ref.shape[0]
    out_ref[...] = _uv_norm([o_ref[hh] for hh in range(heads)], wuv_ref, g_ref[...])


def _decode_out(o_hm, wuv, g):
    heads, rows, kvl = o_hm.shape
    vdim = wuv.shape[2]
    return pl.pallas_call(
        _dec_out_kernel,
        out_shape=jax.ShapeDtypeStruct((rows, heads * vdim), BF16),
        name="mla_decode_out",
    )(o_hm, wuv, g)


def _head_norm_gate(o, gn, rg):
    mu = jnp.mean(o, axis=-1, keepdims=True)
    var = jnp.mean(jnp.square(o - mu), axis=-1, keepdims=True)
    y = (o - mu) * lax.rsqrt(var + EPS) * gn
    return y * jax.nn.silu(rg)


def _ret_kernel(rq_ref, rk_ref, rv_ref, rg_ref, rkm_ref, rvm_ref, dec_ref, qd_ref, kd_ref, kdm_ref, gc_ref,
                gn_ref, o_ref, st_ref, *, chunk, nchunks):
    dec = dec_ref[0]
    qd = qd_ref[0]
    kd = kd_ref[0]
    gc = gc_ref[0]
    gn = gn_ref[...]
    s0 = _dot_tn((rkm_ref[...].astype(F32) * kdm_ref[0]).astype(BF16), rvm_ref[...])

    def body(c, state):
        r0 = pl.multiple_of(c * chunk, chunk)
        q = rq_ref[pl.ds(r0, chunk), :]
        k = rk_ref[pl.ds(r0, chunk), :]
        v = rv_ref[pl.ds(r0, chunk), :]
        rg = rg_ref[pl.ds(r0, chunk), :].astype(F32)
        scores = _dot_nt(q, k) * dec
        o = _dot(scores.astype(BF16), v) + _dot(q, state.astype(BF16)) * qd
        new_state = state * gc + _dot_tn((k.astype(F32) * kd).astype(BF16), v)
        o_ref[pl.ds(r0, chunk), :] = _head_norm_gate(o, gn, rg).astype(BF16)
        return new_state

    st_ref[0, 0, 0] = lax.fori_loop(0, nchunks, body, s0)


def _retention_prompt(rb, rkm, rvm, tabs, gn, *, batch, seq, heads, dk, chunk):
    dec, qd, kd, kdm, gc = tabs
    nmeta = rkm.shape[0]
    col = lambda off: (lambda b, h: (b, off * heads + h))
    tab = lambda b, h: (h, 0, 0)
    return pl.pallas_call(
        functools.partial(_ret_kernel, chunk=chunk, nchunks=seq // chunk),
        grid=(batch, heads),
        in_specs=[pl.BlockSpec((seq, dk), col(0)), pl.BlockSpec((seq, dk), col(1)),
                  pl.BlockSpec((seq, dk), col(2)), pl.BlockSpec((seq, dk), col(3)),
                  pl.BlockSpec((nmeta, dk), lambda b, h: (0, h)), pl.BlockSpec((nmeta, dk), lambda b, h: (0, h)),
                  pl.BlockSpec((1, chunk, chunk), tab), pl.BlockSpec((1, chunk, dk), tab),
                  pl.BlockSpec((1, chunk, dk), tab), pl.BlockSpec((1, nmeta, dk), tab),
                  pl.BlockSpec((1, dk, dk), tab),
                  pl.BlockSpec((1, dk), lambda b, h: (0, h))],
        out_specs=[pl.BlockSpec((seq, dk), lambda b, h: (b, h)),
                   pl.BlockSpec((1, 1, 1, dk, dk), lambda b, h: (0, b, h, 0, 0))],
        out_shape=[jax.ShapeDtypeStruct((batch * seq, heads * dk), BF16),
                   jax.ShapeDtypeStruct((1, batch, heads, dk, dk), F32)],
        compiler_params=_cparams(("parallel", "parallel")),
        name="retention_prompt",
    )(rb, rb, rb, rb, rkm, rvm, dec, qd, kd, kdm, gc, gn)


def _ret_s_kernel(q_ref, k_ref, v_ref, g_ref, gam_ref, gn_ref, st_ref, o_ref, ns_ref, *, heads, dk, bs):
    rowid = lax.broadcasted_iota(jnp.int32, (8, dk), 0)
    for hh in range(heads):
        sl = slice(hh * dk, (hh + 1) * dk)
        gam = gam_ref[hh]
        outs = []
        for i in range(bs):
            q = q_ref[i:i + 1, sl]
            k = k_ref[i:i + 1, sl]
            v = v_ref[i:i + 1, sl]
            state = st_ref[0, i, hh]
            k8 = jnp.where(rowid == 0, jnp.broadcast_to(k, (8, dk)), 0.0).astype(BF16)
            v8 = jnp.broadcast_to(v, (8, dk)).astype(BF16)
            ns_ref[0, i, hh] = state * gam[0:1, :] + _dot_tn(k8, v8)
            q8 = jnp.broadcast_to(q, (8, dk)).astype(BF16)
            qs = _dot(q8, state.astype(BF16))[0:1, :]
            qk = jnp.sum(q.astype(BF16).astype(F32) * k.astype(BF16).astype(F32), axis=1, keepdims=True)
            outs.append(qk * v.astype(BF16).astype(F32) + qs * gam[0:1, :])
        o = jnp.concatenate(outs, axis=0)
        o_ref[:, sl] = _head_norm_gate(o, gn_ref[:, sl], g_ref[:, sl]).astype(BF16)


def _retention_sample(rq, rk, rv, rg, gam, gn, state, *, heads, dk, bs):
    db = rq.shape[0]
    n = heads * dk
    row = lambda i: (i, 0)
    st_spec = pl.BlockSpec((1, bs, heads, dk, dk), lambda i: (0, i, 0, 0, 0))
    return pl.pallas_call(
        functools.partial(_ret_s_kernel, heads=heads, dk=dk, bs=bs),
        grid=(db // bs,),
        in_specs=[pl.BlockSpec((bs, n), row), pl.BlockSpec((bs, n), row), pl.BlockSpec((bs, n), row),
                  pl.BlockSpec((bs, n), row), pl.BlockSpec((heads, 8, dk), lambda i: (0, 0, 0)),
                  pl.BlockSpec((1, n), lambda i: (0, 0)), st_spec],
        out_specs=[pl.BlockSpec((bs, n), row), st_spec],
        out_shape=[jax.ShapeDtypeStruct((db, n), BF16), jax.ShapeDtypeStruct(state.shape, F32)],
        compiler_params=_cparams(("parallel",)),
        name="retention_sample",
    )(rq, rk, rv, rg, gam, gn, state)


def _out_kernel(om_ref, or_ref, x_ref, wo1_ref, wo2_ref, g_ref, wg_ref, bg_ref, wr_ref, br_ref,
                x1_ref, h2_ref, rt_ref, *, nsteps, ngroups, nexperts):
    i = pl.program_id(0)

    @pl.when(i < nsteps)
    def _():
        _out_rows(om_ref, or_ref, x_ref, wo1_ref, wo2_ref, g_ref, wg_ref, bg_ref, wr_ref, br_ref,
                  x1_ref, h2_ref, rt_ref, ngroups=ngroups, nexperts=nexperts)

    @pl.when(i >= nsteps)
    def _():
        h2_ref[...] = jnp.zeros_like(h2_ref)


def _out_rows(om_ref, or_ref, x_ref, wo1_ref, wo2_ref, g_ref, wg_ref, bg_ref, wr_ref, br_ref,
              x1_ref, h2_ref, rt_ref, *, ngroups, nexperts):
    x1 = x_ref[...] + _dot(om_ref[...], wo1_ref[...]) + _dot(or_ref[...], wo2_ref[...])
    x1_ref[...] = x1
    hb = _rms(x1, g_ref[...]).astype(BF16)
    h2_ref[...] = hb.astype(F32)
    tm = x1.shape[0]
    lane = lax.broadcasted_iota(jnp.int32, (tm, LANES), 1)
    gl = jnp.where(lane < ngroups, _dot(hb, wg_ref[...]) + bg_ref[...], NEG)
    gmax = jnp.max(gl, axis=1, keepdims=True)
    gsum = jnp.sum(jnp.exp(gl - gmax), axis=1, keepdims=True)
    g_w = 1.0 / gsum
    g_idx = jnp.min(jnp.where(gl == gmax, lane, LANES), axis=1, keepdims=True)
    epg = nexperts // ngroups
    in_group = (lane >= g_idx * epg) & (lane < (g_idx + 1) * epg)
    el = jnp.where(in_group, _dot(hb, wr_ref[...]) + br_ref[...], NEG)
    emax = jnp.max(el, axis=1, keepdims=True)
    esum = jnp.sum(jnp.exp(el - emax), axis=1, keepdims=True)
    idx1 = jnp.min(jnp.where(el == emax, lane, LANES), axis=1, keepdims=True)
    el2 = jnp.where(lane == idx1, NEG, el)
    emax2 = jnp.max(el2, axis=1, keepdims=True)
    idx2 = jnp.min(jnp.where(el2 == emax2, lane, LANES), axis=1, keepdims=True)
    p1 = 1.0 / esum
    p2 = jnp.exp(emax2 - emax) / esum
    w1 = p1 / (p1 + p2) * g_w
    w2 = p2 / (p1 + p2) * g_w
    rt_ref[...] = jnp.where(lane == 0, w1, jnp.where(lane == 1, w2, jnp.where(
        lane == 2, idx1.astype(F32), jnp.where(lane == 3, idx2.astype(F32), 0.0))))


def _out_kernel_into(om_ref, or_ref, x_ref, wo1_ref, wo2_ref, g_ref, wg_ref, bg_ref, wr_ref, br_ref,
                     h2_prev_ref, x1_ref, h2_ref, rt_ref, **kw):
    del h2_prev_ref
    _out_kernel(om_ref, or_ref, x_ref, wo1_ref, wo2_ref, g_ref, wg_ref, bg_ref, wr_ref, br_ref,
                x1_ref, h2_ref, rt_ref, **kw)


def _out_proj(om, orr, x, wo1, wo2, g, wg, bg, wr, br, *, rows, tm, h2_rows, h2_row0, h2_prev,
              ngroups, nexperts):
    d = x.shape[1]
    n1, n2 = om.shape[1], orr.shape[1]
    nsteps = rows // tm
    ntail = (h2_rows - rows) // tm if h2_prev is None else 0
    row = lambda i: (jnp.minimum(i, nsteps - 1), 0)
    fix = lambda i: (0, 0)
    in_specs = [pl.BlockSpec((tm, n1), row), pl.BlockSpec((tm, n2), row), pl.BlockSpec((tm, d), row),
                pl.BlockSpec((n1, d), fix), pl.BlockSpec((n2, d), fix), pl.BlockSpec((1, d), fix),
                pl.BlockSpec((d, LANES), fix), pl.BlockSpec((1, LANES), fix),
                pl.BlockSpec((d, LANES), fix), pl.BlockSpec((1, LANES), fix)]
    args = [om, orr, x, wo1, wo2, g, wg, bg, wr, br]
    body, aliases = _out_kernel, {}
    if h2_prev is not None:
        in_specs.append(pl.BlockSpec(memory_space=pl.ANY))
        args.append(h2_prev)
        body, aliases = _out_kernel_into, {len(args) - 1: 1}
    return pl.pallas_call(
        functools.partial(body, nsteps=nsteps, ngroups=ngroups, nexperts=nexperts),
        grid=(nsteps + ntail,),
        in_specs=in_specs,
        out_specs=[pl.BlockSpec((tm, d), row), pl.BlockSpec((tm, d), lambda i: (h2_row0 // tm + i, 0)),
                   pl.BlockSpec((tm, LANES), row)],
        out_shape=[jax.ShapeDtypeStruct((rows, d), F32), jax.ShapeDtypeStruct((h2_rows, d), F32),
                   jax.ShapeDtypeStruct((rows, LANES), F32)],
        input_output_aliases=aliases,
        compiler_params=_cparams(("arbitrary",)),
        name="out_proj_router",
    )(*args)


def _moe_kernel(te_ref, s0_ref, nr_ref, nu_ref, sp_ref, h2_hbm, wg_ref, wu_ref, wd_ref, ys_hbm,
                xbuf, ybuf, gsem, ssem, *, tm, npair, ntiles):
    i = pl.program_id(0)
    nused = nu_ref[0]

    def gather(tile, slot):
        src0 = s0_ref[tile]
        last = src0 + nr_ref[tile] - 1

        for r in range(tm):
            pair = sp_ref[jnp.minimum(src0 + r, last)]
            pltpu.make_async_copy(h2_hbm.at[pl.ds(lax.shift_right_logical(pair, 1), 1), :],
                                  xbuf.at[slot, pl.ds(r, 1), :], gsem.at[slot]).start()

    def scatter(tile, slot):
        src0 = s0_ref[tile]
        n = nr_ref[tile]
        for r in range(tm):
            pair = sp_ref[jnp.minimum(src0 + r, npair - 1)]
            dst = jnp.where(r < n, pair, npair + slot * tm + r)
            pltpu.make_async_copy(ybuf.at[slot, pl.ds(r, 1), :], ys_hbm.at[pl.ds(dst, 1), :],
                                  ssem.at[slot]).start()

    def wait_tile(buf, sem, slot):
        pltpu.make_async_copy(h2_hbm.at[pl.ds(0, tm), :], buf.at[slot], sem.at[slot]).wait()

    @pl.when(i == 0)
    def _():
        gather(0, 0)
        ybuf[0] = jnp.zeros(ybuf.shape[1:], F32)
        for s in range(2):
            pltpu.make_async_copy(ybuf.at[0], ys_hbm.at[pl.ds(npair + s * tm, tm), :], ssem.at[0]).start()
        for s in range(2):
            wait_tile(ybuf, ssem, 0)

    @pl.when(i + 1 < nused)
    def _():
        gather(i + 1, (i + 1) % 2)

    @pl.when(i < nused)
    def _():
        slot = i % 2
        wait_tile(xbuf, gsem, slot)

        @pl.when(i >= 2)
        def _():
            wait_tile(ybuf, ssem, slot)

        x = xbuf[slot]
        a = jax.nn.silu(_dot(x, wg_ref[0])) * _dot(x, wu_ref[0])
        ybuf[slot] = _dot(a, wd_ref[0])
        scatter(i, slot)

    @pl.when(i == ntiles - 1)
    def _():
        @pl.when(nused >= 2)
        def _():
            wait_tile(ybuf, ssem, nused % 2)
        wait_tile(ybuf, ssem, (nused + 1) % 2)


def _moe(tile_expert, tile_src0, tile_rows, nused, sorted_pairs, h2, w_gate, w_up, w_down, *, tm):
    ntiles = tile_expert.shape[0]
    npair = sorted_pairs.shape[0]
    d = h2.shape[1]
    f = w_gate.shape[2]
    wspec = lambda shape: pl.BlockSpec(shape, lambda i, te, s0, nr, nu, sp: (te[i], 0, 0))
    grid_spec = pltpu.PrefetchScalarGridSpec(
        num_scalar_prefetch=5,
        grid=(ntiles,),
        in_specs=[pl.BlockSpec(memory_space=pl.ANY), wspec((1, d, f)), wspec((1, d, f)), wspec((1, f, d))],
        out_specs=pl.BlockSpec(memory_space=pl.ANY),
        scratch_shapes=[pltpu.VMEM((2, tm, d), F32), pltpu.VMEM((2, tm, d), F32),
                        pltpu.SemaphoreType.DMA((2,)), pltpu.SemaphoreType.DMA((2,))],
    )
    return pl.pallas_call(
        functools.partial(_moe_kernel, tm=tm, npair=npair, ntiles=ntiles),
        grid_spec=grid_spec,
        out_shape=jax.ShapeDtypeStruct((npair + 2 * tm, d), F32),
        compiler_params=_cparams(("arbitrary",)),
        name="moe_experts",
    )(tile_expert, tile_src0, tile_rows, nused, sorted_pairs, h2, w_gate, w_up, w_down)


def _comb_kernel(x1_ref, ys_ref, rt_ref, g_ref, y_ref):
    d = x1_ref.shape[1]
    rt = rt_ref[...]
    x = x1_ref[...] + rt[:, 0:1] * ys_ref[:, :d] + rt[:, 1:2] * ys_ref[:, d:]
    y_ref[...] = _rms(x, g_ref[...])


def _combine(x1, ys2, rt, g, *, tm, row0, rows):
    d = x1.shape[1]
    return pl.pallas_call(
        _comb_kernel,
        grid=(rows // tm,),
        in_specs=[pl.BlockSpec((tm, d), lambda i: (i, 0)), pl.BlockSpec((tm, 2 * d), lambda i: (row0 // tm + i, 0)),
                  pl.BlockSpec((tm, LANES), lambda i: (i, 0)), pl.BlockSpec((1, d), lambda i: (0, 0))],
        out_specs=pl.BlockSpec((tm, d), lambda i: (i, 0)),
        out_shape=jax.ShapeDtypeStruct((rows, d), F32),
        compiler_params=_cparams(("parallel",)),
        name="moe_combine_norm",
    )(x1, ys2, rt, g)


def _rope_tables(pos, dim):
    half = dim // 2
    inv = jnp.exp(jnp.arange(half, dtype=F32) * (-2.0 * math.log(ROPE_BASE) / dim))
    ang = pos.astype(F32)[:, None] * inv[None, :]
    return jnp.cos(ang), jnp.sin(ang)


def _rot_cols(w):
    half = w.shape[-1] // 2
    return jnp.concatenate([-w[..., half:], w[..., :half]], axis=-1)


def _pick_tile(rows, target):
    t = min(rows, target)
    while rows % t:
        t //= 2
    return t


def _dispatch(route, ntok, nexperts, tm, ntiles):
    i32 = jnp.int32
    ids = route[:, 2:4].astype(i32).reshape(-1)
    npair = 2 * ntok
    _, sorted_pairs = lax.sort((ids, jnp.arange(npair, dtype=i32)), num_keys=1, is_stable=True)
    ex = jnp.arange(nexperts, dtype=i32)
    counts = jnp.sum((ids[:, None] == ex[None, :]).astype(i32), axis=0)
    starts = jnp.cumsum(counts) - counts
    tcount = (counts + tm - 1) // tm
    tend = jnp.cumsum(tcount)
    tstart = tend - tcount
    nused = tend[-1]
    t = jnp.arange(ntiles, dtype=i32)
    tile_expert = jnp.minimum(jnp.sum((tend[None, :] <= t[:, None]).astype(i32), axis=1), nexperts - 1)
    pick = lambda tab: jnp.sum(jnp.where(tile_expert[:, None] == ex[None, :], tab[None, :], 0), axis=1)
    local = t - pick(tstart)
    tile_src0 = jnp.clip(pick(starts) + local * tm, 0, npair - 1)
    tile_rows = jnp.where(t < nused, jnp.clip(pick(counts) - local * tm, 0, tm), 0)
    return (tile_expert.astype(i32), tile_src0.astype(i32), tile_rows.astype(i32),
            nused.reshape(1).astype(i32), sorted_pairs)


def kernel(x_prompt, x_sample, cache_kv_latent, cache_k_rope, state_retention, page_table, meta_tokens,
           ln_mix_g, w_in, q_norm_g, w_uq, w_uk, kv_norm_g, w_uv, mla_out_g, ret_gn_g, w_o, ln_ffn_g,
           w_group, b_group, w_router, b_router, w_gate, w_up, w_down, final_g):
    batch, seq, d = x_prompt.shape
    db, dec_seq, _ = x_sample.shape
    depth = w_in.shape[0]
    assert depth == 1 and dec_seq == 1
    ql = q_norm_g.shape[1]
    kvl = kv_norm_g.shape[1]
    mh, nr = w_uq.shape[2], w_uq.shape[3]
    nope = w_uk.shape[3]
    rope = nr - nope
    vdim = w_uv.shape[3]
    rdim = ret_gn_g.shape[1]
    dk = state_retention.shape[3]
    rh = rdim // dk
    ngroups = w_group.shape[2]
    nexperts = w_router.shape[2]
    page = cache_kv_latent.shape[2]
    npages = page_table.shape[1]
    past_len = npages * page
    chunk = min(2 * ROW_TILE, seq)
    tk = min(4 * ROW_TILE, seq)
    nmain = batch * seq
    naux = -(-(db + N_META) // LANES) * LANES
    ts = -(-db // ROW_TILE) * ROW_TILE
    assert seq % 256 == 0 and seq % tk == 0 and ts <= naux and db % 8 == 0 and rh == mh

    wi = w_in[0]
    o_kr = ql + kvl
    o_ret = o_kr + rope
    w_a = jnp.concatenate([wi[:, :o_ret], _rot_cols(wi[:, o_kr:o_ret])], axis=1).astype(BF16)
    w_b = wi[:, o_ret:].astype(BF16)
    wq = w_uq[0]
    wq_rope = wq[:, :, nope:]
    wuq = jnp.concatenate([wq[:, :, :nope].reshape(ql, mh * nope), wq_rope.reshape(ql, mh * rope),
                           _rot_cols(wq_rope).reshape(ql, mh * rope)], axis=1).astype(BF16)
    wukt = jnp.transpose(w_uk[0], (1, 2, 0)).astype(BF16)
    wuv = jnp.transpose(w_uv[0], (1, 0, 2)).astype(BF16)
    wo = w_o[0].astype(BF16)
    wo1, wo2 = wo[:mh * vdim], wo[mh * vdim:]
    wg_pad = jnp.zeros((d, LANES), F32).at[:, :ngroups].set(w_group[0]).astype(BF16)
    bg_pad = jnp.zeros((1, LANES), F32).at[0, :ngroups].set(b_group[0])
    wr_pad = jnp.zeros((d, LANES), F32).at[:, :nexperts].set(w_router[0]).astype(BF16)
    br_pad = jnp.zeros((1, LANES), F32).at[0, :nexperts].set(b_router[0])

    pos_main = N_META + jnp.arange(nmain, dtype=jnp.int32) % seq
    pos_aux = jnp.concatenate([jnp.full((db,), past_len, jnp.int32), jnp.arange(N_META, dtype=jnp.int32),
                               jnp.zeros((naux - db - N_META,), jnp.int32)])

    def tables(pos):
        c64, s64 = _rope_tables(pos, rope)
        c128, s128 = _rope_tables(pos, dk)
        cs = jnp.concatenate([c64, c64], axis=1)
        sn = jnp.concatenate([s64, s64], axis=1)
        return (cs, sn, jnp.tile(cs, (1, mh)), jnp.tile(sn, (1, mh)),
                jnp.concatenate([c128, c128], axis=1), jnp.concatenate([-s128, s128], axis=1))

    log_g = jnp.log1p(-jnp.exp2(-5.0 - jnp.arange(rh, dtype=F32)))
    n = jnp.arange(chunk, dtype=F32)
    diff = n[:, None] - n[None, :]
    t_dec = jnp.where(diff >= 0, jnp.exp(log_g[:, None, None] * jnp.maximum(diff, 0.0)), 0.0)
    t_qd = jnp.broadcast_to(jnp.exp((n[None, :] + 1.0) * log_g[:, None])[:, :, None], (rh, chunk, dk))
    k_decay = jnp.exp((chunk - 1.0 - n)[None, :] * log_g[:, None])
    t_kd = jnp.broadcast_to(k_decay[:, :, None], (rh, chunk, dk))
    t_kdm = jnp.broadcast_to(k_decay[:, chunk - N_META:, None], (rh, N_META, dk))
    t_gc = jnp.broadcast_to(jnp.exp(chunk * log_g)[:, None, None], (rh, dk, dk))
    t_gam = jnp.broadcast_to(jnp.exp(log_g)[:, None, None], (rh, 8, dk))

    x_main = x_prompt.reshape(nmain, d)
    x_aux = jnp.concatenate([x_sample.reshape(db, d), meta_tokens.astype(F32),
                             jnp.zeros((naux - db - N_META, d), F32)], axis=0)
    g_mix = ln_mix_g[0][None]
    scale = float(nope + rope) ** -0.5
    tm_main = _pick_tile(nmain, 512)
    tm_aux = _pick_tile(naux, 512)

    def token_front(x, pos, tm):
        cs, sn, cs8, sn8, cos_r, sin_r = tables(pos)
        cq, kv, kvb, kr, krb = _in_a(x, g_mix, w_a, q_norm_g, kv_norm_g, cs, sn,
                                     tm=tm, ql=ql, kvl=kvl, rope=rope)
        rb = _in_b(x, g_mix, w_b, cos_r, sin_r, tm=tm, heads=rh, dk=dk)
        q_lat, q_rope = _q_proj(cq, wuq, wukt, cs8, sn8, tm=tm, heads=mh, nope=nope, rope=rope, scale=scale)
        return kv, kvb, kr, krb, rb, q_lat, q_rope

    kv_m, kvb_m, kr_m, krb_m, rb_m, ql_m, qr_m = token_front(x_main, pos_main, tm_main)
    kv_a, kvb_a, kr_a, krb_a, rb_a, ql_a, qr_a = token_front(x_aux, pos_aux, tm_aux)
    meta = slice(db, db + N_META)

    om_m = _attention(ql_m, qr_m, kvb_m, krb_m, kvb_a[meta], krb_a[meta], wuv, mla_out_g,
                      batch=batch, seq=seq, tq=ROW_TILE, tk=tk)
    rdk = rh * dk
    or_m, st_p = _retention_prompt(rb_m, rb_a[meta, rdk:2 * rdk], rb_a[meta, 2 * rdk:3 * rdk],
                                   (t_dec, t_qd, t_kd, t_kdm, t_gc), ret_gn_g,
                                   batch=batch, seq=seq, heads=rh, dk=dk, chunk=chunk)

    ql_s = jnp.transpose(ql_a[:, :db], (1, 0, 2))
    qr_s = jnp.transpose(qr_a[:, :db], (1, 0, 2))
    o_s = _decode_attention(page_table, ql_s, qr_s, kv_a[:db, None, :], kr_a[:db, None, :],
                            cache_kv_latent[0], jnp.swapaxes(cache_k_rope[0], 1, 2),
                            gpages=min(16, npages), nbuf=3)
    om_s = _decode_out(jnp.transpose(o_s, (1, 0, 2)), wuv, mla_out_g)
    rs = rb_a[:db].astype(F32)
    or_s, st_s = _retention_sample(rs[:, :rdk], rs[:, rdk:2 * rdk], rs[:, 2 * rdk:3 * rdk], rs[:, 3 * rdk:],
                                   t_gam, ret_gn_g, state_retention, heads=rh, dk=dk, bs=8)
    pad_rows = lambda a: jnp.concatenate([a, jnp.zeros((ts - db, a.shape[1]), a.dtype)], axis=0)
    om_a, or_a = pad_rows(om_s), pad_rows(or_s)

    g_ffn = ln_ffn_g[0][None]
    ntok = nmain + ts
    tm_o = _pick_tile(nmain, 256)
    outp = functools.partial(_out_proj, h2_rows=nmain + -(-ts // tm_o) * tm_o, ngroups=ngroups, nexperts=nexperts)
    x1_m, h2, rt_m = outp(om_m, or_m, x_main, wo1, wo2, g_ffn, wg_pad, bg_pad, wr_pad, br_pad,
                          rows=nmain, tm=tm_o, h2_row0=0, h2_prev=None)
    x1_a, h2, rt_a = outp(om_a, or_a, x_aux, wo1, wo2, g_ffn, wg_pad, bg_pad, wr_pad, br_pad,
                          rows=ts, tm=ROW_TILE, h2_row0=nmain, h2_prev=h2)

    route = jnp.concatenate([rt_m[:, :4], rt_a[:, :4]], axis=0)
    tm_e = ROW_TILE
    ntiles = -(-(2 * ntok + nexperts * (tm_e - 1)) // tm_e)
    ys = _moe(*_dispatch(route, ntok, nexperts, tm_e, ntiles), h2, w_gate[0], w_up[0], w_down[0], tm=tm_e)
    ys2 = ys.reshape(ys.shape[0] // 2, 2 * d)
    fg = final_g[None]
    y_m = _combine(x1_m, ys2, rt_m, fg, tm=_pick_tile(nmain, 256), row0=0, rows=nmain)
    y_a = _combine(x1_a, ys2, rt_a, fg, tm=ROW_TILE, row0=nmain, rows=ts)

    def with_meta(a_main, a_aux):
        w = a_main.shape[1]
        m = jnp.broadcast_to(a_aux[meta][None], (batch, N_META, w))
        return jnp.concatenate([m, a_main.reshape(batch, seq, w)], axis=1)[None]

    return (y_m.reshape(batch, seq, d), y_a[:db].reshape(db, 1, d),
            with_meta(kv_m, kv_a), with_meta(kr_m, kr_a), st_p,
            kv_a[:db].reshape(1, db, 1, kvl), kr_a[:db].reshape(1, db, 1, rope), st_s)
```

```python
import functools
import math

import jax
import jax.numpy as jnp
from jax import lax
from jax.experimental import pallas as pl
from jax.experimental.pallas import tpu as pltpu

F32 = jnp.float32
BF16 = jnp.bfloat16

N_META = 16
ROPE_BASE = 10000.0
EPS = 1e-6
NEG = -1e30
EXPERTS_PER_GROUP = 8
LANES = 128
ROW_TILE = 128
VMEM_LIMIT = 56 * 1024 * 1024


def _cparams(sem, vmem=VMEM_LIMIT):
    return pltpu.CompilerParams(dimension_semantics=sem, vmem_limit_bytes=vmem)


def _rms(x, g):
    return x * lax.rsqrt(jnp.mean(x * x, axis=-1, keepdims=True) + EPS) * g


def _dot(a, b):
    return jnp.dot(a, b, preferred_element_type=F32)


def _dot_nt(a, b):
    return lax.dot_general(a, b, (((1,), (1,)), ((), ())), preferred_element_type=F32)


def _dot_tn(a, b):
    return lax.dot_general(a, b, (((0,), (0,)), ((), ())), preferred_element_type=F32)


def _store_chunked(ref, val, pitch, base=0):
    rows, d = val.shape
    for c in range(d // LANES):
        ref[pl.ds(base + c, rows, stride=pitch), :] = val[:, c * LANES:(c + 1) * LANES]


def _load_chunked(ref, rows, nchunk, pitch, base=0):
    return jnp.concatenate([ref[pl.ds(base + c, rows, stride=pitch), :] for c in range(nchunk)], axis=1)


def _in_a_kernel(x_ref, g_ref, w_ref, qg_ref, kvg_ref, cs_ref, sn_ref,
                 cq_ref, kv_ref, kvb_ref, kr_ref, krb_ref, *, ql, kvl, rope):
    h = _rms(x_ref[...], g_ref[...]).astype(BF16)
    c = _dot_nt(h, w_ref[...])
    cq_ref[...] = _rms(c[:, :ql], qg_ref[...]).astype(BF16)
    kv = _rms(c[:, ql:ql + kvl], kvg_ref[...])
    kv_ref[...] = kv
    kvb_ref[...] = kv.astype(BF16)
    a = c[:, ql + kvl:ql + kvl + rope]
    b = c[:, ql + kvl + rope:]
    kr = a * cs_ref[...] + b * sn_ref[...]
    kr_ref[...] = kr
    krb_ref[...] = kr.astype(BF16)


def _in_a(x, g, w_a, qg, kvg, cs, sn, *, tm, ql, kvl, rope):
    rows, d = x.shape
    n = w_a.shape[0]
    row = lambda i: (i, 0)
    fix = lambda i: (0, 0)
    tab = lambda i: (i % (cs.shape[0] // tm), 0)
    return pl.pallas_call(
        functools.partial(_in_a_kernel, ql=ql, kvl=kvl, rope=rope),
        grid=(rows // tm,),
        in_specs=[pl.BlockSpec((tm, d), row), pl.BlockSpec((1, d), fix), pl.BlockSpec((n, d), fix),
                  pl.BlockSpec((1, ql), fix), pl.BlockSpec((1, kvl), fix),
                  pl.BlockSpec((tm, rope), tab), pl.BlockSpec((tm, rope), tab)],
        out_specs=[pl.BlockSpec((tm, ql), row), pl.BlockSpec((tm, kvl), row), pl.BlockSpec((tm, kvl), row),
                   pl.BlockSpec((tm, rope), row), pl.BlockSpec((tm, rope), row)],
        out_shape=[jax.ShapeDtypeStruct((rows, ql), BF16), jax.ShapeDtypeStruct((rows, kvl), F32),
                   jax.ShapeDtypeStruct((rows, kvl), BF16), jax.ShapeDtypeStruct((rows, rope), F32),
                   jax.ShapeDtypeStruct((rows, rope), BF16)],
        compiler_params=_cparams(("parallel",)),
        name="in_proj_a",
    )(x, g, w_a, qg, kvg, cs, sn)


def _in_b_kernel(x_ref, g_ref, w_ref, cos_ref, sin_ref, out_ref, h_scr, *, heads, dk):
    j = pl.program_id(1)

    @pl.when(j == 0)
    def _():
        h_scr[...] = _rms(x_ref[...], g_ref[...]).astype(BF16)

    c = _dot_nt(h_scr[...], w_ref[...])

    @pl.when(j < 2)
    def _():
        scale = jnp.where(j == 1, dk ** -0.5, 1.0).astype(F32)
        cos = cos_ref[...]
        sin = sin_ref[...]
        for hh in range(heads):
            blk = c[:, hh * dk:(hh + 1) * dk]
            r = blk * cos + pltpu.roll(blk, dk // 2, 1) * sin
            out_ref[:, hh * dk:(hh + 1) * dk] = (r * scale).astype(BF16)

    @pl.when(j >= 2)
    def _():
        out_ref[...] = c.astype(BF16)


def _in_b(x, g, w_b, cos, sin, *, tm, heads, dk):
    rows, d = x.shape
    n = heads * dk
    tab = lambda i, j: (i % (cos.shape[0] // tm), 0)
    return pl.pallas_call(
        functools.partial(_in_b_kernel, heads=heads, dk=dk),
        grid=(rows // tm, 4),
        in_specs=[pl.BlockSpec((tm, d), lambda i, j: (i, 0)), pl.BlockSpec((1, d), lambda i, j: (0, 0)),
                  pl.BlockSpec((n, d), lambda i, j: (j, 0)),
                  pl.BlockSpec((tm, dk), tab), pl.BlockSpec((tm, dk), tab)],
        out_specs=pl.BlockSpec((tm, n), lambda i, j: (i, j)),
        out_shape=jax.ShapeDtypeStruct((rows, 4 * n), BF16),
        scratch_shapes=[pltpu.VMEM((tm, d), BF16)],
        compiler_params=_cparams(("parallel", "arbitrary")),
        name="in_proj_b",
    )(x, g, w_b, cos, sin)


def _q_kernel(cq_ref, wuq_ref, wukt_ref, cos_ref, sin_ref, ql_ref, qr_ref, *, heads, nope, rope, scale):
    qf = _dot(cq_ref[...], wuq_ref[...])
    o1 = heads * nope
    o2 = o1 + heads * rope
    cos = cos_ref[...]
    sin = sin_ref[...]
    for hp in range(heads // 2):
        lo, hi = hp * 2 * rope, (hp + 1) * 2 * rope
        qr = (qf[:, o1 + lo:o1 + hi] * cos + qf[:, o2 + lo:o2 + hi] * sin) * scale
        qr_ref[2 * hp] = qr[:, :rope].astype(BF16)
        qr_ref[2 * hp + 1] = qr[:, rope:].astype(BF16)
    for hh in range(heads):
        qn = qf[:, hh * nope:(hh + 1) * nope].astype(BF16)
        ql_ref[hh] = (_dot(qn, wukt_ref[hh]) * scale).astype(BF16)


def _q_proj(cq, wuq, wukt, cos, sin, *, tm, heads, nope, rope, scale):
    rows, ql = cq.shape
    kvl = wukt.shape[2]
    nq = wuq.shape[1]
    tab = lambda i: (i % (cos.shape[0] // tm), 0)
    return pl.pallas_call(
        functools.partial(_q_kernel, heads=heads, nope=nope, rope=rope, scale=scale),
        grid=(rows // tm,),
        in_specs=[pl.BlockSpec((tm, ql), lambda i: (i, 0)), pl.BlockSpec((ql, nq), lambda i: (0, 0)),
                  pl.BlockSpec((heads, nope, kvl), lambda i: (0, 0, 0)),
                  pl.BlockSpec((tm, 2 * rope), tab), pl.BlockSpec((tm, 2 * rope), tab)],
        out_specs=[pl.BlockSpec((heads, tm, kvl), lambda i: (0, i, 0)),
                   pl.BlockSpec((heads, tm, rope), lambda i: (0, i, 0))],
        out_shape=[jax.ShapeDtypeStruct((heads, rows, kvl), BF16),
                   jax.ShapeDtypeStruct((heads, rows, rope), BF16)],
        compiler_params=_cparams(("parallel",)),
        name="q_proj",
    )(cq, wuq, wukt, cos, sin)


def _uv_norm(o_heads, wuv_ref, g):
    parts = [_dot(o.astype(BF16), wuv_ref[hh]) for hh, o in enumerate(o_heads)]
    return _rms(jnp.concatenate(parts, axis=1), g).astype(BF16)


def _widen(x, n):
    if n <= LANES:
        return x[:, :n]
    return jnp.concatenate([x] * (n // LANES), axis=1)


def _att_kernel(ql_ref, qr_ref, kv_ref, kr_ref, kvm_ref, krm_ref, wuv_ref, g_ref, o_ref,
                m_scr, l_scr, acc_scr, *, heads, tq, tk, nk):
    qi = pl.program_id(1)
    kj = pl.program_id(2)
    kvl = ql_ref.shape[2]
    rows = heads * tq
    ql = ql_ref[...].reshape(rows, kvl)
    qr = qr_ref[...].reshape(rows, qr_ref.shape[2])

    def update(s, kv, first):
        m_cur = jnp.max(s, axis=1, keepdims=True)
        if first:
            m_new = jnp.broadcast_to(m_cur, (rows, LANES))
        else:
            m_prev = m_scr[...]
            m_new = jnp.maximum(m_prev, m_cur)
        p = jnp.exp(s - _widen(m_new, s.shape[1]))
        p_sum = jnp.sum(p, axis=1, keepdims=True)
        pv = _dot(p.astype(BF16), kv)
        if first:
            l_scr[...] = jnp.broadcast_to(p_sum, (rows, LANES))
            acc_scr[...] = pv
        else:
            alpha = jnp.exp(m_prev - m_new)
            l_scr[...] = alpha * l_scr[...] + p_sum
            acc_scr[...] = acc_scr[...] * _widen(alpha, kvl) + pv
        m_scr[...] = m_new

    @pl.when(kj == 0)
    def _():
        kvm = kvm_ref[...]
        update(_dot_nt(ql, kvm) + _dot_nt(qr, krm_ref[...]), kvm, True)

    last = (qi * tq + tq - 1) // tk

    @pl.when(kj < last)
    def _():
        kv = kv_ref[...]
        update(_dot_nt(ql, kv) + _dot_nt(qr, kr_ref[...]), kv, False)

    @pl.when(kj == last)
    def _():
        kv = kv_ref[...]
        s = _dot_nt(ql, kv) + _dot_nt(qr, kr_ref[...])
        qpos = qi * tq + lax.broadcasted_iota(jnp.int32, (rows, tk), 0) % tq
        kpos = kj * tk + lax.broadcasted_iota(jnp.int32, (rows, tk), 1)
        update(jnp.where(kpos <= qpos, s, NEG), kv, False)

    @pl.when(kj == nk - 1)
    def _():
        o = acc_scr[...] / _widen(l_scr[...], kvl)
        o_ref[...] = _uv_norm([o[hh * tq:(hh + 1) * tq] for hh in range(heads)], wuv_ref, g_ref[...])


def _attention(ql, qr, kvb, krb, kvm, krm, wuv, g, *, batch, seq, tq, tk):
    heads, rows, kvl = ql.shape
    rope = qr.shape[2]
    vdim = wuv.shape[2]
    nq, nk = seq // tq, seq // tk
    nmeta = kvm.shape[0]

    def kidx(b, qi, kj):
        return (b * nk + jnp.minimum(kj, (qi * tq + tq - 1) // tk), 0)

    return pl.pallas_call(
        functools.partial(_att_kernel, heads=heads, tq=tq, tk=tk, nk=nk),
        grid=(batch, nq, nk),
        in_specs=[pl.BlockSpec((heads, tq, kvl), lambda b, qi, kj: (0, b * nq + qi, 0)),
                  pl.BlockSpec((heads, tq, rope), lambda b, qi, kj: (0, b * nq + qi, 0)),
                  pl.BlockSpec((tk, kvl), kidx), pl.BlockSpec((tk, rope), kidx),
                  pl.BlockSpec((nmeta, kvl), lambda b, qi, kj: (0, 0)),
                  pl.BlockSpec((nmeta, rope), lambda b, qi, kj: (0, 0)),
                  pl.BlockSpec((heads, kvl, vdim), lambda b, qi, kj: (0, 0, 0)),
                  pl.BlockSpec((1, heads * vdim), lambda b, qi, kj: (0, 0))],
        out_specs=pl.BlockSpec((tq, heads * vdim), lambda b, qi, kj: (b * nq + qi, 0)),
        out_shape=jax.ShapeDtypeStruct((rows, heads * vdim), BF16),
        scratch_shapes=[pltpu.VMEM((heads * tq, LANES), F32), pltpu.VMEM((heads * tq, LANES), F32),
                        pltpu.VMEM((heads * tq, kvl), F32)],
        compiler_params=_cparams(("parallel", "parallel", "arbitrary")),
        name="mla_prompt_attention",
    )(ql, qr, kvb, krb, kvm, krm, wuv, g)


def _dec_kernel(pt_ref, ql_ref, qr_ref, kvn_ref, krn_ref, ckv_hbm, ckr_hbm, o_ref,
                kvbuf, krbuf, sem, *, nchunk, gpages, nbuf, total):
    b = pl.program_id(0)
    page = kvbuf.shape[2]
    kvl = kvbuf.shape[3]

    def copies(g, slot):
        bb = g // nchunk
        c0 = (g % nchunk) * gpages
        out = []
        for p in range(gpages):
            pg = pt_ref[bb, c0 + p]
            out.append(pltpu.make_async_copy(ckv_hbm.at[pg], kvbuf.at[slot, p], sem.at[slot]))
            out.append(pltpu.make_async_copy(ckr_hbm.at[pg], krbuf.at[slot, p], sem.at[slot]))
        return out

    def start(g, slot):
        for cp in copies(g, slot):
            cp.start()

    @pl.when(b == 0)
    def _():
        for g0 in range(min(nbuf - 1, total)):
            start(g0, g0 % nbuf)

    ql = ql_ref[0].astype(F32)
    qr = qr_ref[0].astype(F32)
    kvn = kvn_ref[0]
    krn = krn_ref[0]
    heads = ql.shape[0]
    m0 = jnp.sum(ql * kvn, axis=1, keepdims=True) + jnp.sum(qr * krn, axis=1, keepdims=True)
    l0 = jnp.ones((heads, 1), F32)
    acc0 = jnp.broadcast_to(kvn, (heads, kvl))

    def body(c, carry):
        m, l, acc = carry
        g = b * nchunk + c
        slot = g % nbuf
        for cp in copies(g, slot):
            cp.wait()
        nxt = g + nbuf - 1

        @pl.when(nxt < total)
        def _():
            start(nxt, nxt % nbuf)

        kvc = kvbuf[slot].reshape(gpages * page, kvl)
        krt = jnp.concatenate([krbuf[slot, p] for p in range(gpages)], axis=1)
        s = _dot_nt(ql, kvc) + _dot(qr, krt)
        m_new = jnp.maximum(m, jnp.max(s, axis=1, keepdims=True))
        alpha = jnp.exp(m - m_new)
        p = jnp.exp(s - m_new)
        l = l * alpha + jnp.sum(p, axis=1, keepdims=True)
        acc = acc * alpha + _dot(p, kvc)
        return m_new, l, acc

    m, l, acc = lax.fori_loop(0, nchunk, body, (m0, l0, acc0))
    o_ref[0] = acc / l


def _decode_attention(page_table, ql_s, qr_s, kvn, krn, cache_kv, cache_krt, *, gpages, nbuf):
    db, npages = page_table.shape
    heads, kvl = ql_s.shape[1:]
    rope = qr_s.shape[2]
    page = cache_kv.shape[1]
    nchunk = npages // gpages
    total = db * nchunk
    grid_spec = pltpu.PrefetchScalarGridSpec(
        num_scalar_prefetch=1,
        grid=(db,),
        in_specs=[pl.BlockSpec((1, heads, kvl), lambda b, pt: (b, 0, 0)),
                  pl.BlockSpec((1, heads, rope), lambda b, pt: (b, 0, 0)),
                  pl.BlockSpec((1, 1, kvl), lambda b, pt: (b, 0, 0)),
                  pl.BlockSpec((1, 1, rope), lambda b, pt: (b, 0, 0)),
                  pl.BlockSpec(memory_space=pl.ANY), pl.BlockSpec(memory_space=pl.ANY)],
        out_specs=pl.BlockSpec((1, heads, kvl), lambda b, pt: (b, 0, 0)),
        scratch_shapes=[pltpu.VMEM((nbuf, gpages, page, kvl), F32),
                        pltpu.VMEM((nbuf, gpages, rope, page), F32),
                        pltpu.SemaphoreType.DMA((nbuf,))],
    )
    return pl.pallas_call(
        functools.partial(_dec_kernel, nchunk=nchunk, gpages=gpages, nbuf=nbuf, total=total),
        grid_spec=grid_spec,
        out_shape=jax.ShapeDtypeStruct((db, heads, kvl), F32),
        compiler_params=_cparams(("arbitrary",)),
        name="mla_decode_attention",
    )(page_table, ql_s, qr_s, kvn, krn, cache_kv, cache_krt)


def _dec_out_kernel(o_ref, wuv_ref, g_ref, out_ref):
    heads = o_ref.shape[0]
    out_ref[...] = _uv_norm([o_ref[hh] for hh in range(heads)], wuv_ref, g_ref[...])


def _decode_out(o_hm, wuv, g):
    heads, rows, kvl = o_hm.shape
    vdim = wuv.shape[2]
    return pl.pallas_call(
        _dec_out_kernel,
        out_shape=jax.ShapeDtypeStruct((rows, heads * vdim), BF16),
        name="mla_decode_out",
    )(o_hm, wuv, g)


def _head_norm_gate(o, gn, rg):
    mu = jnp.mean(o, axis=-1, keepdims=True)
    var = jnp.mean(jnp.square(o - mu), axis=-1, keepdims=True)
    y = (o - mu) * lax.rsqrt(var + EPS) * gn
    return y * jax.nn.silu(rg)


def _ret_kernel(rq_ref, rk_ref, rv_ref, rg_ref, rkm_ref, rvm_ref, dec_ref, qd_ref, kd_ref, kdm_ref, gc_ref,
                gn_ref, o_ref, st_ref, *, chunk, nchunks):
    dec = dec_ref[0]
    qd = qd_ref[0]
    kd = kd_ref[0]
    gc = gc_ref[0]
    gn = gn_ref[...]
    s0 = _dot_tn((rkm_ref[...].astype(F32) * kdm_ref[0]).astype(BF16), rvm_ref[...])

    def body(c, state):
        r0 = pl.multiple_of(c * chunk, chunk)
        q = rq_ref[pl.ds(r0, chunk), :]
        k = rk_ref[pl.ds(r0, chunk), :]
        v = rv_ref[pl.ds(r0, chunk), :]
        rg = rg_ref[pl.ds(r0, chunk), :].astype(F32)
        scores = _dot_nt(q, k) * dec
        o = _dot(scores.astype(BF16), v) + _dot(q, state.astype(BF16)) * qd
        new_state = state * gc + _dot_tn((k.astype(F32) * kd).astype(BF16), v)
        o_ref[pl.ds(r0, chunk), :] = _head_norm_gate(o, gn, rg).astype(BF16)
        return new_state

    st_ref[0, 0, 0] = lax.fori_loop(0, nchunks, body, s0, unroll=2 if nchunks % 2 == 0 else 1)


def _retention_prompt(rb, rkm, rvm, tabs, gn, *, batch, seq, heads, dk, chunk):
    dec, qd, kd, kdm, gc = tabs
    nmeta = rkm.shape[0]
    col = lambda off: (lambda b, h: (b, off * heads + h))
    tab = lambda b, h: (h, 0, 0)
    return pl.pallas_call(
        functools.partial(_ret_kernel, chunk=chunk, nchunks=seq // chunk),
        grid=(batch, heads),
        in_specs=[pl.BlockSpec((seq, dk), col(0)), pl.BlockSpec((seq, dk), col(1)),
                  pl.BlockSpec((seq, dk), col(2)), pl.BlockSpec((seq, dk), col(3)),
                  pl.BlockSpec((nmeta, dk), lambda b, h: (0, h)), pl.BlockSpec((nmeta, dk), lambda b, h: (0, h)),
                  pl.BlockSpec((1, chunk, chunk), tab), pl.BlockSpec((1, chunk, dk), tab),
                  pl.BlockSpec((1, chunk, dk), tab), pl.BlockSpec((1, nmeta, dk), tab),
                  pl.BlockSpec((1, dk, dk), tab),
                  pl.BlockSpec((1, dk), lambda b, h: (0, h))],
        out_specs=[pl.BlockSpec((seq, dk), lambda b, h: (b, h)),
                   pl.BlockSpec((1, 1, 1, dk, dk), lambda b, h: (0, b, h, 0, 0))],
        out_shape=[jax.ShapeDtypeStruct((batch * seq, heads * dk), BF16),
                   jax.ShapeDtypeStruct((1, batch, heads, dk, dk), F32)],
        compiler_params=_cparams(("parallel", "parallel")),
        name="retention_prompt",
    )(rb, rb, rb, rb, rkm, rvm, dec, qd, kd, kdm, gc, gn)


def _ret_s_kernel(q_ref, k_ref, v_ref, g_ref, gam_ref, gn_ref, st_ref, o_ref, ns_ref, *, heads, dk, bs):
    rowid = lax.broadcasted_iota(jnp.int32, (8, dk), 0)
    for hh in range(heads):
        sl = slice(hh * dk, (hh + 1) * dk)
        gam = gam_ref[hh]
        outs = []
        for i in range(bs):
            q = q_ref[i:i + 1, sl]
            k = k_ref[i:i + 1, sl]
            v = v_ref[i:i + 1, sl]
            state = st_ref[0, i, hh]
            k8 = jnp.where(rowid == 0, jnp.broadcast_to(k, (8, dk)), 0.0).astype(BF16)
            v8 = jnp.broadcast_to(v, (8, dk)).astype(BF16)
            ns_ref[0, i, hh] = state * gam[0:1, :] + _dot_tn(k8, v8)
            q8 = jnp.broadcast_to(q, (8, dk)).astype(BF16)
            qs = _dot(q8, state.astype(BF16))[0:1, :]
            qk = jnp.sum(q.astype(BF16).astype(F32) * k.astype(BF16).astype(F32), axis=1, keepdims=True)
            outs.append(qk * v.astype(BF16).astype(F32) + qs * gam[0:1, :])
        o = jnp.concatenate(outs, axis=0)
        o_ref[:, sl] = _head_norm_gate(o, gn_ref[:, sl], g_ref[:, sl]).astype(BF16)


def _retention_sample(rq, rk, rv, rg, gam, gn, state, *, heads, dk, bs):
    db = rq.shape[0]
    n = heads * dk
    row = lambda i: (i, 0)
    st_spec = pl.BlockSpec((1, bs, heads, dk, dk), lambda i: (0, i, 0, 0, 0))
    return pl.pallas_call(
        functools.partial(_ret_s_kernel, heads=heads, dk=dk, bs=bs),
        grid=(db // bs,),
        in_specs=[pl.BlockSpec((bs, n), row), pl.BlockSpec((bs, n), row), pl.BlockSpec((bs, n), row),
                  pl.BlockSpec((bs, n), row), pl.BlockSpec((heads, 8, dk), lambda i: (0, 0, 0)),
                  pl.BlockSpec((1, n), lambda i: (0, 0)), st_spec],
        out_specs=[pl.BlockSpec((bs, n), row), st_spec],
        out_shape=[jax.ShapeDtypeStruct((db, n), BF16), jax.ShapeDtypeStruct(state.shape, F32)],
        compiler_params=_cparams(("parallel",)),
        name="retention_sample",
    )(rq, rk, rv, rg, gam, gn, state)


def _out_kernel(om_ref, or_ref, x_ref, wo1_ref, wo2_ref, g_ref, wg_ref, bg_ref, wr_ref, br_ref,
                x1_ref, h2_ref, rt_ref, *, nsteps, ngroups, nexperts):
    i = pl.program_id(0)

    @pl.when(i < nsteps)
    def _():
        _out_rows(om_ref, or_ref, x_ref, wo1_ref, wo2_ref, g_ref, wg_ref, bg_ref, wr_ref, br_ref,
                  x1_ref, h2_ref, rt_ref, ngroups=ngroups, nexperts=nexperts)

    @pl.when(i >= nsteps)
    def _():
        h2_ref[...] = jnp.zeros_like(h2_ref)


def _out_rows(om_ref, or_ref, x_ref, wo1_ref, wo2_ref, g_ref, wg_ref, bg_ref, wr_ref, br_ref,
              x1_ref, h2_ref, rt_ref, *, ngroups, nexperts):
    x1 = x_ref[...] + _dot(om_ref[...], wo1_ref[...]) + _dot(or_ref[...], wo2_ref[...])
    x1_ref[...] = x1
    hb = _rms(x1, g_ref[...]).astype(BF16)
    _store_chunked(h2_ref, hb.astype(F32), x1.shape[1] // LANES)
    tm = x1.shape[0]
    lane = lax.broadcasted_iota(jnp.int32, (tm, LANES), 1)
    gl = jnp.where(lane < ngroups, _dot(hb, wg_ref[...]) + bg_ref[...], NEG)
    gmax = jnp.max(gl, axis=1, keepdims=True)
    gsum = jnp.sum(jnp.exp(gl - gmax), axis=1, keepdims=True)
    g_w = 1.0 / gsum
    g_idx = jnp.min(jnp.where(gl == gmax, lane, LANES), axis=1, keepdims=True)
    epg = nexperts // ngroups
    in_group = (lane >= g_idx * epg) & (lane < (g_idx + 1) * epg)
    el = jnp.where(in_group, _dot(hb, wr_ref[...]) + br_ref[...], NEG)
    emax = jnp.max(el, axis=1, keepdims=True)
    esum = jnp.sum(jnp.exp(el - emax), axis=1, keepdims=True)
    idx1 = jnp.min(jnp.where(el == emax, lane, LANES), axis=1, keepdims=True)
    el2 = jnp.where(lane == idx1, NEG, el)
    emax2 = jnp.max(el2, axis=1, keepdims=True)
    idx2 = jnp.min(jnp.where(el2 == emax2, lane, LANES), axis=1, keepdims=True)
    p1 = 1.0 / esum
    p2 = jnp.exp(emax2 - emax) / esum
    w1 = p1 / (p1 + p2) * g_w
    w2 = p2 / (p1 + p2) * g_w
    rt_ref[...] = jnp.where(lane == 0, w1, jnp.where(lane == 1, w2, jnp.where(
        lane == 2, idx1.astype(F32), jnp.where(lane == 3, idx2.astype(F32), 0.0))))


def _out_kernel_into(om_ref, or_ref, x_ref, wo1_ref, wo2_ref, g_ref, wg_ref, bg_ref, wr_ref, br_ref,
                     h2_prev_ref, x1_ref, h2_ref, rt_ref, **kw):
    del h2_prev_ref
    _out_kernel(om_ref, or_ref, x_ref, wo1_ref, wo2_ref, g_ref, wg_ref, bg_ref, wr_ref, br_ref,
                x1_ref, h2_ref, rt_ref, **kw)


def _out_proj(om, orr, x, wo1, wo2, g, wg, bg, wr, br, *, rows, tm, h2_rows, h2_row0, h2_prev,
              ngroups, nexperts):
    d = x.shape[1]
    n1, n2 = om.shape[1], orr.shape[1]
    nsteps = rows // tm
    ntail = (h2_rows - rows) // tm if h2_prev is None else 0
    row = lambda i: (jnp.minimum(i, nsteps - 1), 0)
    fix = lambda i: (0, 0)
    in_specs = [pl.BlockSpec((tm, n1), row), pl.BlockSpec((tm, n2), row), pl.BlockSpec((tm, d), row),
                pl.BlockSpec((n1, d), fix), pl.BlockSpec((n2, d), fix), pl.BlockSpec((1, d), fix),
                pl.BlockSpec((d, LANES), fix), pl.BlockSpec((1, LANES), fix),
                pl.BlockSpec((d, LANES), fix), pl.BlockSpec((1, LANES), fix)]
    args = [om, orr, x, wo1, wo2, g, wg, bg, wr, br]
    body, aliases = _out_kernel, {}
    if h2_prev is not None:
        in_specs.append(pl.BlockSpec(memory_space=pl.ANY))
        args.append(h2_prev)
        body, aliases = _out_kernel_into, {len(args) - 1: 1}
    return pl.pallas_call(
        functools.partial(body, nsteps=nsteps, ngroups=ngroups, nexperts=nexperts),
        grid=(nsteps + ntail,),
        in_specs=in_specs,
        out_specs=[pl.BlockSpec((tm, d), row),
                   pl.BlockSpec((tm * (d // LANES), LANES), lambda i: (h2_row0 // tm + i, 0)),
                   pl.BlockSpec((tm, LANES), row)],
        out_shape=[jax.ShapeDtypeStruct((rows, d), F32),
                   jax.ShapeDtypeStruct((h2_rows * (d // LANES), LANES), F32),
                   jax.ShapeDtypeStruct((rows, LANES), F32)],
        input_output_aliases=aliases,
        compiler_params=_cparams(("arbitrary",)),
        name="out_proj_router",
    )(*args)


def _moe_kernel(te_ref, s0_ref, nr_ref, nu_ref, sp_ref, h2_hbm, wg_ref, wu_ref, wd_ref, ys_hbm,
                xbuf, ybuf, gsem, ssem, *, tm, ntok, nck, pitch, ntiles):
    i = pl.program_id(0)
    nused = nu_ref[0]
    npair = 2 * ntok

    def hbm_row(ref, t):
        return ref.at[pl.ds(t * nck, nck), :]

    def buf_row(buf, slot, r):
        return buf.at[slot, pl.ds(r * pitch, nck), :]

    def gather(tile, slot):
        src0 = s0_ref[tile]
        last = src0 + nr_ref[tile] - 1
        for r in range(tm):
            pair = sp_ref[jnp.minimum(src0 + r, last)]
            tok = pair - jnp.where(pair >= ntok, ntok, 0)
            pltpu.make_async_copy(hbm_row(h2_hbm, tok), buf_row(xbuf, slot, r), gsem.at[slot]).start()

    def scatter(tile, slot):
        src0 = s0_ref[tile]
        n = nr_ref[tile]
        for r in range(tm):
            pair = sp_ref[jnp.minimum(src0 + r, npair - 1)]
            dst = jnp.where(r < n, pair, npair + slot * tm + r)
            pltpu.make_async_copy(buf_row(ybuf, slot, r), hbm_row(ys_hbm, dst), ssem.at[slot]).start()

    def wait_tile(buf, sem, slot):
        pltpu.make_async_copy(h2_hbm.at[pl.ds(0, tm * nck), :], buf.at[slot, pl.ds(0, tm * nck), :],
                              sem.at[slot]).wait()

    @pl.when(i == 0)
    def _():
        gather(0, 0)
        ybuf[0] = jnp.zeros(ybuf.shape[1:], F32)
        for s in range(2):
            pltpu.make_async_copy(ybuf.at[0, pl.ds(0, tm * nck), :],
                                  ys_hbm.at[pl.ds((npair + s * tm) * nck, tm * nck), :], ssem.at[0]).start()
        for s in range(2):
            wait_tile(ybuf, ssem, 0)

    @pl.when(i + 1 < nused)
    def _():
        gather(i + 1, (i + 1) % 2)

    @pl.when(i < nused)
    def _():
        slot = i % 2
        wait_tile(xbuf, gsem, slot)

        @pl.when(i >= 2)
        def _():
            wait_tile(ybuf, ssem, slot)

        x = _load_chunked(xbuf.at[slot], tm, nck, pitch)
        a = jax.nn.silu(_dot(x, wg_ref[0])) * _dot(x, wu_ref[0])
        _store_chunked(ybuf.at[slot], _dot(a, wd_ref[0]), pitch)
        scatter(i, slot)

    @pl.when(i == ntiles - 1)
    def _():
        @pl.when(nused >= 2)
        def _():
            wait_tile(ybuf, ssem, nused % 2)
        wait_tile(ybuf, ssem, (nused + 1) % 2)


def _moe(tile_expert, tile_src0, tile_rows, nused, sorted_pairs, h2, w_gate, w_up, w_down, *, tm):
    ntiles = tile_expert.shape[0]
    ntok = sorted_pairs.shape[0] // 2
    d, f = w_gate.shape[1:]
    nck = d // LANES
    pitch = nck + 1
    wspec = lambda shape: pl.BlockSpec(shape, lambda i, te, s0, nr, nu, sp: (te[i], 0, 0))
    grid_spec = pltpu.PrefetchScalarGridSpec(
        num_scalar_prefetch=5,
        grid=(ntiles,),
        in_specs=[pl.BlockSpec(memory_space=pl.ANY), wspec((1, d, f)), wspec((1, d, f)), wspec((1, f, d))],
        out_specs=pl.BlockSpec(memory_space=pl.ANY),
        scratch_shapes=[pltpu.VMEM((2, tm * pitch, LANES), F32), pltpu.VMEM((2, tm * pitch, LANES), F32),
                        pltpu.SemaphoreType.DMA((2,)), pltpu.SemaphoreType.DMA((2,))],
    )
    return pl.pallas_call(
        functools.partial(_moe_kernel, tm=tm, ntok=ntok, nck=nck, pitch=pitch, ntiles=ntiles),
        grid_spec=grid_spec,
        out_shape=jax.ShapeDtypeStruct(((2 * ntok + 2 * tm) * nck, LANES), F32),
        compiler_params=_cparams(("arbitrary",)),
        name="moe_experts",
    )(tile_expert, tile_src0, tile_rows, nused, sorted_pairs, h2, w_gate, w_up, w_down)


def _comb_kernel(x1_ref, ya_ref, yb_ref, rt_ref, g_ref, y_ref):
    tm, d = x1_ref.shape
    nck = d // LANES
    rt = rt_ref[...]
    x = (x1_ref[...] + rt[:, 0:1] * _load_chunked(ya_ref, tm, nck, nck)
         + rt[:, 1:2] * _load_chunked(yb_ref, tm, nck, nck))
    y_ref[...] = _rms(x, g_ref[...])


def _combine(x1, ys, rt, g, *, tm, row0, ntok, rows):
    d = x1.shape[1]
    nck = d // LANES
    return pl.pallas_call(
        _comb_kernel,
        grid=(rows // tm,),
        in_specs=[pl.BlockSpec((tm, d), lambda i: (i, 0)),
                  pl.BlockSpec((tm * nck, LANES), lambda i: (row0 // tm + i, 0)),
                  pl.BlockSpec((tm * nck, LANES), lambda i: ((ntok + row0) // tm + i, 0)),
                  pl.BlockSpec((tm, LANES), lambda i: (i, 0)), pl.BlockSpec((1, d), lambda i: (0, 0))],
        out_specs=pl.BlockSpec((tm, d), lambda i: (i, 0)),
        out_shape=jax.ShapeDtypeStruct((rows, d), F32),
        compiler_params=_cparams(("parallel",)),
        name="moe_combine_norm",
    )(x1, ys, ys, rt, g)


def _rope_tables(pos, dim):
    half = dim // 2
    inv = jnp.exp(jnp.arange(half, dtype=F32) * (-2.0 * math.log(ROPE_BASE) / dim))
    ang = pos.astype(F32)[:, None] * inv[None, :]
    return jnp.cos(ang), jnp.sin(ang)


def _rot_cols(w):
    half = w.shape[-1] // 2
    return jnp.concatenate([-w[..., half:], w[..., :half]], axis=-1)


def _pick_tile(rows, target):
    t = min(rows, target)
    while rows % t:
        t //= 2
    return t


def _dispatch(route, ntok, nexperts, tm, ntiles):
    i32 = jnp.int32
    ids = jnp.concatenate([route[:, 2], route[:, 3]]).astype(i32)
    npair = 2 * ntok
    _, sorted_pairs = lax.sort((ids, jnp.arange(npair, dtype=i32)), num_keys=1, is_stable=True)
    ex = jnp.arange(nexperts, dtype=i32)
    counts = jnp.sum((ids[:, None] == ex[None, :]).astype(i32), axis=0)
    starts = jnp.cumsum(counts) - counts
    tcount = (counts + tm - 1) // tm
    tend = jnp.cumsum(tcount)
    tstart = tend - tcount
    nused = tend[-1]
    t = jnp.arange(ntiles, dtype=i32)
    tile_expert = jnp.minimum(jnp.sum((tend[None, :] <= t[:, None]).astype(i32), axis=1), nexperts - 1)
    pick = lambda tab: jnp.sum(jnp.where(tile_expert[:, None] == ex[None, :], tab[None, :], 0), axis=1)
    local = t - pick(tstart)
    tile_src0 = jnp.clip(pick(starts) + local * tm, 0, npair - 1)
    tile_rows = jnp.where(t < nused, jnp.clip(pick(counts) - local * tm, 0, tm), 0)
    return (tile_expert.astype(i32), tile_src0.astype(i32), tile_rows.astype(i32),
            nused.reshape(1).astype(i32), sorted_pairs)


def kernel(x_prompt, x_sample, cache_kv_latent, cache_k_rope, state_retention, page_table, meta_tokens,
           ln_mix_g, w_in, q_norm_g, w_uq, w_uk, kv_norm_g, w_uv, mla_out_g, ret_gn_g, w_o, ln_ffn_g,
           w_group, b_group, w_router, b_router, w_gate, w_up, w_down, final_g):
    batch, seq, d = x_prompt.shape
    db, dec_seq, _ = x_sample.shape
    depth = w_in.shape[0]
    assert depth == 1 and dec_seq == 1
    ql = q_norm_g.shape[1]
    kvl = kv_norm_g.shape[1]
    mh, nr = w_uq.shape[2], w_uq.shape[3]
    nope = w_uk.shape[3]
    rope = nr - nope
    vdim = w_uv.shape[3]
    rdim = ret_gn_g.shape[1]
    dk = state_retention.shape[3]
    rh = rdim // dk
    ngroups = w_group.shape[2]
    nexperts = w_router.shape[2]
    page = cache_kv_latent.shape[2]
    npages = page_table.shape[1]
    past_len = npages * page
    chunk = min(2 * ROW_TILE, seq)
    tk = min(4 * ROW_TILE, seq)
    nmain = batch * seq
    naux = -(-(db + N_META) // LANES) * LANES
    ts = -(-db // ROW_TILE) * ROW_TILE
    assert seq % 256 == 0 and seq % tk == 0 and ts <= naux and db % 8 == 0 and rh == mh

    wit = jnp.swapaxes(w_in[0], 0, 1)
    o_kr = ql + kvl
    o_ret = o_kr + rope
    w_a = jnp.concatenate([wit[:o_ret], jnp.swapaxes(_rot_cols(jnp.swapaxes(wit[o_kr:o_ret], 0, 1)), 0, 1)],
                          axis=0).astype(BF16)
    w_b = wit[o_ret:].astype(BF16)
    wq = w_uq[0]
    wq_rope = wq[:, :, nope:]
    wuq = jnp.concatenate([wq[:, :, :nope].reshape(ql, mh * nope), wq_rope.reshape(ql, mh * rope),
                           _rot_cols(wq_rope).reshape(ql, mh * rope)], axis=1).astype(BF16)
    wukt = jnp.transpose(w_uk[0], (1, 2, 0)).astype(BF16)
    wuv = jnp.transpose(w_uv[0], (1, 0, 2)).astype(BF16)
    wo = w_o[0].astype(BF16)
    wo1, wo2 = wo[:mh * vdim], wo[mh * vdim:]
    wg_pad = jnp.zeros((d, LANES), F32).at[:, :ngroups].set(w_group[0]).astype(BF16)
    bg_pad = jnp.zeros((1, LANES), F32).at[0, :ngroups].set(b_group[0])
    wr_pad = jnp.zeros((d, LANES), F32).at[:, :nexperts].set(w_router[0]).astype(BF16)
    br_pad = jnp.zeros((1, LANES), F32).at[0, :nexperts].set(b_router[0])

    pos_main = N_META + jnp.arange(seq, dtype=jnp.int32)
    pos_aux = jnp.concatenate([jnp.full((db,), past_len, jnp.int32), jnp.arange(N_META, dtype=jnp.int32),
                               jnp.zeros((naux - db - N_META,), jnp.int32)])

    def tables(pos):
        c64, s64 = _rope_tables(pos, rope)
        c128, s128 = _rope_tables(pos, dk)
        cs = jnp.concatenate([c64, c64], axis=1)
        sn = jnp.concatenate([s64, s64], axis=1)
        return (cs, sn, jnp.tile(cs, (1, 2)), jnp.tile(sn, (1, 2)),
                jnp.concatenate([c128, c128], axis=1), jnp.concatenate([-s128, s128], axis=1))

    log_g = jnp.log1p(-jnp.exp2(-5.0 - jnp.arange(rh, dtype=F32)))
    n = jnp.arange(chunk, dtype=F32)
    diff = n[:, None] - n[None, :]
    t_dec = jnp.where(diff >= 0, jnp.exp(log_g[:, None, None] * jnp.maximum(diff, 0.0)), 0.0)
    t_qd = jnp.broadcast_to(jnp.exp((n[None, :] + 1.0) * log_g[:, None])[:, :, None], (rh, chunk, dk))
    k_decay = jnp.exp((chunk - 1.0 - n)[None, :] * log_g[:, None])
    t_kd = jnp.broadcast_to(k_decay[:, :, None], (rh, chunk, dk))
    t_kdm = jnp.broadcast_to(k_decay[:, chunk - N_META:, None], (rh, N_META, dk))
    t_gc = jnp.broadcast_to(jnp.exp(chunk * log_g)[:, None, None], (rh, dk, dk))
    t_gam = jnp.broadcast_to(jnp.exp(log_g)[:, None, None], (rh, 8, dk))

    x_main = x_prompt.reshape(nmain, d)
    x_aux = jnp.concatenate([x_sample.reshape(db, d), meta_tokens.astype(F32),
                             jnp.zeros((naux - db - N_META, d), F32)], axis=0)
    g_mix = ln_mix_g[0][None]
    scale = float(nope + rope) ** -0.5
    tm_main = _pick_tile(seq, 512)
    tm_aux = _pick_tile(naux, 512)

    def token_front(x, pos, tm):
        cs, sn, cs8, sn8, cos_r, sin_r = tables(pos)
        cq, kv, kvb, kr, krb = _in_a(x, g_mix, w_a, q_norm_g, kv_norm_g, cs, sn,
                                     tm=tm, ql=ql, kvl=kvl, rope=rope)
        rb = _in_b(x, g_mix, w_b, cos_r, sin_r, tm=tm, heads=rh, dk=dk)
        q_lat, q_rope = _q_proj(cq, wuq, wukt, cs8, sn8, tm=tm, heads=mh, nope=nope, rope=rope, scale=scale)
        return kv, kvb, kr, krb, rb, q_lat, q_rope

    kv_m, kvb_m, kr_m, krb_m, rb_m, ql_m, qr_m = token_front(x_main, pos_main, tm_main)
    kv_a, kvb_a, kr_a, krb_a, rb_a, ql_a, qr_a = token_front(x_aux, pos_aux, tm_aux)
    meta = slice(db, db + N_META)

    om_m = _attention(ql_m, qr_m, kvb_m, krb_m, kvb_a[meta], krb_a[meta], wuv, mla_out_g,
                      batch=batch, seq=seq, tq=ROW_TILE, tk=tk)
    rdk = rh * dk
    or_m, st_p = _retention_prompt(rb_m, rb_a[meta, rdk:2 * rdk], rb_a[meta, 2 * rdk:3 * rdk],
                                   (t_dec, t_qd, t_kd, t_kdm, t_gc), ret_gn_g,
                                   batch=batch, seq=seq, heads=rh, dk=dk, chunk=chunk)

    ql_s = jnp.transpose(ql_a[:, :db], (1, 0, 2))
    qr_s = jnp.transpose(qr_a[:, :db], (1, 0, 2))
    o_s = _decode_attention(page_table, ql_s, qr_s, kv_a[:db, None, :], kr_a[:db, None, :],
                            cache_kv_latent[0], jnp.swapaxes(cache_k_rope[0], 1, 2),
                            gpages=min(16, npages), nbuf=3)
    om_s = _decode_out(jnp.transpose(o_s, (1, 0, 2)), wuv, mla_out_g)
    rs = rb_a[:db].astype(F32)
    or_s, st_s = _retention_sample(rs[:, :rdk], rs[:, rdk:2 * rdk], rs[:, 2 * rdk:3 * rdk], rs[:, 3 * rdk:],
                                   t_gam, ret_gn_g, state_retention, heads=rh, dk=dk, bs=8)
    pad_rows = lambda a: jnp.concatenate([a, jnp.zeros((ts - db, a.shape[1]), a.dtype)], axis=0)
    om_a, or_a = pad_rows(om_s), pad_rows(or_s)

    g_ffn = ln_ffn_g[0][None]
    ntok = nmain + ts
    tm_o = _pick_tile(nmain, 256)
    outp = functools.partial(_out_proj, h2_rows=nmain + -(-ts // tm_o) * tm_o, ngroups=ngroups, nexperts=nexperts)
    x1_m, h2, rt_m = outp(om_m, or_m, x_main, wo1, wo2, g_ffn, wg_pad, bg_pad, wr_pad, br_pad,
                          rows=nmain, tm=tm_o, h2_row0=0, h2_prev=None)
    x1_a, h2, rt_a = outp(om_a, or_a, x_aux, wo1, wo2, g_ffn, wg_pad, bg_pad, wr_pad, br_pad,
                          rows=ts, tm=ROW_TILE, h2_row0=nmain, h2_prev=h2)

    route = jnp.concatenate([rt_m[:, :4], rt_a[:, :4]], axis=0)
    tm_e = ROW_TILE
    ntiles = -(-(2 * ntok + nexperts * (tm_e - 1)) // tm_e)
    ys = _moe(*_dispatch(route, ntok, nexperts, tm_e, ntiles), h2, w_gate[0], w_up[0], w_down[0], tm=tm_e)
    fg = final_g[None]
    y_m = _combine(x1_m, ys, rt_m, fg, tm=ROW_TILE, row0=0, ntok=ntok, rows=nmain)
    y_a = _combine(x1_a, ys, rt_a, fg, tm=ROW_TILE, row0=nmain, ntok=ntok, rows=ts)

    def with_meta(a_main, a_aux):
        w = a_main.shape[1]
        m = jnp.broadcast_to(a_aux[meta][None], (batch, N_META, w))
        return jnp.concatenate([m, a_main.reshape(batch, seq, w)], axis=1)[None]

    return (y_m.reshape(batch, seq, d), y_a[:db].reshape(db, 1, d),
            with_meta(kv_m, kv_a), with_meta(kr_m, kr_a), st_p,
            kv_a[:db].reshape(1, db, 1, kvl), kr_a[:db].reshape(1, db, 1, rope), st_s)
```

```python
import functools
import math

import jax
import jax.numpy as jnp
from jax import lax
from jax.experimental import pallas as pl
from jax.experimental.pallas import tpu as pltpu

F32 = jnp.float32
BF16 = jnp.bfloat16

N_META = 16
ROPE_BASE = 10000.0
EPS = 1e-6
NEG = -1e30
EXPERTS_PER_GROUP = 8
LANES = 128
ROW_TILE = 128
VMEM_LIMIT = 56 * 1024 * 1024


def _cparams(sem, vmem=VMEM_LIMIT):
    return pltpu.CompilerParams(dimension_semantics=sem, vmem_limit_bytes=vmem)


def _rms(x, g):
    return x * lax.rsqrt(jnp.mean(x * x, axis=-1, keepdims=True) + EPS) * g


def _dot(a, b):
    return jnp.dot(a, b, preferred_element_type=F32)


def _dot_nt(a, b):
    return lax.dot_general(a, b, (((1,), (1,)), ((), ())), preferred_element_type=F32)


def _dot_tn(a, b):
    return lax.dot_general(a, b, (((0,), (0,)), ((), ())), preferred_element_type=F32)


def _store_chunked(ref, val, pitch, base=0):
    rows, d = val.shape
    for c in range(d // LANES):
        ref[pl.ds(base + c, rows, stride=pitch), :] = val[:, c * LANES:(c + 1) * LANES]


def _load_chunked(ref, rows, nchunk, pitch, base=0):
    return jnp.concatenate([ref[pl.ds(base + c, rows, stride=pitch), :] for c in range(nchunk)], axis=1)


def _in_a_kernel(x_ref, g_ref, w_ref, qg_ref, kvg_ref, cs_ref, sn_ref,
                 cq_ref, kv_ref, kvb_ref, kr_ref, krb_ref, *, ql, kvl, rope):
    h = _rms(x_ref[...], g_ref[...]).astype(BF16)
    c = _dot_nt(h, w_ref[...])
    cq_ref[...] = _rms(c[:, :ql], qg_ref[...]).astype(BF16)
    kv = _rms(c[:, ql:ql + kvl], kvg_ref[...])
    kv_ref[...] = kv
    kvb_ref[...] = kv.astype(BF16)
    a = c[:, ql + kvl:ql + kvl + rope]
    b = c[:, ql + kvl + rope:]
    kr = a * cs_ref[...] + b * sn_ref[...]
    kr_ref[...] = kr
    krb_ref[...] = kr.astype(BF16)


def _in_a(x, g, w_a, qg, kvg, cs, sn, *, tm, ql, kvl, rope):
    rows, d = x.shape
    n = w_a.shape[0]
    row = lambda i: (i, 0)
    fix = lambda i: (0, 0)
    tab = lambda i: (i % (cs.shape[0] // tm), 0)
    return pl.pallas_call(
        functools.partial(_in_a_kernel, ql=ql, kvl=kvl, rope=rope),
        grid=(rows // tm,),
        in_specs=[pl.BlockSpec((tm, d), row), pl.BlockSpec((1, d), fix), pl.BlockSpec((n, d), fix),
                  pl.BlockSpec((1, ql), fix), pl.BlockSpec((1, kvl), fix),
                  pl.BlockSpec((tm, rope), tab), pl.BlockSpec((tm, rope), tab)],
        out_specs=[pl.BlockSpec((tm, ql), row), pl.BlockSpec((tm, kvl), row), pl.BlockSpec((tm, kvl), row),
                   pl.BlockSpec((tm, rope), row), pl.BlockSpec((tm, rope), row)],
        out_shape=[jax.ShapeDtypeStruct((rows, ql), BF16), jax.ShapeDtypeStruct((rows, kvl), F32),
                   jax.ShapeDtypeStruct((rows, kvl), BF16), jax.ShapeDtypeStruct((rows, rope), F32),
                   jax.ShapeDtypeStruct((rows, rope), BF16)],
        compiler_params=_cparams(("parallel",)),
        name="in_proj_a",
    )(x, g, w_a, qg, kvg, cs, sn)


def _in_b_kernel(x_ref, g_ref, w_ref, cos_ref, sin_ref, out_ref, h_scr, *, heads, dk):
    j = pl.program_id(1)

    @pl.when(j == 0)
    def _():
        h_scr[...] = _rms(x_ref[...], g_ref[...]).astype(BF16)

    c = _dot_nt(h_scr[...], w_ref[...])

    @pl.when(j < 2)
    def _():
        scale = jnp.where(j == 1, dk ** -0.5, 1.0).astype(F32)
        cos = cos_ref[...]
        sin = sin_ref[...]
        for hh in range(heads):
            blk = c[:, hh * dk:(hh + 1) * dk]
            r = blk * cos + pltpu.roll(blk, dk // 2, 1) * sin
            out_ref[:, hh * dk:(hh + 1) * dk] = (r * scale).astype(BF16)

    @pl.when(j >= 2)
    def _():
        out_ref[...] = c.astype(BF16)


def _in_b(x, g, w_b, cos, sin, *, tm, heads, dk):
    rows, d = x.shape
    n = heads * dk
    tab = lambda i, j: (i % (cos.shape[0] // tm), 0)
    return pl.pallas_call(
        functools.partial(_in_b_kernel, heads=heads, dk=dk),
        grid=(rows // tm, 4),
        in_specs=[pl.BlockSpec((tm, d), lambda i, j: (i, 0)), pl.BlockSpec((1, d), lambda i, j: (0, 0)),
                  pl.BlockSpec((n, d), lambda i, j: (j, 0)),
                  pl.BlockSpec((tm, dk), tab), pl.BlockSpec((tm, dk), tab)],
        out_specs=pl.BlockSpec((tm, n), lambda i, j: (i, j)),
        out_shape=jax.ShapeDtypeStruct((rows, 4 * n), BF16),
        scratch_shapes=[pltpu.VMEM((tm, d), BF16)],
        compiler_params=_cparams(("parallel", "arbitrary")),
        name="in_proj_b",
    )(x, g, w_b, cos, sin)


def _q_kernel(cq_ref, wuq_ref, wukt_ref, cos_ref, sin_ref, ql_ref, qr_ref, *, heads, nope, rope, scale):
    qf = _dot(cq_ref[...], wuq_ref[...])
    o1 = heads * nope
    o2 = o1 + heads * rope
    cos = cos_ref[...]
    sin = sin_ref[...]
    for hp in range(heads // 2):
        lo, hi = hp * 2 * rope, (hp + 1) * 2 * rope
        qr = (qf[:, o1 + lo:o1 + hi] * cos + qf[:, o2 + lo:o2 + hi] * sin) * scale
        qr_ref[2 * hp] = qr[:, :rope].astype(BF16)
        qr_ref[2 * hp + 1] = qr[:, rope:].astype(BF16)
    for hh in range(heads):
        qn = qf[:, hh * nope:(hh + 1) * nope].astype(BF16)
        ql_ref[hh] = (_dot(qn, wukt_ref[hh]) * scale).astype(BF16)


def _q_proj(cq, wuq, wukt, cos, sin, *, tm, heads, nope, rope, scale):
    rows, ql = cq.shape
    kvl = wukt.shape[2]
    nq = wuq.shape[1]
    tab = lambda i: (i % (cos.shape[0] // tm), 0)
    return pl.pallas_call(
        functools.partial(_q_kernel, heads=heads, nope=nope, rope=rope, scale=scale),
        grid=(rows // tm,),
        in_specs=[pl.BlockSpec((tm, ql), lambda i: (i, 0)), pl.BlockSpec((ql, nq), lambda i: (0, 0)),
                  pl.BlockSpec((heads, nope, kvl), lambda i: (0, 0, 0)),
                  pl.BlockSpec((tm, 2 * rope), tab), pl.BlockSpec((tm, 2 * rope), tab)],
        out_specs=[pl.BlockSpec((heads, tm, kvl), lambda i: (0, i, 0)),
                   pl.BlockSpec((heads, tm, rope), lambda i: (0, i, 0))],
        out_shape=[jax.ShapeDtypeStruct((heads, rows, kvl), BF16),
                   jax.ShapeDtypeStruct((heads, rows, rope), BF16)],
        compiler_params=_cparams(("parallel",)),
        name="q_proj",
    )(cq, wuq, wukt, cos, sin)


def _uv_norm(o_heads, wuv_ref, g):
    parts = [_dot(o.astype(BF16), wuv_ref[hh]) for hh, o in enumerate(o_heads)]
    return _rms(jnp.concatenate(parts, axis=1), g).astype(BF16)


def _widen(x, n):
    if n <= LANES:
        return x[:, :n]
    return jnp.concatenate([x] * (n // LANES), axis=1)


def _att_kernel(qi_ref, kj_ref, ql_ref, qr_ref, kv_ref, kr_ref, kvm_ref, krm_ref, wuv_ref, g_ref, o_ref,
                m_scr, l_scr, acc_scr, *, heads, tq, tk):
    qi = qi_ref[pl.program_id(1)]
    kj = kj_ref[pl.program_id(1)]
    kvl = ql_ref.shape[2]
    rows = heads * tq
    ql = ql_ref[...].reshape(rows, kvl)
    qr = qr_ref[...].reshape(rows, qr_ref.shape[2])

    def update(s, kv, first):
        m_cur = jnp.max(s, axis=1, keepdims=True)
        if first:
            m_new = jnp.broadcast_to(m_cur, (rows, LANES))
        else:
            m_prev = m_scr[...]
            m_new = jnp.maximum(m_prev, m_cur)
        p = jnp.exp(s - _widen(m_new, s.shape[1]))
        p_sum = jnp.sum(p, axis=1, keepdims=True)
        pv = _dot(p.astype(BF16), kv)
        if first:
            l_scr[...] = jnp.broadcast_to(p_sum, (rows, LANES))
            acc_scr[...] = pv
        else:
            alpha = jnp.exp(m_prev - m_new)
            l_scr[...] = alpha * l_scr[...] + p_sum
            acc_scr[...] = acc_scr[...] * _widen(alpha, kvl) + pv
        m_scr[...] = m_new

    @pl.when(kj == 0)
    def _():
        kvm = kvm_ref[...]
        update(_dot_nt(ql, kvm) + _dot_nt(qr, krm_ref[...]), kvm, True)

    last = (qi * tq + tq - 1) // tk

    @pl.when(kj < last)
    def _():
        kv = kv_ref[...]
        update(_dot_nt(ql, kv) + _dot_nt(qr, kr_ref[...]), kv, False)

    @pl.when(kj == last)
    def _():
        kv = kv_ref[...]
        s = _dot_nt(ql, kv) + _dot_nt(qr, kr_ref[...])
        qpos = qi * tq + lax.broadcasted_iota(jnp.int32, (rows, tk), 0) % tq
        kpos = kj * tk + lax.broadcasted_iota(jnp.int32, (rows, tk), 1)
        update(jnp.where(kpos <= qpos, s, NEG), kv, False)
        o = acc_scr[...] / _widen(l_scr[...], kvl)
        o_ref[...] = _uv_norm([o[hh * tq:(hh + 1) * tq] for hh in range(heads)], wuv_ref, g_ref[...])


def _attention(ql, qr, kvb, krb, kvm, krm, wuv, g, *, batch, seq, tq, tk):
    heads, rows, kvl = ql.shape
    rope = qr.shape[2]
    vdim = wuv.shape[2]
    nq, nk = seq // tq, seq // tk
    nmeta = kvm.shape[0]
    pairs = [(qi, kj) for qi in range(nq) for kj in range((qi * tq + tq - 1) // tk + 1)]
    qi_tab = jnp.asarray([p[0] for p in pairs], jnp.int32)
    kj_tab = jnp.asarray([p[1] for p in pairs], jnp.int32)
    qidx = lambda b, s, qt, kt: (0, b * nq + qt[s], 0)
    kidx = lambda b, s, qt, kt: (b * nk + kt[s], 0)
    fix2 = lambda b, s, qt, kt: (0, 0)
    grid_spec = pltpu.PrefetchScalarGridSpec(
        num_scalar_prefetch=2,
        grid=(batch, len(pairs)),
        in_specs=[pl.BlockSpec((heads, tq, kvl), qidx), pl.BlockSpec((heads, tq, rope), qidx),
                  pl.BlockSpec((tk, kvl), kidx), pl.BlockSpec((tk, rope), kidx),
                  pl.BlockSpec((nmeta, kvl), fix2), pl.BlockSpec((nmeta, rope), fix2),
                  pl.BlockSpec((heads, kvl, vdim), lambda b, s, qt, kt: (0, 0, 0)),
                  pl.BlockSpec((1, heads * vdim), fix2)],
        out_specs=pl.BlockSpec((tq, heads * vdim), lambda b, s, qt, kt: (b * nq + qt[s], 0)),
        scratch_shapes=[pltpu.VMEM((heads * tq, LANES), F32), pltpu.VMEM((heads * tq, LANES), F32),
                        pltpu.VMEM((heads * tq, kvl), F32)],
    )
    return pl.pallas_call(
        functools.partial(_att_kernel, heads=heads, tq=tq, tk=tk),
        grid_spec=grid_spec,
        out_shape=jax.ShapeDtypeStruct((rows, heads * vdim), BF16),
        compiler_params=_cparams(("parallel", "arbitrary")),
        name="mla_prompt_attention",
    )(qi_tab, kj_tab, ql, qr, kvb, krb, kvm, krm, wuv, g)


def _dec_kernel(pt_ref, ql_ref, qr_ref, kvn_ref, krn_ref, ckv_hbm, ckr_hbm, o_ref,
                kvbuf, krbuf, sem, *, nchunk, gpages, nbuf, total):
    b = pl.program_id(0)
    page = kvbuf.shape[2]
    kvl = kvbuf.shape[3]

    def copies(g, slot):
        bb = g // nchunk
        c0 = (g % nchunk) * gpages
        out = []
        for p in range(gpages):
            pg = pt_ref[bb, c0 + p]
            out.append(pltpu.make_async_copy(ckv_hbm.at[pg], kvbuf.at[slot, p], sem.at[slot]))
            out.append(pltpu.make_async_copy(ckr_hbm.at[pg], krbuf.at[slot, p], sem.at[slot]))
        return out

    def start(g, slot):
        for cp in copies(g, slot):
            cp.start()

    @pl.when(b == 0)
    def _():
        for g0 in range(min(nbuf - 1, total)):
            start(g0, g0 % nbuf)

    ql = ql_ref[0].astype(F32)
    qr = qr_ref[0].astype(F32)
    kvn = kvn_ref[0]
    krn = krn_ref[0]
    heads = ql.shape[0]
    m0 = jnp.sum(ql * kvn, axis=1, keepdims=True) + jnp.sum(qr * krn, axis=1, keepdims=True)
    l0 = jnp.ones((heads, 1), F32)
    acc0 = jnp.broadcast_to(kvn, (heads, kvl))

    def body(c, carry):
        m, l, acc = carry
        g = b * nchunk + c
        slot = g % nbuf
        for cp in copies(g, slot):
            cp.wait()
        nxt = g + nbuf - 1

        @pl.when(nxt < total)
        def _():
            start(nxt, nxt % nbuf)

        kvc = kvbuf[slot].reshape(gpages * page, kvl)
        krt = jnp.concatenate([krbuf[slot, p] for p in range(gpages)], axis=1)
        s = _dot_nt(ql, kvc) + _dot(qr, krt)
        m_new = jnp.maximum(m, jnp.max(s, axis=1, keepdims=True))
        alpha = jnp.exp(m - m_new)
        p = jnp.exp(s - m_new)
        l = l * alpha + jnp.sum(p, axis=1, keepdims=True)
        acc = acc * alpha + _dot(p, kvc)
        return m_new, l, acc

    m, l, acc = lax.fori_loop(0, nchunk, body, (m0, l0, acc0))
    o_ref[0] = acc / l


def _decode_attention(page_table, ql_s, qr_s, kvn, krn, cache_kv, cache_krt, *, gpages, nbuf):
    db, npages = page_table.shape
    heads, kvl = ql_s.shape[1:]
    rope = qr_s.shape[2]
    page = cache_kv.shape[1]
    nchunk = npages // gpages
    total = db * nchunk
    grid_spec = pltpu.PrefetchScalarGridSpec(
        num_scalar_prefetch=1,
        grid=(db,),
        in_specs=[pl.BlockSpec((1, heads, kvl), lambda b, pt: (b, 0, 0)),
                  pl.BlockSpec((1, heads, rope), lambda b, pt: (b, 0, 0)),
                  pl.BlockSpec((1, 1, kvl), lambda b, pt: (b, 0, 0)),
                  pl.BlockSpec((1, 1, rope), lambda b, pt: (b, 0, 0)),
                  pl.BlockSpec(memory_space=pl.ANY), pl.BlockSpec(memory_space=pl.ANY)],
        out_specs=pl.BlockSpec((1, heads, kvl), lambda b, pt: (b, 0, 0)),
        scratch_shapes=[pltpu.VMEM((nbuf, gpages, page, kvl), F32),
                        pltpu.VMEM((nbuf, gpages, rope, page), F32),
                        pltpu.SemaphoreType.DMA((nbuf,))],
    )
    return pl.pallas_call(
        functools.partial(_dec_kernel, nchunk=nchunk, gpages=gpages, nbuf=nbuf, total=total),
        grid_spec=grid_spec,
        out_shape=jax.ShapeDtypeStruct((db, heads, kvl), F32),
        compiler_params=_cparams(("arbitrary",)),
        name="mla_decode_attention",
    )(page_table, ql_s, qr_s, kvn, krn, cache_kv, cache_krt)


def _dec_out_kernel(o_ref, wuv_ref, g_ref, out_ref):
    heads = o_ref.shape[0]
    out_ref[...] = _uv_norm([o_ref[hh] for hh in range(heads)], wuv_ref, g_ref[...])


def _decode_out(o_hm, wuv, g):
    heads, rows, kvl = o_hm.shape
    vdim = wuv.shape[2]
    return pl.pallas_call(
        _dec_out_kernel,
        out_shape=jax.ShapeDtypeStruct((rows, heads * vdim), BF16),
        name="mla_decode_out",
    )(o_hm, wuv, g)


def _head_norm_gate(o, gn, rg):
    mu = jnp.mean(o, axis=-1, keepdims=True)
    var = jnp.mean(jnp.square(o - mu), axis=-1, keepdims=True)
    y = (o - mu) * lax.rsqrt(var + EPS) * gn
    return y * jax.nn.silu(rg)


def _ret_kernel(rq_ref, rk_ref, rv_ref, rg_ref, rkm_ref, rvm_ref, dec_ref, qd_ref, kd_ref, kdm_ref, gc_ref,
                gn_ref, o_ref, st_ref, *, chunk, nchunks):
    dec = dec_ref[0]
    qd = qd_ref[0]
    kd = kd_ref[0]
    gc = gc_ref[0]
    gn = gn_ref[...]
    s0 = _dot_tn((rkm_ref[...].astype(F32) * kdm_ref[0]).astype(BF16), rvm_ref[...])

    def body(c, state):
        r0 = pl.multiple_of(c * chunk, chunk)
        q = rq_ref[pl.ds(r0, chunk), :]
        k = rk_ref[pl.ds(r0, chunk), :]
        v = rv_ref[pl.ds(r0, chunk), :]
        rg = rg_ref[pl.ds(r0, chunk), :].astype(F32)
        scores = _dot_nt(q, k) * dec
        o = _dot(scores.astype(BF16), v) + _dot(q, state.astype(BF16)) * qd
        new_state = state * gc + _dot_tn((k.astype(F32) * kd).astype(BF16), v)
        o_ref[pl.ds(r0, chunk), :] = _head_norm_gate(o, gn, rg).astype(BF16)
        return new_state

    st_ref[0, 0, 0] = lax.fori_loop(0, nchunks, body, s0, unroll=2 if nchunks % 2 == 0 else 1)


def _retention_prompt(rb, rkm, rvm, tabs, gn, *, batch, seq, heads, dk, chunk):
    dec, qd, kd, kdm, gc = tabs
    nmeta = rkm.shape[0]
    col = lambda off: (lambda b, h: (b, off * heads + h))
    tab = lambda b, h: (h, 0, 0)
    return pl.pallas_call(
        functools.partial(_ret_kernel, chunk=chunk, nchunks=seq // chunk),
        grid=(batch, heads),
        in_specs=[pl.BlockSpec((seq, dk), col(0)), pl.BlockSpec((seq, dk), col(1)),
                  pl.BlockSpec((seq, dk), col(2)), pl.BlockSpec((seq, dk), col(3)),
                  pl.BlockSpec((nmeta, dk), lambda b, h: (0, h)), pl.BlockSpec((nmeta, dk), lambda b, h: (0, h)),
                  pl.BlockSpec((1, chunk, chunk), tab), pl.BlockSpec((1, chunk, dk), tab),
                  pl.BlockSpec((1, chunk, dk), tab), pl.BlockSpec((1, nmeta, dk), tab),
                  pl.BlockSpec((1, dk, dk), tab),
                  pl.BlockSpec((1, dk), lambda b, h: (0, h))],
        out_specs=[pl.BlockSpec((seq, dk), lambda b, h: (b, h)),
                   pl.BlockSpec((1, 1, 1, dk, dk), lambda b, h: (0, b, h, 0, 0))],
        out_shape=[jax.ShapeDtypeStruct((batch * seq, heads * dk), BF16),
                   jax.ShapeDtypeStruct((1, batch, heads, dk, dk), F32)],
        compiler_params=_cparams(("parallel", "parallel")),
        name="retention_prompt",
    )(rb, rb, rb, rb, rkm, rvm, dec, qd, kd, kdm, gc, gn)


def _ret_s_kernel(q_ref, k_ref, v_ref, g_ref, gam_ref, gn_ref, st_ref, o_ref, ns_ref, *, heads, dk, bs):
    rowid = lax.broadcasted_iota(jnp.int32, (8, dk), 0)
    for hh in range(heads):
        sl = slice(hh * dk, (hh + 1) * dk)
        gam = gam_ref[hh]
        outs = []
        for i in range(bs):
            q = q_ref[i:i + 1, sl]
            k = k_ref[i:i + 1, sl]
            v = v_ref[i:i + 1, sl]
            state = st_ref[0, i, hh]
            k8 = jnp.where(rowid == 0, jnp.broadcast_to(k, (8, dk)), 0.0).astype(BF16)
            v8 = jnp.broadcast_to(v, (8, dk)).astype(BF16)
            ns_ref[0, i, hh] = state * gam[0:1, :] + _dot_tn(k8, v8)
            q8 = jnp.broadcast_to(q, (8, dk)).astype(BF16)
            qs = _dot(q8, state.astype(BF16))[0:1, :]
            qk = jnp.sum(q.astype(BF16).astype(F32) * k.astype(BF16).astype(F32), axis=1, keepdims=True)
            outs.append(qk * v.astype(BF16).astype(F32) + qs * gam[0:1, :])
        o = jnp.concatenate(outs, axis=0)
        o_ref[:, sl] = _head_norm_gate(o, gn_ref[:, sl], g_ref[:, sl]).astype(BF16)


def _retention_sample(rq, rk, rv, rg, gam, gn, state, *, heads, dk, bs):
    db = rq.shape[0]
    n = heads * dk
    row = lambda i: (i, 0)
    st_spec = pl.BlockSpec((1, bs, heads, dk, dk), lambda i: (0, i, 0, 0, 0))
    return pl.pallas_call(
        functools.partial(_ret_s_kernel, heads=heads, dk=dk, bs=bs),
        grid=(db // bs,),
        in_specs=[pl.BlockSpec((bs, n), row), pl.BlockSpec((bs, n), row), pl.BlockSpec((bs, n), row),
                  pl.BlockSpec((bs, n), row), pl.BlockSpec((heads, 8, dk), lambda i: (0, 0, 0)),
                  pl.BlockSpec((1, n), lambda i: (0, 0)), st_spec],
        out_specs=[pl.BlockSpec((bs, n), row), st_spec],
        out_shape=[jax.ShapeDtypeStruct((db, n), BF16), jax.ShapeDtypeStruct(state.shape, F32)],
        compiler_params=_cparams(("parallel",)),
        name="retention_sample",
    )(rq, rk, rv, rg, gam, gn, state)


def _out_kernel(om_ref, or_ref, x_ref, wo1_ref, wo2_ref, g_ref, wg_ref, bg_ref, wr_ref, br_ref,
                x1_ref, h2_ref, rt_ref, *, nsteps, ngroups, nexperts):
    i = pl.program_id(0)

    @pl.when(i < nsteps)
    def _():
        _out_rows(om_ref, or_ref, x_ref, wo1_ref, wo2_ref, g_ref, wg_ref, bg_ref, wr_ref, br_ref,
                  x1_ref, h2_ref, rt_ref, ngroups=ngroups, nexperts=nexperts)

    @pl.when(i >= nsteps)
    def _():
        h2_ref[...] = jnp.zeros_like(h2_ref)


def _out_rows(om_ref, or_ref, x_ref, wo1_ref, wo2_ref, g_ref, wg_ref, bg_ref, wr_ref, br_ref,
              x1_ref, h2_ref, rt_ref, *, ngroups, nexperts):
    x1 = x_ref[...] + _dot(om_ref[...], wo1_ref[...]) + _dot(or_ref[...], wo2_ref[...])
    x1_ref[...] = x1
    hb = _rms(x1, g_ref[...]).astype(BF16)
    _store_chunked(h2_ref, hb.astype(F32), x1.shape[1] // LANES)
    tm = x1.shape[0]
    lane = lax.broadcasted_iota(jnp.int32, (tm, LANES), 1)
    gl = jnp.where(lane < ngroups, _dot(hb, wg_ref[...]) + bg_ref[...], NEG)
    gmax = jnp.max(gl, axis=1, keepdims=True)
    gsum = jnp.sum(jnp.exp(gl - gmax), axis=1, keepdims=True)
    g_w = 1.0 / gsum
    g_idx = jnp.min(jnp.where(gl == gmax, lane, LANES), axis=1, keepdims=True)
    epg = nexperts // ngroups
    in_group = (lane >= g_idx * epg) & (lane < (g_idx + 1) * epg)
    el = jnp.where(in_group, _dot(hb, wr_ref[...]) + br_ref[...], NEG)
    emax = jnp.max(el, axis=1, keepdims=True)
    esum = jnp.sum(jnp.exp(el - emax), axis=1, keepdims=True)
    idx1 = jnp.min(jnp.where(el == emax, lane, LANES), axis=1, keepdims=True)
    el2 = jnp.where(lane == idx1, NEG, el)
    emax2 = jnp.max(el2, axis=1, keepdims=True)
    idx2 = jnp.min(jnp.where(el2 == emax2, lane, LANES), axis=1, keepdims=True)
    p1 = 1.0 / esum
    p2 = jnp.exp(emax2 - emax) / esum
    w1 = p1 / (p1 + p2) * g_w
    w2 = p2 / (p1 + p2) * g_w
    rt_ref[...] = jnp.where(lane == 0, w1, jnp.where(lane == 1, w2, jnp.where(
        lane == 2, idx1.astype(F32), jnp.where(lane == 3, idx2.astype(F32), 0.0))))


def _out_kernel_into(om_ref, or_ref, x_ref, wo1_ref, wo2_ref, g_ref, wg_ref, bg_ref, wr_ref, br_ref,
                     h2_prev_ref, x1_ref, h2_ref, rt_ref, **kw):
    del h2_prev_ref
    _out_kernel(om_ref, or_ref, x_ref, wo1_ref, wo2_ref, g_ref, wg_ref, bg_ref, wr_ref, br_ref,
                x1_ref, h2_ref, rt_ref, **kw)


def _out_proj(om, orr, x, wo1, wo2, g, wg, bg, wr, br, *, rows, tm, h2_rows, h2_row0, h2_prev,
              ngroups, nexperts):
    d = x.shape[1]
    n1, n2 = om.shape[1], orr.shape[1]
    nsteps = rows // tm
    ntail = (h2_rows - rows) // tm if h2_prev is None else 0
    row = lambda i: (jnp.minimum(i, nsteps - 1), 0)
    fix = lambda i: (0, 0)
    in_specs = [pl.BlockSpec((tm, n1), row), pl.BlockSpec((tm, n2), row), pl.BlockSpec((tm, d), row),
                pl.BlockSpec((n1, d), fix), pl.BlockSpec((n2, d), fix), pl.BlockSpec((1, d), fix),
                pl.BlockSpec((d, LANES), fix), pl.BlockSpec((1, LANES), fix),
                pl.BlockSpec((d, LANES), fix), pl.BlockSpec((1, LANES), fix)]
    args = [om, orr, x, wo1, wo2, g, wg, bg, wr, br]
    body, aliases = _out_kernel, {}
    if h2_prev is not None:
        in_specs.append(pl.BlockSpec(memory_space=pl.ANY))
        args.append(h2_prev)
        body, aliases = _out_kernel_into, {len(args) - 1: 1}
    return pl.pallas_call(
        functools.partial(body, nsteps=nsteps, ngroups=ngroups, nexperts=nexperts),
        grid=(nsteps + ntail,),
        in_specs=in_specs,
        out_specs=[pl.BlockSpec((tm, d), row),
                   pl.BlockSpec((tm * (d // LANES), LANES), lambda i: (h2_row0 // tm + i, 0)),
                   pl.BlockSpec((tm, LANES), row)],
        out_shape=[jax.ShapeDtypeStruct((rows, d), F32),
                   jax.ShapeDtypeStruct((h2_rows * (d // LANES), LANES), F32),
                   jax.ShapeDtypeStruct((rows, LANES), F32)],
        input_output_aliases=aliases,
        compiler_params=_cparams(("arbitrary",)),
        name="out_proj_router",
    )(*args)


NWBUF = 3


def _moe_kernel(tt_ref, ms_ref, sp_ref, h2_hbm, wg_hbm, wu_hbm, wd_hbm, ys_hbm,
                xbuf, ybuf, wgb, wub, wdb, gsem, ssem, wsem, *, tm, ntok, nck, pitch, ntiles):
    i = pl.program_id(0)
    nused = ms_ref[0]
    nue = ms_ref[1]
    npair = 2 * ntok

    def hbm_row(ref, t):
        return ref.at[pl.ds(t * nck, nck), :]

    def buf_row(buf, slot, r):
        return buf.at[slot, pl.ds(r * pitch, nck), :]

    def weight_copies(q):
        e = ms_ref[2 + q]
        w = q % NWBUF
        return [pltpu.make_async_copy(src.at[e], dst.at[w], wsem.at[w])
                for src, dst in ((wg_hbm, wgb), (wu_hbm, wub), (wd_hbm, wdb))]

    def gather(tile, slot):
        src0 = tt_ref[0, tile]
        last = src0 + tt_ref[1, tile] - 1
        for r in range(tm):
            pair = sp_ref[jnp.minimum(src0 + r, last)]
            tok = pair - jnp.where(pair >= ntok, ntok, 0)
            pltpu.make_async_copy(hbm_row(h2_hbm, tok), buf_row(xbuf, slot, r), gsem.at[slot]).start(priority=1)

    def scatter(tile, slot):
        src0 = tt_ref[0, tile]
        n = tt_ref[1, tile]
        for r in range(tm):
            pair = sp_ref[jnp.minimum(src0 + r, npair - 1)]
            dst = jnp.where(r < n, pair, npair + slot * tm + r)
            pltpu.make_async_copy(buf_row(ybuf, slot, r), hbm_row(ys_hbm, dst), ssem.at[slot]).start()

    def wait_tile(buf, sem, slot):
        pltpu.make_async_copy(h2_hbm.at[pl.ds(0, tm * nck), :], buf.at[slot, pl.ds(0, tm * nck), :],
                              sem.at[slot]).wait()

    @pl.when(i == 0)
    def _():
        for cp in weight_copies(0):
            cp.start()

        @pl.when(nue > 1)
        def _():
            for cp in weight_copies(1):
                cp.start()

        gather(0, 0)
        ybuf[0] = jnp.zeros(ybuf.shape[1:], F32)
        for s in range(2):
            pltpu.make_async_copy(ybuf.at[0, pl.ds(0, tm * nck), :],
                                  ys_hbm.at[pl.ds((npair + s * tm) * nck, tm * nck), :], ssem.at[0]).start()
        for s in range(2):
            wait_tile(ybuf, ssem, 0)

    @pl.when(i + 1 < nused)
    def _():
        gather(i + 1, (i + 1) % 2)

    @pl.when(i < nused)
    def _():
        slot = i % 2
        wait_tile(xbuf, gsem, slot)

        @pl.when(i >= 2)
        def _():
            wait_tile(ybuf, ssem, slot)

        q = tt_ref[2, i]

        @pl.when((i == 0) | (q != tt_ref[2, jnp.maximum(i - 1, 0)]))
        def _():
            for cp in weight_copies(q):
                cp.wait()

            @pl.when(q + 2 < nue)
            def _():
                for cp in weight_copies(q + 2):
                    cp.start()

        w = q % NWBUF
        x = _load_chunked(xbuf.at[slot], tm, nck, pitch)
        a = jax.nn.silu(_dot(x, wgb[w])) * _dot(x, wub[w])
        _store_chunked(ybuf.at[slot], _dot(a, wdb[w]), pitch)
        scatter(i, slot)

    @pl.when(i == ntiles - 1)
    def _():
        @pl.when(nused >= 2)
        def _():
            wait_tile(ybuf, ssem, nused % 2)
        wait_tile(ybuf, ssem, (nused + 1) % 2)


def _moe(tile_tab, misc, sorted_pairs, h2, w_gate, w_up, w_down, *, tm):
    ntiles = tile_tab.shape[1]
    ntok = sorted_pairs.shape[0] // 2
    d, f = w_gate.shape[1:]
    nck = d // LANES
    pitch = nck + 1
    anyspec = pl.BlockSpec(memory_space=pl.ANY)
    grid_spec = pltpu.PrefetchScalarGridSpec(
        num_scalar_prefetch=3,
        grid=(ntiles,),
        in_specs=[anyspec, anyspec, anyspec, anyspec],
        out_specs=anyspec,
        scratch_shapes=[pltpu.VMEM((2, tm * pitch, LANES), F32), pltpu.VMEM((2, tm * pitch, LANES), F32),
                        pltpu.VMEM((NWBUF, d, f), F32), pltpu.VMEM((NWBUF, d, f), F32),
                        pltpu.VMEM((NWBUF, f, d), F32),
                        pltpu.SemaphoreType.DMA((2,)), pltpu.SemaphoreType.DMA((2,)),
                        pltpu.SemaphoreType.DMA((NWBUF,))],
    )
    return pl.pallas_call(
        functools.partial(_moe_kernel, tm=tm, ntok=ntok, nck=nck, pitch=pitch, ntiles=ntiles),
        grid_spec=grid_spec,
        out_shape=jax.ShapeDtypeStruct(((2 * ntok + 2 * tm) * nck, LANES), F32),
        compiler_params=_cparams(("arbitrary",)),
        name="moe_experts",
    )(tile_tab, misc, sorted_pairs, h2, w_gate, w_up, w_down)


def _comb_kernel(x1_ref, ya_ref, yb_ref, rt_ref, g_ref, y_ref):
    tm, d = x1_ref.shape
    nck = d // LANES
    rt = rt_ref[...]
    x = (x1_ref[...] + rt[:, 0:1] * _load_chunked(ya_ref, tm, nck, nck)
         + rt[:, 1:2] * _load_chunked(yb_ref, tm, nck, nck))
    y_ref[...] = _rms(x, g_ref[...])


def _combine(x1, ys, rt, g, *, tm, row0, ntok, rows):
    d = x1.shape[1]
    nck = d // LANES
    return pl.pallas_call(
        _comb_kernel,
        grid=(rows // tm,),
        in_specs=[pl.BlockSpec((tm, d), lambda i: (i, 0)),
                  pl.BlockSpec((tm * nck, LANES), lambda i: (row0 // tm + i, 0)),
                  pl.BlockSpec((tm * nck, LANES), lambda i: ((ntok + row0) // tm + i, 0)),
                  pl.BlockSpec((tm, LANES), lambda i: (i, 0)), pl.BlockSpec((1, d), lambda i: (0, 0))],
        out_specs=pl.BlockSpec((tm, d), lambda i: (i, 0)),
        out_shape=jax.ShapeDtypeStruct((rows, d), F32),
        compiler_params=_cparams(("parallel",)),
        name="moe_combine_norm",
    )(x1, ys, ys, rt, g)


def _rope_tables(pos, dim):
    half = dim // 2
    inv = jnp.exp(jnp.arange(half, dtype=F32) * (-2.0 * math.log(ROPE_BASE) / dim))
    ang = pos.astype(F32)[:, None] * inv[None, :]
    return jnp.cos(ang), jnp.sin(ang)


def _rot_cols(w):
    half = w.shape[-1] // 2
    return jnp.concatenate([-w[..., half:], w[..., :half]], axis=-1)


def _pick_tile(rows, target):
    t = min(rows, target)
    while rows % t:
        t //= 2
    return t


def _dispatch(route, ntok, nexperts, tm, ntiles):
    i32 = jnp.int32
    ids = jnp.concatenate([route[:, 2], route[:, 3]]).astype(i32)
    npair = 2 * ntok
    _, sorted_pairs = lax.sort((ids, jnp.arange(npair, dtype=i32)), num_keys=1, is_stable=True)
    ex = jnp.arange(nexperts, dtype=i32)
    counts = jnp.sum((ids[:, None] == ex[None, :]).astype(i32), axis=0)
    starts = jnp.cumsum(counts) - counts
    tcount = (counts + tm - 1) // tm
    tend = jnp.cumsum(tcount)
    tstart = tend - tcount
    nused = tend[-1]
    t = jnp.arange(ntiles, dtype=i32)
    tile_expert = jnp.minimum(jnp.sum((tend[None, :] <= t[:, None]).astype(i32), axis=1), nexperts - 1)
    pick = lambda tab: jnp.sum(jnp.where(tile_expert[:, None] == ex[None, :], tab[None, :], 0), axis=1)
    local = t - pick(tstart)
    tile_src0 = jnp.clip(pick(starts) + local * tm, 0, npair - 1)
    tile_rows = jnp.where(t < nused, jnp.clip(pick(counts) - local * tm, 0, tm), 0)
    used = counts > 0
    tile_q = jnp.sum((used[None, :] & (ex[None, :] < tile_expert[:, None])).astype(i32), axis=1)
    used_ids = jnp.sort(jnp.where(used, ex, nexperts))
    tile_tab = jnp.stack([tile_src0, tile_rows, tile_q]).astype(i32)
    misc = jnp.concatenate([nused.reshape(1), jnp.sum(used.astype(i32)).reshape(1), used_ids]).astype(i32)
    return tile_tab, misc, sorted_pairs


def kernel(x_prompt, x_sample, cache_kv_latent, cache_k_rope, state_retention, page_table, meta_tokens,
           ln_mix_g, w_in, q_norm_g, w_uq, w_uk, kv_norm_g, w_uv, mla_out_g, ret_gn_g, w_o, ln_ffn_g,
           w_group, b_group, w_router, b_router, w_gate, w_up, w_down, final_g):
    batch, seq, d = x_prompt.shape
    db, dec_seq, _ = x_sample.shape
    depth = w_in.shape[0]
    assert depth == 1 and dec_seq == 1
    ql = q_norm_g.shape[1]
    kvl = kv_norm_g.shape[1]
    mh, nr = w_uq.shape[2], w_uq.shape[3]
    nope = w_uk.shape[3]
    rope = nr - nope
    vdim = w_uv.shape[3]
    rdim = ret_gn_g.shape[1]
    dk = state_retention.shape[3]
    rh = rdim // dk
    ngroups = w_group.shape[2]
    nexperts = w_router.shape[2]
    page = cache_kv_latent.shape[2]
    npages = page_table.shape[1]
    past_len = npages * page
    chunk = min(2 * ROW_TILE, seq)
    tk = min(4 * ROW_TILE, seq)
    nmain = batch * seq
    naux = -(-(db + N_META) // LANES) * LANES
    ts = -(-db // ROW_TILE) * ROW_TILE
    assert seq % 256 == 0 and seq % tk == 0 and ts <= naux and db % 8 == 0 and rh == mh

    wit = jnp.swapaxes(w_in[0], 0, 1)
    o_kr = ql + kvl
    o_ret = o_kr + rope
    w_a = jnp.concatenate([wit[:o_ret], jnp.swapaxes(_rot_cols(jnp.swapaxes(wit[o_kr:o_ret], 0, 1)), 0, 1)],
                          axis=0).astype(BF16)
    w_b = wit[o_ret:].astype(BF16)
    wq = w_uq[0]
    wq_rope = wq[:, :, nope:]
    wuq = jnp.concatenate([wq[:, :, :nope].reshape(ql, mh * nope), wq_rope.reshape(ql, mh * rope),
                           _rot_cols(wq_rope).reshape(ql, mh * rope)], axis=1).astype(BF16)
    wukt = jnp.transpose(w_uk[0], (1, 2, 0)).astype(BF16)
    wuv = jnp.transpose(w_uv[0], (1, 0, 2)).astype(BF16)
    wo = w_o[0].astype(BF16)
    wo1, wo2 = wo[:mh * vdim], wo[mh * vdim:]
    wg_pad = jnp.zeros((d, LANES), F32).at[:, :ngroups].set(w_group[0]).astype(BF16)
    bg_pad = jnp.zeros((1, LANES), F32).at[0, :ngroups].set(b_group[0])
    wr_pad = jnp.zeros((d, LANES), F32).at[:, :nexperts].set(w_router[0]).astype(BF16)
    br_pad = jnp.zeros((1, LANES), F32).at[0, :nexperts].set(b_router[0])

    pos_main = N_META + jnp.arange(seq, dtype=jnp.int32)
    pos_aux = jnp.concatenate([jnp.full((db,), past_len, jnp.int32), jnp.arange(N_META, dtype=jnp.int32),
                               jnp.zeros((naux - db - N_META,), jnp.int32)])

    def tables(pos):
        c64, s64 = _rope_tables(pos, rope)
        c128, s128 = _rope_tables(pos, dk)
        cs = jnp.concatenate([c64, c64], axis=1)
        sn = jnp.concatenate([s64, s64], axis=1)
        return (cs, sn, jnp.tile(cs, (1, 2)), jnp.tile(sn, (1, 2)),
                jnp.concatenate([c128, c128], axis=1), jnp.concatenate([-s128, s128], axis=1))

    log_g = jnp.log1p(-jnp.exp2(-5.0 - jnp.arange(rh, dtype=F32)))
    n = jnp.arange(chunk, dtype=F32)
    diff = n[:, None] - n[None, :]
    t_dec = jnp.where(diff >= 0, jnp.exp(log_g[:, None, None] * jnp.maximum(diff, 0.0)), 0.0)
    t_qd = jnp.broadcast_to(jnp.exp((n[None, :] + 1.0) * log_g[:, None])[:, :, None], (rh, chunk, dk))
    k_decay = jnp.exp((chunk - 1.0 - n)[None, :] * log_g[:, None])
    t_kd = jnp.broadcast_to(k_decay[:, :, None], (rh, chunk, dk))
    t_kdm = jnp.broadcast_to(k_decay[:, chunk - N_META:, None], (rh, N_META, dk))
    t_gc = jnp.broadcast_to(jnp.exp(chunk * log_g)[:, None, None], (rh, dk, dk))
    t_gam = jnp.broadcast_to(jnp.exp(log_g)[:, None, None], (rh, 8, dk))

    x_main = x_prompt.reshape(nmain, d)
    x_aux = jnp.concatenate([x_sample.reshape(db, d), meta_tokens.astype(F32),
                             jnp.zeros((naux - db - N_META, d), F32)], axis=0)
    g_mix = ln_mix_g[0][None]
    scale = float(nope + rope) ** -0.5
    tm_main = _pick_tile(seq, 512)
    tm_aux = _pick_tile(naux, 512)

    def token_front(x, pos, tm):
        cs, sn, cs8, sn8, cos_r, sin_r = tables(pos)
        cq, kv, kvb, kr, krb = _in_a(x, g_mix, w_a, q_norm_g, kv_norm_g, cs, sn,
                                     tm=tm, ql=ql, kvl=kvl, rope=rope)
        rb = _in_b(x, g_mix, w_b, cos_r, sin_r, tm=tm, heads=rh, dk=dk)
        q_lat, q_rope = _q_proj(cq, wuq, wukt, cs8, sn8, tm=tm, heads=mh, nope=nope, rope=rope, scale=scale)
        return kv, kvb, kr, krb, rb, q_lat, q_rope

    kv_m, kvb_m, kr_m, krb_m, rb_m, ql_m, qr_m = token_front(x_main, pos_main, tm_main)
    kv_a, kvb_a, kr_a, krb_a, rb_a, ql_a, qr_a = token_front(x_aux, pos_aux, tm_aux)
    meta = slice(db, db + N_META)

    om_m = _attention(ql_m, qr_m, kvb_m, krb_m, kvb_a[meta], krb_a[meta], wuv, mla_out_g,
                      batch=batch, seq=seq, tq=ROW_TILE, tk=tk)
    rdk = rh * dk
    or_m, st_p = _retention_prompt(rb_m, rb_a[meta, rdk:2 * rdk], rb_a[meta, 2 * rdk:3 * rdk],
                                   (t_dec, t_qd, t_kd, t_kdm, t_gc), ret_gn_g,
                                   batch=batch, seq=seq, heads=rh, dk=dk, chunk=chunk)

    ql_s = jnp.transpose(ql_a[:, :db], (1, 0, 2))
    qr_s = jnp.transpose(qr_a[:, :db], (1, 0, 2))
    o_s = _decode_attention(page_table, ql_s, qr_s, kv_a[:db, None, :], kr_a[:db, None, :],
                            cache_kv_latent[0], jnp.swapaxes(cache_k_rope[0], 1, 2),
                            gpages=min(16, npages), nbuf=3)
    om_s = _decode_out(jnp.transpose(o_s, (1, 0, 2)), wuv, mla_out_g)
    rs = rb_a[:db].astype(F32)
    or_s, st_s = _retention_sample(rs[:, :rdk], rs[:, rdk:2 * rdk], rs[:, 2 * rdk:3 * rdk], rs[:, 3 * rdk:],
                                   t_gam, ret_gn_g, state_retention, heads=rh, dk=dk, bs=8)
    pad_rows = lambda a: jnp.concatenate([a, jnp.zeros((ts - db, a.shape[1]), a.dtype)], axis=0)
    om_a, or_a = pad_rows(om_s), pad_rows(or_s)

    g_ffn = ln_ffn_g[0][None]
    ntok = nmain + ts
    tm_o = _pick_tile(nmain, 256)
    outp = functools.partial(_out_proj, h2_rows=nmain + -(-ts // tm_o) * tm_o, ngroups=ngroups, nexperts=nexperts)
    x1_m, h2, rt_m = outp(om_m, or_m, x_main, wo1, wo2, g_ffn, wg_pad, bg_pad, wr_pad, br_pad,
                          rows=nmain, tm=tm_o, h2_row0=0, h2_prev=None)
    x1_a, h2, rt_a = outp(om_a, or_a, x_aux, wo1, wo2, g_ffn, wg_pad, bg_pad, wr_pad, br_pad,
                          rows=ts, tm=ROW_TILE, h2_row0=nmain, h2_prev=h2)

    route = jnp.concatenate([rt_m[:, :4], rt_a[:, :4]], axis=0)
    tm_e = ROW_TILE
    ntiles = -(-(2 * ntok + nexperts * (tm_e - 1)) // tm_e)
    ys = _moe(*_dispatch(route, ntok, nexperts, tm_e, ntiles), h2, w_gate[0], w_up[0], w_down[0], tm=tm_e)
    fg = final_g[None]
    y_m = _combine(x1_m, ys, rt_m, fg, tm=ROW_TILE, row0=0, ntok=ntok, rows=nmain)
    y_a = _combine(x1_a, ys, rt_a, fg, tm=ROW_TILE, row0=nmain, ntok=ntok, rows=ts)

    def with_meta(a_main, a_aux):
        w = a_main.shape[1]
        m = jnp.broadcast_to(a_aux[meta][None], (batch, N_META, w))
        return jnp.concatenate([m, a_main.reshape(batch, seq, w)], axis=1)[None]

    return (y_m.reshape(batch, seq, d), y_a[:db].reshape(db, 1, d),
            with_meta(kv_m, kv_a), with_meta(kr_m, kr_a), st_p,
            kv_a[:db].reshape(1, db, 1, kvl), kr_a[:db].reshape(1, db, 1, rope), st_s)
```

```python
import functools
import math

import jax
import jax.numpy as jnp
from jax import lax
from jax.experimental import pallas as pl
from jax.experimental.pallas import tpu as pltpu

F32 = jnp.float32
BF16 = jnp.bfloat16

N_META = 16
ROPE_BASE = 10000.0
EPS = 1e-6
NEG = -1e30
EXPERTS_PER_GROUP = 8
LANES = 128
ROW_TILE = 128
VMEM_LIMIT = 56 * 1024 * 1024


def _cparams(sem, vmem=VMEM_LIMIT):
    return pltpu.CompilerParams(dimension_semantics=sem, vmem_limit_bytes=vmem)


def _rms(x, g):
    return x * lax.rsqrt(jnp.mean(x * x, axis=-1, keepdims=True) + EPS) * g


def _dot(a, b):
    return jnp.dot(a, b, preferred_element_type=F32)


def _dot_nt(a, b):
    return lax.dot_general(a, b, (((1,), (1,)), ((), ())), preferred_element_type=F32)


def _dot_tn(a, b):
    return lax.dot_general(a, b, (((0,), (0,)), ((), ())), preferred_element_type=F32)


def _store_chunked(ref, val, pitch, base=0):
    rows, d = val.shape
    for c in range(d // LANES):
        ref[pl.ds(base + c, rows, stride=pitch), :] = val[:, c * LANES:(c + 1) * LANES]


def _load_chunked(ref, rows, nchunk, pitch, base=0):
    return jnp.concatenate([ref[pl.ds(base + c, rows, stride=pitch), :] for c in range(nchunk)], axis=1)


def _in_a_kernel(x_ref, g_ref, w_ref, qg_ref, kvg_ref, cs_ref, sn_ref,
                 cq_ref, kv_ref, kvb_ref, kr_ref, krb_ref, *, ql, kvl, rope):
    h = _rms(x_ref[...], g_ref[...]).astype(BF16)
    c = _dot_nt(h, w_ref[...])
    cq_ref[...] = _rms(c[:, :ql], qg_ref[...]).astype(BF16)
    kv = _rms(c[:, ql:ql + kvl], kvg_ref[...])
    kv_ref[...] = kv
    kvb_ref[...] = kv.astype(BF16)
    a = c[:, ql + kvl:ql + kvl + rope]
    b = c[:, ql + kvl + rope:]
    kr = a * cs_ref[...] + b * sn_ref[...]
    kr_ref[...] = kr
    krb_ref[...] = kr.astype(BF16)


def _in_a(x, g, w_a, qg, kvg, cs, sn, *, tm, ql, kvl, rope):
    rows, d = x.shape
    n = w_a.shape[0]
    row = lambda i: (i, 0)
    fix = lambda i: (0, 0)
    tab = lambda i: (i % (cs.shape[0] // tm), 0)
    return pl.pallas_call(
        functools.partial(_in_a_kernel, ql=ql, kvl=kvl, rope=rope),
        grid=(rows // tm,),
        in_specs=[pl.BlockSpec((tm, d), row), pl.BlockSpec((1, d), fix), pl.BlockSpec((n, d), fix),
                  pl.BlockSpec((1, ql), fix), pl.BlockSpec((1, kvl), fix),
                  pl.BlockSpec((tm, rope), tab), pl.BlockSpec((tm, rope), tab)],
        out_specs=[pl.BlockSpec((tm, ql), row), pl.BlockSpec((tm, kvl), row), pl.BlockSpec((tm, kvl), row),
                   pl.BlockSpec((tm, rope), row), pl.BlockSpec((tm, rope), row)],
        out_shape=[jax.ShapeDtypeStruct((rows, ql), BF16), jax.ShapeDtypeStruct((rows, kvl), F32),
                   jax.ShapeDtypeStruct((rows, kvl), BF16), jax.ShapeDtypeStruct((rows, rope), F32),
                   jax.ShapeDtypeStruct((rows, rope), BF16)],
        compiler_params=_cparams(("parallel",)),
        name="in_proj_a",
    )(x, g, w_a, qg, kvg, cs, sn)


def _in_b_kernel(x_ref, g_ref, w_ref, cos_ref, sin_ref, out_ref, h_scr, *, heads, dk):
    j = pl.program_id(1)

    @pl.when(j == 0)
    def _():
        h_scr[...] = _rms(x_ref[...], g_ref[...]).astype(BF16)

    c = _dot_nt(h_scr[...], w_ref[...])

    @pl.when(j < 2)
    def _():
        scale = jnp.where(j == 1, dk ** -0.5, 1.0).astype(F32)
        cos = cos_ref[...]
        sin = sin_ref[...]
        for hh in range(heads):
            blk = c[:, hh * dk:(hh + 1) * dk]
            r = blk * cos + pltpu.roll(blk, dk // 2, 1) * sin
            out_ref[:, hh * dk:(hh + 1) * dk] = (r * scale).astype(BF16)

    @pl.when(j >= 2)
    def _():
        out_ref[...] = c.astype(BF16)


def _in_b(x, g, w_b, cos, sin, *, tm, heads, dk):
    rows, d = x.shape
    n = heads * dk
    tab = lambda i, j: (i % (cos.shape[0] // tm), 0)
    return pl.pallas_call(
        functools.partial(_in_b_kernel, heads=heads, dk=dk),
        grid=(rows // tm, 4),
        in_specs=[pl.BlockSpec((tm, d), lambda i, j: (i, 0)), pl.BlockSpec((1, d), lambda i, j: (0, 0)),
                  pl.BlockSpec((n, d), lambda i, j: (j, 0)),
                  pl.BlockSpec((tm, dk), tab), pl.BlockSpec((tm, dk), tab)],
        out_specs=pl.BlockSpec((tm, n), lambda i, j: (i, j)),
        out_shape=jax.ShapeDtypeStruct((rows, 4 * n), BF16),
        scratch_shapes=[pltpu.VMEM((tm, d), BF16)],
        compiler_params=_cparams(("parallel", "arbitrary")),
        name="in_proj_b",
    )(x, g, w_b, cos, sin)


def _q_kernel(cq_ref, wuq_ref, wukt_ref, cos_ref, sin_ref, ql_ref, qr_ref, *, heads, nope, rope, scale):
    qf = _dot(cq_ref[...], wuq_ref[...])
    o1 = heads * nope
    o2 = o1 + heads * rope
    cos = cos_ref[...]
    sin = sin_ref[...]
    for hp in range(heads // 2):
        lo, hi = hp * 2 * rope, (hp + 1) * 2 * rope
        qr = (qf[:, o1 + lo:o1 + hi] * cos + qf[:, o2 + lo:o2 + hi] * sin) * scale
        qr_ref[2 * hp] = qr[:, :rope].astype(BF16)
        qr_ref[2 * hp + 1] = qr[:, rope:].astype(BF16)
    for hh in range(heads):
        qn = qf[:, hh * nope:(hh + 1) * nope].astype(BF16)
        ql_ref[hh] = (_dot(qn, wukt_ref[hh]) * scale).astype(BF16)


def _q_proj(cq, wuq, wukt, cos, sin, *, tm, heads, nope, rope, scale):
    rows, ql = cq.shape
    kvl = wukt.shape[2]
    nq = wuq.shape[1]
    tab = lambda i: (i % (cos.shape[0] // tm), 0)
    return pl.pallas_call(
        functools.partial(_q_kernel, heads=heads, nope=nope, rope=rope, scale=scale),
        grid=(rows // tm,),
        in_specs=[pl.BlockSpec((tm, ql), lambda i: (i, 0)), pl.BlockSpec((ql, nq), lambda i: (0, 0)),
                  pl.BlockSpec((heads, nope, kvl), lambda i: (0, 0, 0)),
                  pl.BlockSpec((tm, 2 * rope), tab), pl.BlockSpec((tm, 2 * rope), tab)],
        out_specs=[pl.BlockSpec((heads, tm, kvl), lambda i: (0, i, 0)),
                   pl.BlockSpec((heads, tm, rope), lambda i: (0, i, 0))],
        out_shape=[jax.ShapeDtypeStruct((heads, rows, kvl), BF16),
                   jax.ShapeDtypeStruct((heads, rows, rope), BF16)],
        compiler_params=_cparams(("parallel",)),
        name="q_proj",
    )(cq, wuq, wukt, cos, sin)


def _uv_norm(o_heads, wuv_ref, g):
    parts = [_dot(o.astype(BF16), wuv_ref[hh]) for hh, o in enumerate(o_heads)]
    return _rms(jnp.concatenate(parts, axis=1), g).astype(BF16)


def _widen(x, n):
    if n <= LANES:
        return x[:, :n]
    return jnp.concatenate([x] * (n // LANES), axis=1)


def _att_kernel(qi_ref, kj_ref, ql_ref, qr_ref, kv_ref, kr_ref, kvm_ref, krm_ref, wuv_ref, g_ref, o_ref,
                m_scr, l_scr, acc_scr, *, heads, tq, tk):
    qi = qi_ref[pl.program_id(1)]
    kj = kj_ref[pl.program_id(1)]
    kvl = ql_ref.shape[2]
    rows = heads * tq
    ql = ql_ref[...].reshape(rows, kvl)
    qr = qr_ref[...].reshape(rows, qr_ref.shape[2])

    def update(s, kv, first):
        m_cur = jnp.max(s, axis=1, keepdims=True)
        if first:
            m_new = jnp.broadcast_to(m_cur, (rows, LANES))
        else:
            m_prev = m_scr[...]
            m_new = jnp.maximum(m_prev, m_cur)
        p = jnp.exp(s - _widen(m_new, s.shape[1]))
        p_sum = jnp.sum(p, axis=1, keepdims=True)
        pv = _dot(p.astype(BF16), kv)
        if first:
            l_scr[...] = jnp.broadcast_to(p_sum, (rows, LANES))
            acc_scr[...] = pv
        else:
            alpha = jnp.exp(m_prev - m_new)
            l_scr[...] = alpha * l_scr[...] + p_sum
            acc_scr[...] = acc_scr[...] * _widen(alpha, kvl) + pv
        m_scr[...] = m_new

    @pl.when(kj == 0)
    def _():
        kvm = kvm_ref[...]
        update(_dot_nt(ql, kvm) + _dot_nt(qr, krm_ref[...]), kvm, True)

    last = (qi * tq + tq - 1) // tk

    @pl.when(kj < last)
    def _():
        kv = kv_ref[...]
        update(_dot_nt(ql, kv) + _dot_nt(qr, kr_ref[...]), kv, False)

    @pl.when(kj == last)
    def _():
        kv = kv_ref[...]
        s = _dot_nt(ql, kv) + _dot_nt(qr, kr_ref[...])
        qpos = qi * tq + lax.broadcasted_iota(jnp.int32, (rows, tk), 0) % tq
        kpos = kj * tk + lax.broadcasted_iota(jnp.int32, (rows, tk), 1)
        update(jnp.where(kpos <= qpos, s, NEG), kv, False)
        o = acc_scr[...] / _widen(l_scr[...], kvl)
        o_ref[...] = _uv_norm([o[hh * tq:(hh + 1) * tq] for hh in range(heads)], wuv_ref, g_ref[...])


def _attention(ql, qr, kvb, krb, kvm, krm, wuv, g, *, batch, seq, tq, tk):
    heads, rows, kvl = ql.shape
    rope = qr.shape[2]
    vdim = wuv.shape[2]
    nq, nk = seq // tq, seq // tk
    nmeta = kvm.shape[0]
    pairs = [(qi, kj) for qi in range(nq) for kj in range((qi * tq + tq - 1) // tk + 1)]
    qi_tab = jnp.asarray([p[0] for p in pairs], jnp.int32)
    kj_tab = jnp.asarray([p[1] for p in pairs], jnp.int32)
    qidx = lambda b, s, qt, kt: (0, b * nq + qt[s], 0)
    kidx = lambda b, s, qt, kt: (b * nk + kt[s], 0)
    fix2 = lambda b, s, qt, kt: (0, 0)
    grid_spec = pltpu.PrefetchScalarGridSpec(
        num_scalar_prefetch=2,
        grid=(batch, len(pairs)),
        in_specs=[pl.BlockSpec((heads, tq, kvl), qidx), pl.BlockSpec((heads, tq, rope), qidx),
                  pl.BlockSpec((tk, kvl), kidx), pl.BlockSpec((tk, rope), kidx),
                  pl.BlockSpec((nmeta, kvl), fix2), pl.BlockSpec((nmeta, rope), fix2),
                  pl.BlockSpec((heads, kvl, vdim), lambda b, s, qt, kt: (0, 0, 0)),
                  pl.BlockSpec((1, heads * vdim), fix2)],
        out_specs=pl.BlockSpec((tq, heads * vdim), lambda b, s, qt, kt: (b * nq + qt[s], 0)),
        scratch_shapes=[pltpu.VMEM((heads * tq, LANES), F32), pltpu.VMEM((heads * tq, LANES), F32),
                        pltpu.VMEM((heads * tq, kvl), F32)],
    )
    return pl.pallas_call(
        functools.partial(_att_kernel, heads=heads, tq=tq, tk=tk),
        grid_spec=grid_spec,
        out_shape=jax.ShapeDtypeStruct((rows, heads * vdim), BF16),
        compiler_params=_cparams(("parallel", "arbitrary")),
        name="mla_prompt_attention",
    )(qi_tab, kj_tab, ql, qr, kvb, krb, kvm, krm, wuv, g)


def _dec_kernel(pt_ref, ql_ref, qr_ref, kvn_ref, krn_ref, ckv_hbm, ckr_hbm, o_ref,
                kvbuf, krbuf, sem, *, nchunk, gpages, nbuf, total):
    b = pl.program_id(0)
    page = kvbuf.shape[2]
    kvl = kvbuf.shape[3]

    def copies(g, slot):
        bb = g // nchunk
        c0 = (g % nchunk) * gpages
        out = []
        for p in range(gpages):
            pg = pt_ref[bb, c0 + p]
            out.append(pltpu.make_async_copy(ckv_hbm.at[pg], kvbuf.at[slot, p], sem.at[slot]))
            out.append(pltpu.make_async_copy(ckr_hbm.at[pg], krbuf.at[slot, p], sem.at[slot]))
        return out

    def start(g, slot):
        for cp in copies(g, slot):
            cp.start()

    @pl.when(b == 0)
    def _():
        for g0 in range(min(nbuf - 1, total)):
            start(g0, g0 % nbuf)

    ql = ql_ref[0].astype(F32)
    qr = qr_ref[0].astype(F32)
    kvn = kvn_ref[0]
    krn = krn_ref[0]
    heads = ql.shape[0]
    m0 = jnp.sum(ql * kvn, axis=1, keepdims=True) + jnp.sum(qr * krn, axis=1, keepdims=True)
    l0 = jnp.ones((heads, 1), F32)
    acc0 = jnp.broadcast_to(kvn, (heads, kvl))

    def body(c, carry):
        m, l, acc = carry
        g = b * nchunk + c
        slot = g % nbuf
        for cp in copies(g, slot):
            cp.wait()
        nxt = g + nbuf - 1

        @pl.when(nxt < total)
        def _():
            start(nxt, nxt % nbuf)

        kvc = kvbuf[slot].reshape(gpages * page, kvl)
        krt = jnp.concatenate([krbuf[slot, p] for p in range(gpages)], axis=1)
        s = _dot_nt(ql, kvc) + _dot(qr, krt)
        m_new = jnp.maximum(m, jnp.max(s, axis=1, keepdims=True))
        alpha = jnp.exp(m - m_new)
        p = jnp.exp(s - m_new)
        l = l * alpha + jnp.sum(p, axis=1, keepdims=True)
        acc = acc * alpha + _dot(p, kvc)
        return m_new, l, acc

    m, l, acc = lax.fori_loop(0, nchunk, body, (m0, l0, acc0))
    o_ref[0] = acc / l


def _decode_attention(page_table, ql_s, qr_s, kvn, krn, cache_kv, cache_krt, *, gpages, nbuf):
    db, npages = page_table.shape
    heads, kvl = ql_s.shape[1:]
    rope = qr_s.shape[2]
    page = cache_kv.shape[1]
    nchunk = npages // gpages
    total = db * nchunk
    grid_spec = pltpu.PrefetchScalarGridSpec(
        num_scalar_prefetch=1,
        grid=(db,),
        in_specs=[pl.BlockSpec((1, heads, kvl), lambda b, pt: (b, 0, 0)),
                  pl.BlockSpec((1, heads, rope), lambda b, pt: (b, 0, 0)),
                  pl.BlockSpec((1, 1, kvl), lambda b, pt: (b, 0, 0)),
                  pl.BlockSpec((1, 1, rope), lambda b, pt: (b, 0, 0)),
                  pl.BlockSpec(memory_space=pl.ANY), pl.BlockSpec(memory_space=pl.ANY)],
        out_specs=pl.BlockSpec((1, heads, kvl), lambda b, pt: (b, 0, 0)),
        scratch_shapes=[pltpu.VMEM((nbuf, gpages, page, kvl), F32),
                        pltpu.VMEM((nbuf, gpages, rope, page), F32),
                        pltpu.SemaphoreType.DMA((nbuf,))],
    )
    return pl.pallas_call(
        functools.partial(_dec_kernel, nchunk=nchunk, gpages=gpages, nbuf=nbuf, total=total),
        grid_spec=grid_spec,
        out_shape=jax.ShapeDtypeStruct((db, heads, kvl), F32),
        compiler_params=_cparams(("arbitrary",)),
        name="mla_decode_attention",
    )(page_table, ql_s, qr_s, kvn, krn, cache_kv, cache_krt)


def _dec_out_kernel(o_ref, wuv_ref, g_ref, out_ref):
    heads = o_ref.shape[0]
    out_ref[...] = _uv_norm([o_ref[hh] for hh in range(heads)], wuv_ref, g_ref[...])


def _decode_out(o_hm, wuv, g):
    heads, rows, kvl = o_hm.shape
    vdim = wuv.shape[2]
    return pl.pallas_call(
        _dec_out_kernel,
        out_shape=jax.ShapeDtypeStruct((rows, heads * vdim), BF16),
        name="mla_decode_out",
    )(o_hm, wuv, g)


def _head_norm_gate(o, gn, rg):
    mu = jnp.mean(o, axis=-1, keepdims=True)
    var = jnp.mean(jnp.square(o - mu), axis=-1, keepdims=True)
    y = (o - mu) * lax.rsqrt(var + EPS) * gn
    return y * jax.nn.silu(rg)


def _ret_kernel(rq_ref, rk_ref, rv_ref, rg_ref, rkm_ref, rvm_ref, dec_ref, qd_ref, kd_ref, kdm_ref, gc_ref,
                gn_ref, o_ref, st_ref, *, chunk, nchunks):
    dec = dec_ref[0]
    qd = qd_ref[0]
    kd = kd_ref[0]
    gc = gc_ref[0]
    gn = gn_ref[...]
    s0 = _dot_tn((rkm_ref[...].astype(F32) * kdm_ref[0]).astype(BF16), rvm_ref[...])

    def body(c, state):
        r0 = pl.multiple_of(c * chunk, chunk)
        q = rq_ref[pl.ds(r0, chunk), :]
        k = rk_ref[pl.ds(r0, chunk), :]
        v = rv_ref[pl.ds(r0, chunk), :]
        rg = rg_ref[pl.ds(r0, chunk), :].astype(F32)
        scores = _dot_nt(q, k) * dec
        o = _dot(scores.astype(BF16), v) + _dot(q, state.astype(BF16)) * qd
        new_state = state * gc + _dot_tn((k.astype(F32) * kd).astype(BF16), v)
        o_ref[pl.ds(r0, chunk), :] = _head_norm_gate(o, gn, rg).astype(BF16)
        return new_state

    st_ref[0, 0, 0] = lax.fori_loop(0, nchunks, body, s0, unroll=2 if nchunks % 2 == 0 else 1)


def _retention_prompt(rb, rkm, rvm, tabs, gn, *, batch, seq, heads, dk, chunk):
    dec, qd, kd, kdm, gc = tabs
    nmeta = rkm.shape[0]
    col = lambda off: (lambda b, h: (b, off * heads + h))
    tab = lambda b, h: (h, 0, 0)
    return pl.pallas_call(
        functools.partial(_ret_kernel, chunk=chunk, nchunks=seq // chunk),
        grid=(batch, heads),
        in_specs=[pl.BlockSpec((seq, dk), col(0)), pl.BlockSpec((seq, dk), col(1)),
                  pl.BlockSpec((seq, dk), col(2)), pl.BlockSpec((seq, dk), col(3)),
                  pl.BlockSpec((nmeta, dk), lambda b, h: (0, h)), pl.BlockSpec((nmeta, dk), lambda b, h: (0, h)),
                  pl.BlockSpec((1, chunk, chunk), tab), pl.BlockSpec((1, chunk, dk), tab),
                  pl.BlockSpec((1, chunk, dk), tab), pl.BlockSpec((1, nmeta, dk), tab),
                  pl.BlockSpec((1, dk, dk), tab),
                  pl.BlockSpec((1, dk), lambda b, h: (0, h))],
        out_specs=[pl.BlockSpec((seq, dk), lambda b, h: (b, h)),
                   pl.BlockSpec((1, 1, 1, dk, dk), lambda b, h: (0, b, h, 0, 0))],
        out_shape=[jax.ShapeDtypeStruct((batch * seq, heads * dk), BF16),
                   jax.ShapeDtypeStruct((1, batch, heads, dk, dk), F32)],
        compiler_params=_cparams(("parallel", "parallel")),
        name="retention_prompt",
    )(rb, rb, rb, rb, rkm, rvm, dec, qd, kd, kdm, gc, gn)


def _ret_s_kernel(q_ref, k_ref, v_ref, g_ref, gam_ref, gn_ref, st_ref, o_ref, ns_ref, *, heads, dk, bs):
    rowid = lax.broadcasted_iota(jnp.int32, (8, dk), 0)
    for hh in range(heads):
        sl = slice(hh * dk, (hh + 1) * dk)
        gam = gam_ref[hh]
        outs = []
        for i in range(bs):
            q = q_ref[i:i + 1, sl]
            k = k_ref[i:i + 1, sl]
            v = v_ref[i:i + 1, sl]
            state = st_ref[0, i, hh]
            k8 = jnp.where(rowid == 0, jnp.broadcast_to(k, (8, dk)), 0.0).astype(BF16)
            v8 = jnp.broadcast_to(v, (8, dk)).astype(BF16)
            ns_ref[0, i, hh] = state * gam[0:1, :] + _dot_tn(k8, v8)
            q8 = jnp.broadcast_to(q, (8, dk)).astype(BF16)
            qs = _dot(q8, state.astype(BF16))[0:1, :]
            qk = jnp.sum(q.astype(BF16).astype(F32) * k.astype(BF16).astype(F32), axis=1, keepdims=True)
            outs.append(qk * v.astype(BF16).astype(F32) + qs * gam[0:1, :])
        o = jnp.concatenate(outs, axis=0)
        o_ref[:, sl] = _head_norm_gate(o, gn_ref[:, sl], g_ref[:, sl]).astype(BF16)


def _retention_sample(rq, rk, rv, rg, gam, gn, state, *, heads, dk, bs):
    db = rq.shape[0]
    n = heads * dk
    row = lambda i: (i, 0)
    st_spec = pl.BlockSpec((1, bs, heads, dk, dk), lambda i: (0, i, 0, 0, 0))
    return pl.pallas_call(
        functools.partial(_ret_s_kernel, heads=heads, dk=dk, bs=bs),
        grid=(db // bs,),
        in_specs=[pl.BlockSpec((bs, n), row), pl.BlockSpec((bs, n), row), pl.BlockSpec((bs, n), row),
                  pl.BlockSpec((bs, n), row), pl.BlockSpec((heads, 8, dk), lambda i: (0, 0, 0)),
                  pl.BlockSpec((1, n), lambda i: (0, 0)), st_spec],
        out_specs=[pl.BlockSpec((bs, n), row), st_spec],
        out_shape=[jax.ShapeDtypeStruct((db, n), BF16), jax.ShapeDtypeStruct(state.shape, F32)],
        compiler_params=_cparams(("parallel",)),
        name="retention_sample",
    )(rq, rk, rv, rg, gam, gn, state)


def _out_kernel(om_ref, or_ref, x_ref, wo1_ref, wo2_ref, g_ref, wg_ref, bg_ref, wr_ref, br_ref,
                x1_ref, h2_ref, rt_ref, *, nsteps, ngroups, nexperts):
    i = pl.program_id(0)

    @pl.when(i < nsteps)
    def _():
        _out_rows(om_ref, or_ref, x_ref, wo1_ref, wo2_ref, g_ref, wg_ref, bg_ref, wr_ref, br_ref,
                  x1_ref, h2_ref, rt_ref, ngroups=ngroups, nexperts=nexperts)

    @pl.when(i >= nsteps)
    def _():
        h2_ref[...] = jnp.zeros_like(h2_ref)


def _out_rows(om_ref, or_ref, x_ref, wo1_ref, wo2_ref, g_ref, wg_ref, bg_ref, wr_ref, br_ref,
              x1_ref, h2_ref, rt_ref, *, ngroups, nexperts):
    x1 = x_ref[...] + _dot(om_ref[...], wo1_ref[...]) + _dot(or_ref[...], wo2_ref[...])
    x1_ref[...] = x1
    hb = _rms(x1, g_ref[...]).astype(BF16)
    _store_chunked(h2_ref, hb.astype(F32), x1.shape[1] // LANES)
    tm = x1.shape[0]
    lane = lax.broadcasted_iota(jnp.int32, (tm, LANES), 1)
    gl = jnp.where(lane < ngroups, _dot(hb, wg_ref[...]) + bg_ref[...], NEG)
    gmax = jnp.max(gl, axis=1, keepdims=True)
    gsum = jnp.sum(jnp.exp(gl - gmax), axis=1, keepdims=True)
    g_w = 1.0 / gsum
    g_idx = jnp.min(jnp.where(gl == gmax, lane, LANES), axis=1, keepdims=True)
    epg = nexperts // ngroups
    in_group = (lane >= g_idx * epg) & (lane < (g_idx + 1) * epg)
    el = jnp.where(in_group, _dot(hb, wr_ref[...]) + br_ref[...], NEG)
    emax = jnp.max(el, axis=1, keepdims=True)
    esum = jnp.sum(jnp.exp(el - emax), axis=1, keepdims=True)
    idx1 = jnp.min(jnp.where(el == emax, lane, LANES), axis=1, keepdims=True)
    el2 = jnp.where(lane == idx1, NEG, el)
    emax2 = jnp.max(el2, axis=1, keepdims=True)
    idx2 = jnp.min(jnp.where(el2 == emax2, lane, LANES), axis=1, keepdims=True)
    p1 = 1.0 / esum
    p2 = jnp.exp(emax2 - emax) / esum
    w1 = p1 / (p1 + p2) * g_w
    w2 = p2 / (p1 + p2) * g_w
    rt_ref[...] = jnp.where(lane == 0, w1, jnp.where(lane == 1, w2, jnp.where(
        lane == 2, idx1.astype(F32), jnp.where(lane == 3, idx2.astype(F32), 0.0))))


def _out_kernel_into(om_ref, or_ref, x_ref, wo1_ref, wo2_ref, g_ref, wg_ref, bg_ref, wr_ref, br_ref,
                     h2_prev_ref, x1_ref, h2_ref, rt_ref, **kw):
    del h2_prev_ref
    _out_kernel(om_ref, or_ref, x_ref, wo1_ref, wo2_ref, g_ref, wg_ref, bg_ref, wr_ref, br_ref,
                x1_ref, h2_ref, rt_ref, **kw)


def _out_proj(om, orr, x, wo1, wo2, g, wg, bg, wr, br, *, rows, tm, h2_rows, h2_row0, h2_prev,
              ngroups, nexperts):
    d = x.shape[1]
    n1, n2 = om.shape[1], orr.shape[1]
    nsteps = rows // tm
    ntail = (h2_rows - rows) // tm if h2_prev is None else 0
    row = lambda i: (jnp.minimum(i, nsteps - 1), 0)
    fix = lambda i: (0, 0)
    in_specs = [pl.BlockSpec((tm, n1), row), pl.BlockSpec((tm, n2), row), pl.BlockSpec((tm, d), row),
                pl.BlockSpec((n1, d), fix), pl.BlockSpec((n2, d), fix), pl.BlockSpec((1, d), fix),
                pl.BlockSpec((d, LANES), fix), pl.BlockSpec((1, LANES), fix),
                pl.BlockSpec((d, LANES), fix), pl.BlockSpec((1, LANES), fix)]
    args = [om, orr, x, wo1, wo2, g, wg, bg, wr, br]
    body, aliases = _out_kernel, {}
    if h2_prev is not None:
        in_specs.append(pl.BlockSpec(memory_space=pl.ANY))
        args.append(h2_prev)
        body, aliases = _out_kernel_into, {len(args) - 1: 1}
    return pl.pallas_call(
        functools.partial(body, nsteps=nsteps, ngroups=ngroups, nexperts=nexperts),
        grid=(nsteps + ntail,),
        in_specs=in_specs,
        out_specs=[pl.BlockSpec((tm, d), row),
                   pl.BlockSpec((tm * (d // LANES), LANES), lambda i: (h2_row0 // tm + i, 0)),
                   pl.BlockSpec((tm, LANES), row)],
        out_shape=[jax.ShapeDtypeStruct((rows, d), F32),
                   jax.ShapeDtypeStruct((h2_rows * (d // LANES), LANES), F32),
                   jax.ShapeDtypeStruct((rows, LANES), F32)],
        input_output_aliases=aliases,
        compiler_params=_cparams(("arbitrary",)),
        name="out_proj_router",
    )(*args)


NWBUF = 3
NYBUF = 3


def _expert_tile(npair, nexperts):
    share = npair / nexperts
    return max(8, -(-int(math.ceil((share + 2.5 * math.sqrt(share)) / 2)) // 8) * 8)


def _moe_kernel(tt_ref, ms_ref, sp_ref, h2_hbm, wg_hbm, wu_hbm, wd_hbm, ys_hbm,
                xbuf, ybuf, wgb, wub, wdb, gsem, ssem, wsem, *, tm, ntok, nck, pitch, ntiles):
    i = pl.program_id(0)
    nused = ms_ref[0]
    nue = ms_ref[1]
    npair = 2 * ntok

    def hbm_row(ref, t):
        return ref.at[pl.ds(t * nck, nck), :]

    def buf_row(buf, slot, r):
        return buf.at[slot, pl.ds(r * pitch, nck), :]

    def weight_copies(q):
        e = ms_ref[2 + q]
        w = q % NWBUF
        return [pltpu.make_async_copy(src.at[e], dst.at[w], wsem.at[w])
                for src, dst in ((wg_hbm, wgb), (wu_hbm, wub), (wd_hbm, wdb))]

    def gather(tile, slot):
        src0 = tt_ref[0, tile]
        last = src0 + tt_ref[1, tile] - 1
        for r in range(tm):
            pair = sp_ref[jnp.minimum(src0 + r, last)]
            tok = pair - jnp.where(pair >= ntok, ntok, 0)
            pltpu.make_async_copy(hbm_row(h2_hbm, tok), buf_row(xbuf, slot, r), gsem.at[slot]).start(priority=1)

    def scatter(tile, slot, n):
        src0 = tt_ref[0, tile]
        for r in range(tm):
            pair = sp_ref[jnp.minimum(src0 + r, npair - 1)]
            dst = jnp.where(r < n, pair, npair + slot * tm + r)
            pltpu.make_async_copy(buf_row(ybuf, slot, r), hbm_row(ys_hbm, dst), ssem.at[slot]).start()

    def wait_tile(buf, sem, slot):
        pltpu.make_async_copy(h2_hbm.at[pl.ds(0, tm * nck), :], buf.at[slot, pl.ds(0, tm * nck), :],
                              sem.at[slot]).wait()

    @pl.when(i == 0)
    def _():
        for cp in weight_copies(0):
            cp.start()

        @pl.when(nue > 1)
        def _():
            for cp in weight_copies(1):
                cp.start()

        gather(0, 0)
        ybuf[...] = jnp.zeros(ybuf.shape, F32)
        for s in range(NYBUF - 1):
            scatter(0, s, 0)

    @pl.when(i < nused)
    def _():
        xs = i % 2
        ys = i % NYBUF
        wait_tile(xbuf, gsem, xs)
        q = tt_ref[2, i]

        @pl.when((i == 0) | (q != tt_ref[2, jnp.maximum(i - 1, 0)]))
        def _():
            for cp in weight_copies(q):
                cp.wait()

            @pl.when(q + 2 < nue)
            def _():
                for cp in weight_copies(q + 2):
                    cp.start()

        w = q % NWBUF
        x = _load_chunked(xbuf.at[xs], tm, nck, pitch)
        a = jax.nn.silu(_dot(x, wgb[w])) * _dot(x, wub[w])
        gather(jnp.minimum(i + 1, nused - 1), 1 - xs)
        prev = jnp.maximum(i - 1, 0)
        scatter(prev, (i + NYBUF - 1) % NYBUF, jnp.where(i > 0, tt_ref[1, prev], 0))
        wait_tile(ybuf, ssem, ys)
        _store_chunked(ybuf.at[ys], _dot(a, wdb[w]), pitch)

        @pl.when(i == nused - 1)
        def _():
            scatter(i, ys, tt_ref[1, i])

    @pl.when(i == ntiles - 1)
    def _():
        wait_tile(xbuf, gsem, nused % 2)
        for s in range(NYBUF):
            wait_tile(ybuf, ssem, s)


def _moe(tile_tab, misc, sorted_pairs, h2, w_gate, w_up, w_down, *, tm):
    ntiles = tile_tab.shape[1]
    ntok = sorted_pairs.shape[0] // 2
    d, f = w_gate.shape[1:]
    nck = d // LANES
    pitch = nck + 1
    anyspec = pl.BlockSpec(memory_space=pl.ANY)
    grid_spec = pltpu.PrefetchScalarGridSpec(
        num_scalar_prefetch=3,
        grid=(ntiles,),
        in_specs=[anyspec, anyspec, anyspec, anyspec],
        out_specs=anyspec,
        scratch_shapes=[pltpu.VMEM((2, tm * pitch, LANES), F32), pltpu.VMEM((NYBUF, tm * pitch, LANES), F32),
                        pltpu.VMEM((NWBUF, d, f), F32), pltpu.VMEM((NWBUF, d, f), F32),
                        pltpu.VMEM((NWBUF, f, d), F32),
                        pltpu.SemaphoreType.DMA((2,)), pltpu.SemaphoreType.DMA((NYBUF,)),
                        pltpu.SemaphoreType.DMA((NWBUF,))],
    )
    return pl.pallas_call(
        functools.partial(_moe_kernel, tm=tm, ntok=ntok, nck=nck, pitch=pitch, ntiles=ntiles),
        grid_spec=grid_spec,
        out_shape=jax.ShapeDtypeStruct(((2 * ntok + NYBUF * tm) * nck, LANES), F32),
        compiler_params=_cparams(("arbitrary",)),
        name="moe_experts",
    )(tile_tab, misc, sorted_pairs, h2, w_gate, w_up, w_down)


def _comb_kernel(x1_ref, ya_ref, yb_ref, rt_ref, g_ref, y_ref):
    tm, d = x1_ref.shape
    nck = d // LANES
    rt = rt_ref[...]
    x = (x1_ref[...] + rt[:, 0:1] * _load_chunked(ya_ref, tm, nck, nck)
         + rt[:, 1:2] * _load_chunked(yb_ref, tm, nck, nck))
    y_ref[...] = _rms(x, g_ref[...])


def _combine(x1, ys, rt, g, *, tm, row0, ntok, rows):
    d = x1.shape[1]
    nck = d // LANES
    return pl.pallas_call(
        _comb_kernel,
        grid=(rows // tm,),
        in_specs=[pl.BlockSpec((tm, d), lambda i: (i, 0)),
                  pl.BlockSpec((tm * nck, LANES), lambda i: (row0 // tm + i, 0)),
                  pl.BlockSpec((tm * nck, LANES), lambda i: ((ntok + row0) // tm + i, 0)),
                  pl.BlockSpec((tm, LANES), lambda i: (i, 0)), pl.BlockSpec((1, d), lambda i: (0, 0))],
        out_specs=pl.BlockSpec((tm, d), lambda i: (i, 0)),
        out_shape=jax.ShapeDtypeStruct((rows, d), F32),
        compiler_params=_cparams(("parallel",)),
        name="moe_combine_norm",
    )(x1, ys, ys, rt, g)


def _rope_tables(pos, dim):
    half = dim // 2
    inv = jnp.exp(jnp.arange(half, dtype=F32) * (-2.0 * math.log(ROPE_BASE) / dim))
    ang = pos.astype(F32)[:, None] * inv[None, :]
    return jnp.cos(ang), jnp.sin(ang)


def _rot_cols(w):
    half = w.shape[-1] // 2
    return jnp.concatenate([-w[..., half:], w[..., :half]], axis=-1)


def _pick_tile(rows, target):
    t = min(rows, target)
    while rows % t:
        t //= 2
    return t


def _dispatch(route, ntok, nexperts, tm, ntiles):
    i32 = jnp.int32
    ids = jnp.concatenate([route[:, 2], route[:, 3]]).astype(i32)
    npair = 2 * ntok
    _, sorted_pairs = lax.sort((ids, jnp.arange(npair, dtype=i32)), num_keys=1, is_stable=True)
    ex = jnp.arange(nexperts, dtype=i32)
    counts = jnp.sum((ids[:, None] == ex[None, :]).astype(i32), axis=0)
    starts = jnp.cumsum(counts) - counts
    tcount = (counts + tm - 1) // tm
    tend = jnp.cumsum(tcount)
    tstart = tend - tcount
    nused = tend[-1]
    t = jnp.arange(ntiles, dtype=i32)
    tile_expert = jnp.minimum(jnp.sum((tend[None, :] <= t[:, None]).astype(i32), axis=1), nexperts - 1)
    pick = lambda tab: jnp.sum(jnp.where(tile_expert[:, None] == ex[None, :], tab[None, :], 0), axis=1)
    local = t - pick(tstart)
    tile_src0 = jnp.clip(pick(starts) + local * tm, 0, npair - 1)
    tile_rows = jnp.where(t < nused, jnp.clip(pick(counts) - local * tm, 0, tm), 0)
    used = counts > 0
    tile_q = jnp.sum((used[None, :] & (ex[None, :] < tile_expert[:, None])).astype(i32), axis=1)
    used_ids = jnp.sort(jnp.where(used, ex, nexperts))
    tile_tab = jnp.stack([tile_src0, tile_rows, tile_q]).astype(i32)
    misc = jnp.concatenate([nused.reshape(1), jnp.sum(used.astype(i32)).reshape(1), used_ids]).astype(i32)
    return tile_tab, misc, sorted_pairs


def kernel(x_prompt, x_sample, cache_kv_latent, cache_k_rope, state_retention, page_table, meta_tokens,
           ln_mix_g, w_in, q_norm_g, w_uq, w_uk, kv_norm_g, w_uv, mla_out_g, ret_gn_g, w_o, ln_ffn_g,
           w_group, b_group, w_router, b_router, w_gate, w_up, w_down, final_g):
    batch, seq, d = x_prompt.shape
    db, dec_seq, _ = x_sample.shape
    depth = w_in.shape[0]
    assert depth == 1 and dec_seq == 1
    ql = q_norm_g.shape[1]
    kvl = kv_norm_g.shape[1]
    mh, nr = w_uq.shape[2], w_uq.shape[3]
    nope = w_uk.shape[3]
    rope = nr - nope
    vdim = w_uv.shape[3]
    rdim = ret_gn_g.shape[1]
    dk = state_retention.shape[3]
    rh = rdim // dk
    ngroups = w_group.shape[2]
    nexperts = w_router.shape[2]
    page = cache_kv_latent.shape[2]
    npages = page_table.shape[1]
    past_len = npages * page
    chunk = min(2 * ROW_TILE, seq)
    tk = min(4 * ROW_TILE, seq)
    nmain = batch * seq
    naux = -(-(db + N_META) // LANES) * LANES
    ts = -(-db // ROW_TILE) * ROW_TILE
    assert seq % 256 == 0 and seq % tk == 0 and ts <= naux and db % 8 == 0 and rh == mh

    wit = jnp.swapaxes(w_in[0], 0, 1)
    o_kr = ql + kvl
    o_ret = o_kr + rope
    w_a = jnp.concatenate([wit[:o_ret], jnp.swapaxes(_rot_cols(jnp.swapaxes(wit[o_kr:o_ret], 0, 1)), 0, 1)],
                          axis=0).astype(BF16)
    w_b = wit[o_ret:].astype(BF16)
    wq = w_uq[0]
    wq_rope = wq[:, :, nope:]
    wuq = jnp.concatenate([wq[:, :, :nope].reshape(ql, mh * nope), wq_rope.reshape(ql, mh * rope),
                           _rot_cols(wq_rope).reshape(ql, mh * rope)], axis=1).astype(BF16)
    wukt = jnp.transpose(w_uk[0], (1, 2, 0)).astype(BF16)
    wuv = jnp.transpose(w_uv[0], (1, 0, 2)).astype(BF16)
    wo = w_o[0].astype(BF16)
    wo1, wo2 = wo[:mh * vdim], wo[mh * vdim:]
    wg_pad = jnp.zeros((d, LANES), F32).at[:, :ngroups].set(w_group[0]).astype(BF16)
    bg_pad = jnp.zeros((1, LANES), F32).at[0, :ngroups].set(b_group[0])
    wr_pad = jnp.zeros((d, LANES), F32).at[:, :nexperts].set(w_router[0]).astype(BF16)
    br_pad = jnp.zeros((1, LANES), F32).at[0, :nexperts].set(b_router[0])

    pos_main = N_META + jnp.arange(seq, dtype=jnp.int32)
    pos_aux = jnp.concatenate([jnp.full((db,), past_len, jnp.int32), jnp.arange(N_META, dtype=jnp.int32),
                               jnp.zeros((naux - db - N_META,), jnp.int32)])

    def tables(pos):
        c64, s64 = _rope_tables(pos, rope)
        c128, s128 = _rope_tables(pos, dk)
        cs = jnp.concatenate([c64, c64], axis=1)
        sn = jnp.concatenate([s64, s64], axis=1)
        return (cs, sn, jnp.tile(cs, (1, 2)), jnp.tile(sn, (1, 2)),
                jnp.concatenate([c128, c128], axis=1), jnp.concatenate([-s128, s128], axis=1))

    log_g = jnp.log1p(-jnp.exp2(-5.0 - jnp.arange(rh, dtype=F32)))
    n = jnp.arange(chunk, dtype=F32)
    diff = n[:, None] - n[None, :]
    t_dec = jnp.where(diff >= 0, jnp.exp(log_g[:, None, None] * jnp.maximum(diff, 0.0)), 0.0)
    t_qd = jnp.broadcast_to(jnp.exp((n[None, :] + 1.0) * log_g[:, None])[:, :, None], (rh, chunk, dk))
    k_decay = jnp.exp((chunk - 1.0 - n)[None, :] * log_g[:, None])
    t_kd = jnp.broadcast_to(k_decay[:, :, None], (rh, chunk, dk))
    t_kdm = jnp.broadcast_to(k_decay[:, chunk - N_META:, None], (rh, N_META, dk))
    t_gc = jnp.broadcast_to(jnp.exp(chunk * log_g)[:, None, None], (rh, dk, dk))
    t_gam = jnp.broadcast_to(jnp.exp(log_g)[:, None, None], (rh, 8, dk))

    x_main = x_prompt.reshape(nmain, d)
    x_aux = jnp.concatenate([x_sample.reshape(db, d), meta_tokens.astype(F32),
                             jnp.zeros((naux - db - N_META, d), F32)], axis=0)
    g_mix = ln_mix_g[0][None]
    scale = float(nope + rope) ** -0.5
    tm_main = _pick_tile(seq, 512)
    tm_aux = _pick_tile(naux, 512)

    def token_front(x, pos, tm):
        cs, sn, cs8, sn8, cos_r, sin_r = tables(pos)
        cq, kv, kvb, kr, krb = _in_a(x, g_mix, w_a, q_norm_g, kv_norm_g, cs, sn,
                                     tm=tm, ql=ql, kvl=kvl, rope=rope)
        rb = _in_b(x, g_mix, w_b, cos_r, sin_r, tm=tm, heads=rh, dk=dk)
        q_lat, q_rope = _q_proj(cq, wuq, wukt, cs8, sn8, tm=tm, heads=mh, nope=nope, rope=rope, scale=scale)
        return kv, kvb, kr, krb, rb, q_lat, q_rope

    kv_m, kvb_m, kr_m, krb_m, rb_m, ql_m, qr_m = token_front(x_main, pos_main, tm_main)
    kv_a, kvb_a, kr_a, krb_a, rb_a, ql_a, qr_a = token_front(x_aux, pos_aux, tm_aux)
    meta = slice(db, db + N_META)

    om_m = _attention(ql_m, qr_m, kvb_m, krb_m, kvb_a[meta], krb_a[meta], wuv, mla_out_g,
                      batch=batch, seq=seq, tq=ROW_TILE, tk=tk)
    rdk = rh * dk
    or_m, st_p = _retention_prompt(rb_m, rb_a[meta, rdk:2 * rdk], rb_a[meta, 2 * rdk:3 * rdk],
                                   (t_dec, t_qd, t_kd, t_kdm, t_gc), ret_gn_g,
                                   batch=batch, seq=seq, heads=rh, dk=dk, chunk=chunk)

    ql_s = jnp.transpose(ql_a[:, :db], (1, 0, 2))
    qr_s = jnp.transpose(qr_a[:, :db], (1, 0, 2))
    o_s = _decode_attention(page_table, ql_s, qr_s, kv_a[:db, None, :], kr_a[:db, None, :],
                            cache_kv_latent[0], jnp.swapaxes(cache_k_rope[0], 1, 2),
                            gpages=min(16, npages), nbuf=3)
    om_s = _decode_out(jnp.transpose(o_s, (1, 0, 2)), wuv, mla_out_g)
    rs = rb_a[:db].astype(F32)
    or_s, st_s = _retention_sample(rs[:, :rdk], rs[:, rdk:2 * rdk], rs[:, 2 * rdk:3 * rdk], rs[:, 3 * rdk:],
                                   t_gam, ret_gn_g, state_retention, heads=rh, dk=dk, bs=8)
    pad_rows = lambda a: jnp.concatenate([a, jnp.zeros((ts - db, a.shape[1]), a.dtype)], axis=0)
    om_a, or_a = pad_rows(om_s), pad_rows(or_s)

    g_ffn = ln_ffn_g[0][None]
    ntok = nmain + ts
    tm_o = _pick_tile(nmain, 256)
    outp = functools.partial(_out_proj, h2_rows=nmain + -(-ts // tm_o) * tm_o, ngroups=ngroups, nexperts=nexperts)
    x1_m, h2, rt_m = outp(om_m, or_m, x_main, wo1, wo2, g_ffn, wg_pad, bg_pad, wr_pad, br_pad,
                          rows=nmain, tm=tm_o, h2_row0=0, h2_prev=None)
    x1_a, h2, rt_a = outp(om_a, or_a, x_aux, wo1, wo2, g_ffn, wg_pad, bg_pad, wr_pad, br_pad,
                          rows=ts, tm=ROW_TILE, h2_row0=nmain, h2_prev=h2)

    route = jnp.concatenate([rt_m[:, :4], rt_a[:, :4]], axis=0)
    tm_e = _expert_tile(2 * ntok, nexperts)
    ntiles =-(-(2 * ntok + nexperts * (tm_e - 1)) // tm_e)
    ys = _moe(*_dispatch(route, ntok, nexperts, tm_e, ntiles), h2, w_gate[0], w_up[0], w_down[0], tm=tm_e)
    fg = final_g[None]
    y_m = _combine(x1_m, ys, rt_m, fg, tm=ROW_TILE, row0=0, ntok=ntok, rows=nmain)
    y_a = _combine(x1_a, ys, rt_a, fg, tm=ROW_TILE, row0=nmain, ntok=ntok, rows=ts)

    def with_meta(a_main, a_aux):
        w = a_main.shape[1]
        m = jnp.broadcast_to(a_aux[meta][None], (batch, N_META, w))
        return jnp.concatenate([m, a_main.reshape(batch, seq, w)], axis=1)[None]

    return (y_m.reshape(batch, seq, d), y_a[:db].reshape(db, 1, d),
            with_meta(kv_m, kv_a), with_meta(kr_m, kr_a), st_p,
            kv_a[:db].reshape(1, db, 1, kvl), kr_a[:db].reshape(1, db, 1, rope), st_s)
```

```python
import functools
import math

import jax
import jax.numpy as jnp
from jax import lax
from jax.experimental import pallas as pl
from jax.experimental.pallas import tpu as pltpu

F32 = jnp.float32
BF16 = jnp.bfloat16

N_META = 16
ROPE_BASE = 10000.0
EPS = 1e-6
NEG = -1e30
EXPERTS_PER_GROUP = 8
LANES = 128
ROW_TILE = 128
VMEM_LIMIT = 56 * 1024 * 1024


def _cparams(sem, vmem=VMEM_LIMIT):
    return pltpu.CompilerParams(dimension_semantics=sem, vmem_limit_bytes=vmem)


def _rms(x, g):
    return x * lax.rsqrt(jnp.mean(x * x, axis=-1, keepdims=True) + EPS) * g


def _dot(a, b):
    return jnp.dot(a, b, preferred_element_type=F32)


def _dot_nt(a, b):
    return lax.dot_general(a, b, (((1,), (1,)), ((), ())), preferred_element_type=F32)


def _dot_tn(a, b):
    return lax.dot_general(a, b, (((0,), (0,)), ((), ())), preferred_element_type=F32)


def _store_chunked(ref, val, pitch, base=0):
    rows, d = val.shape
    for c in range(d // LANES):
        ref[pl.ds(base + c, rows, stride=pitch), :] = val[:, c * LANES:(c + 1) * LANES]


def _load_chunked(ref, rows, nchunk, pitch, base=0):
    return jnp.concatenate([ref[pl.ds(base + c, rows, stride=pitch), :] for c in range(nchunk)], axis=1)


def _in_a_kernel(x_ref, g_ref, w_ref, qg_ref, kvg_ref, cs_ref, sn_ref,
                 cq_ref, kv_ref, kvb_ref, kr_ref, krb_ref, *, ql, kvl, rope):
    h = _rms(x_ref[...], g_ref[...]).astype(BF16)
    c = _dot_nt(h, w_ref[...])
    cq_ref[...] = _rms(c[:, :ql], qg_ref[...]).astype(BF16)
    kv = _rms(c[:, ql:ql + kvl], kvg_ref[...])
    kv_ref[...] = kv
    kvb_ref[...] = kv.astype(BF16)
    a = c[:, ql + kvl:ql + kvl + rope]
    b = c[:, ql + kvl + rope:]
    kr = a * cs_ref[...] + b * sn_ref[...]
    kr_ref[...] = kr
    krb_ref[...] = kr.astype(BF16)


def _in_a(x, g, w_a, qg, kvg, cs, sn, *, tm, ql, kvl, rope):
    rows, d = x.shape
    n = w_a.shape[0]
    row = lambda i: (i, 0)
    fix = lambda i: (0, 0)
    tab = lambda i: (i % (cs.shape[0] // tm), 0)
    return pl.pallas_call(
        functools.partial(_in_a_kernel, ql=ql, kvl=kvl, rope=rope),
        grid=(rows // tm,),
        in_specs=[pl.BlockSpec((tm, d), row), pl.BlockSpec((1, d), fix), pl.BlockSpec((n, d), fix),
                  pl.BlockSpec((1, ql), fix), pl.BlockSpec((1, kvl), fix),
                  pl.BlockSpec((tm, rope), tab), pl.BlockSpec((tm, rope), tab)],
        out_specs=[pl.BlockSpec((tm, ql), row), pl.BlockSpec((tm, kvl), row), pl.BlockSpec((tm, kvl), row),
                   pl.BlockSpec((tm, rope), row), pl.BlockSpec((tm, rope), row)],
        out_shape=[jax.ShapeDtypeStruct((rows, ql), BF16), jax.ShapeDtypeStruct((rows, kvl), F32),
                   jax.ShapeDtypeStruct((rows, kvl), BF16), jax.ShapeDtypeStruct((rows, rope), F32),
                   jax.ShapeDtypeStruct((rows, rope), BF16)],
        compiler_params=_cparams(("parallel",)),
        name="in_proj_a",
    )(x, g, w_a, qg, kvg, cs, sn)


def _in_b_kernel(x_ref, g_ref, w_ref, cos_ref, sin_ref, out_ref, h_scr, *, heads, dk):
    j = pl.program_id(1)

    @pl.when(j == 0)
    def _():
        h_scr[...] = _rms(x_ref[...], g_ref[...]).astype(BF16)

    c = _dot_nt(h_scr[...], w_ref[...])

    @pl.when(j < 2)
    def _():
        scale = jnp.where(j == 1, dk ** -0.5, 1.0).astype(F32)
        cos = cos_ref[...]
        sin = sin_ref[...]
        for hh in range(heads):
            blk = c[:, hh * dk:(hh + 1) * dk]
            r = blk * cos + pltpu.roll(blk, dk // 2, 1) * sin
            out_ref[:, hh * dk:(hh + 1) * dk] = (r * scale).astype(BF16)

    @pl.when(j >= 2)
    def _():
        out_ref[...] = c.astype(BF16)


def _in_b(x, g, w_b, cos, sin, *, tm, heads, dk):
    rows, d = x.shape
    n = heads * dk
    tab = lambda i, j: (i % (cos.shape[0] // tm), 0)
    return pl.pallas_call(
        functools.partial(_in_b_kernel, heads=heads, dk=dk),
        grid=(rows // tm, 4),
        in_specs=[pl.BlockSpec((tm, d), lambda i, j: (i, 0)), pl.BlockSpec((1, d), lambda i, j: (0, 0)),
                  pl.BlockSpec((n, d), lambda i, j: (j, 0)),
                  pl.BlockSpec((tm, dk), tab), pl.BlockSpec((tm, dk), tab)],
        out_specs=pl.BlockSpec((tm, n), lambda i, j: (i, j)),
        out_shape=jax.ShapeDtypeStruct((rows, 4 * n), BF16),
        scratch_shapes=[pltpu.VMEM((tm, d), BF16)],
        compiler_params=_cparams(("parallel", "arbitrary")),
        name="in_proj_b",
    )(x, g, w_b, cos, sin)


def _q_kernel(cq_ref, wuq_ref, wukt_ref, cos_ref, sin_ref, ql_ref, qr_ref, *, heads, nope, rope, scale):
    qf = _dot(cq_ref[...], wuq_ref[...])
    o1 = heads * nope
    o2 = o1 + heads * rope
    cos = cos_ref[...]
    sin = sin_ref[...]
    for hp in range(heads // 2):
        lo, hi = hp * 2 * rope, (hp + 1) * 2 * rope
        qr = (qf[:, o1 + lo:o1 + hi] * cos + qf[:, o2 + lo:o2 + hi] * sin) * scale
        qr_ref[2 * hp] = qr[:, :rope].astype(BF16)
        qr_ref[2 * hp + 1] = qr[:, rope:].astype(BF16)
    for hh in range(heads):
        qn = qf[:, hh * nope:(hh + 1) * nope].astype(BF16)
        ql_ref[hh] = (_dot(qn, wukt_ref[hh]) * scale).astype(BF16)


def _q_proj(cq, wuq, wukt, cos, sin, *, tm, heads, nope, rope, scale):
    rows, ql = cq.shape
    kvl = wukt.shape[2]
    nq = wuq.shape[1]
    tab = lambda i: (i % (cos.shape[0] // tm), 0)
    return pl.pallas_call(
        functools.partial(_q_kernel, heads=heads, nope=nope, rope=rope, scale=scale),
        grid=(rows // tm,),
        in_specs=[pl.BlockSpec((tm, ql), lambda i: (i, 0)), pl.BlockSpec((ql, nq), lambda i: (0, 0)),
                  pl.BlockSpec((heads, nope, kvl), lambda i: (0, 0, 0)),
                  pl.BlockSpec((tm, 2 * rope), tab), pl.BlockSpec((tm, 2 * rope), tab)],
        out_specs=[pl.BlockSpec((heads, tm, kvl), lambda i: (0, i, 0)),
                   pl.BlockSpec((heads, tm, rope), lambda i: (0, i, 0))],
        out_shape=[jax.ShapeDtypeStruct((heads, rows, kvl), BF16),
                   jax.ShapeDtypeStruct((heads, rows, rope), BF16)],
        compiler_params=_cparams(("parallel",)),
        name="q_proj",
    )(cq, wuq, wukt, cos, sin)


def _uv_norm(o_heads, wuv_ref, g):
    parts = [_dot(o.astype(BF16), wuv_ref[hh]) for hh, o in enumerate(o_heads)]
    return _rms(jnp.concatenate(parts, axis=1), g).astype(BF16)


def _widen(x, n):
    if n <= LANES:
        return x[:, :n]
    return jnp.concatenate([x] * (n // LANES), axis=1)


def _att_kernel(qi_ref, kj_ref, ql_ref, qr_ref, kv_ref, kr_ref, kvm_ref, krm_ref, wuv_ref, g_ref, o_ref,
                m_scr, l_scr, acc_scr, *, heads, tq, tk):
    qi = qi_ref[pl.program_id(1)]
    kj = kj_ref[pl.program_id(1)]
    kvl = ql_ref.shape[2]
    rows = heads * tq
    ql = ql_ref[...].reshape(rows, kvl)
    qr = qr_ref[...].reshape(rows, qr_ref.shape[2])

    def update(s, kv, first):
        m_cur = jnp.max(s, axis=1, keepdims=True)
        if first:
            m_new = jnp.broadcast_to(m_cur, (rows, LANES))
        else:
            m_prev = m_scr[...]
            m_new = jnp.maximum(m_prev, m_cur)
        p = jnp.exp(s - _widen(m_new, s.shape[1]))
        p_sum = jnp.sum(p, axis=1, keepdims=True)
        pv = _dot(p.astype(BF16), kv)
        if first:
            l_scr[...] = jnp.broadcast_to(p_sum, (rows, LANES))
            acc_scr[...] = pv
        else:
            alpha = jnp.exp(m_prev - m_new)
            l_scr[...] = alpha * l_scr[...] + p_sum
            acc_scr[...] = acc_scr[...] * _widen(alpha, kvl) + pv
        m_scr[...] = m_new

    @pl.when(kj == 0)
    def _():
        kvm = kvm_ref[...]
        update(_dot_nt(ql, kvm) + _dot_nt(qr, krm_ref[...]), kvm, True)

    last = (qi * tq + tq - 1) // tk

    @pl.when(kj < last)
    def _():
        kv = kv_ref[...]
        update(_dot_nt(ql, kv) + _dot_nt(qr, kr_ref[...]), kv, False)

    @pl.when(kj == last)
    def _():
        kv = kv_ref[...]
        s = _dot_nt(ql, kv) + _dot_nt(qr, kr_ref[...])
        qpos = qi * tq + lax.broadcasted_iota(jnp.int32, (rows, tk), 0) % tq
        kpos = kj * tk + lax.broadcasted_iota(jnp.int32, (rows, tk), 1)
        update(jnp.where(kpos <= qpos, s, NEG), kv, False)
        o = acc_scr[...] / _widen(l_scr[...], kvl)
        o_ref[...] = _uv_norm([o[hh * tq:(hh + 1) * tq] for hh in range(heads)], wuv_ref, g_ref[...])


def _attention(ql, qr, kvb, krb, kvm, krm, wuv, g, *, batch, seq, tq, tk):
    heads, rows, kvl = ql.shape
    rope = qr.shape[2]
    vdim = wuv.shape[2]
    nq, nk = seq // tq, seq // tk
    nmeta = kvm.shape[0]
    pairs = [(qi, kj) for qi in range(nq) for kj in range((qi * tq + tq - 1) // tk + 1)]
    qi_tab = jnp.asarray([p[0] for p in pairs], jnp.int32)
    kj_tab = jnp.asarray([p[1] for p in pairs], jnp.int32)
    qidx = lambda b, s, qt, kt: (0, b * nq + qt[s], 0)
    kidx = lambda b, s, qt, kt: (b * nk + kt[s], 0)
    fix2 = lambda b, s, qt, kt: (0, 0)
    grid_spec = pltpu.PrefetchScalarGridSpec(
        num_scalar_prefetch=2,
        grid=(batch, len(pairs)),
        in_specs=[pl.BlockSpec((heads, tq, kvl), qidx), pl.BlockSpec((heads, tq, rope), qidx),
                  pl.BlockSpec((tk, kvl), kidx), pl.BlockSpec((tk, rope), kidx),
                  pl.BlockSpec((nmeta, kvl), fix2), pl.BlockSpec((nmeta, rope), fix2),
                  pl.BlockSpec((heads, kvl, vdim), lambda b, s, qt, kt: (0, 0, 0)),
                  pl.BlockSpec((1, heads * vdim), fix2)],
        out_specs=pl.BlockSpec((tq, heads * vdim), lambda b, s, qt, kt: (b * nq + qt[s], 0)),
        scratch_shapes=[pltpu.VMEM((heads * tq, LANES), F32), pltpu.VMEM((heads * tq, LANES), F32),
                        pltpu.VMEM((heads * tq, kvl), F32)],
    )
    return pl.pallas_call(
        functools.partial(_att_kernel, heads=heads, tq=tq, tk=tk),
        grid_spec=grid_spec,
        out_shape=jax.ShapeDtypeStruct((rows, heads * vdim), BF16),
        compiler_params=_cparams(("parallel", "arbitrary")),
        name="mla_prompt_attention",
    )(qi_tab, kj_tab, ql, qr, kvb, krb, kvm, krm, wuv, g)


def _dec_kernel(pt_ref, ql_ref, qr_ref, kvn_ref, krn_ref, ckv_hbm, ckr_hbm, o_ref,
                kvbuf, krbuf, sem, *, nchunk, gpages, nbuf, total):
    b = pl.program_id(0)
    page = kvbuf.shape[2]
    kvl = kvbuf.shape[3]

    def copies(g, slot):
        bb = g // nchunk
        c0 = (g % nchunk) * gpages
        out = []
        for p in range(gpages):
            pg = pt_ref[bb, c0 + p]
            out.append(pltpu.make_async_copy(ckv_hbm.at[pg], kvbuf.at[slot, p], sem.at[slot]))
            out.append(pltpu.make_async_copy(ckr_hbm.at[pg], krbuf.at[slot, p], sem.at[slot]))
        return out

    def start(g, slot):
        for cp in copies(g, slot):
            cp.start()

    @pl.when(b == 0)
    def _():
        for g0 in range(min(nbuf - 1, total)):
            start(g0, g0 % nbuf)

    ql = ql_ref[0].astype(F32)
    qr = qr_ref[0].astype(F32)
    kvn = kvn_ref[0]
    krn = krn_ref[0]
    heads = ql.shape[0]
    m0 = jnp.sum(ql * kvn, axis=1, keepdims=True) + jnp.sum(qr * krn, axis=1, keepdims=True)
    l0 = jnp.ones((heads, 1), F32)
    acc0 = jnp.broadcast_to(kvn, (heads, kvl))

    def body(c, carry):
        m, l, acc = carry
        g = b * nchunk + c
        slot = g % nbuf
        for cp in copies(g, slot):
            cp.wait()
        nxt = g + nbuf - 1

        @pl.when(nxt < total)
        def _():
            start(nxt, nxt % nbuf)

        kvc = kvbuf[slot].reshape(gpages * page, kvl)
        krt = jnp.concatenate([krbuf[slot, p] for p in range(gpages)], axis=1)
        s = _dot_nt(ql, kvc) + _dot(qr, krt)
        m_new = jnp.maximum(m, jnp.max(s, axis=1, keepdims=True))
        alpha = jnp.exp(m - m_new)
        p = jnp.exp(s - m_new)
        l = l * alpha + jnp.sum(p, axis=1, keepdims=True)
        acc = acc * alpha + _dot(p, kvc)
        return m_new, l, acc

    m, l, acc = lax.fori_loop(0, nchunk, body, (m0, l0, acc0))
    o_ref[0] = acc / l


def _decode_attention(page_table, ql_s, qr_s, kvn, krn, cache_kv, cache_krt, *, gpages, nbuf):
    db, npages = page_table.shape
    heads, kvl = ql_s.shape[1:]
    rope = qr_s.shape[2]
    page = cache_kv.shape[1]
    nchunk = npages // gpages
    total = db * nchunk
    grid_spec = pltpu.PrefetchScalarGridSpec(
        num_scalar_prefetch=1,
        grid=(db,),
        in_specs=[pl.BlockSpec((1, heads, kvl), lambda b, pt: (b, 0, 0)),
                  pl.BlockSpec((1, heads, rope), lambda b, pt: (b, 0, 0)),
                  pl.BlockSpec((1, 1, kvl), lambda b, pt: (b, 0, 0)),
                  pl.BlockSpec((1, 1, rope), lambda b, pt: (b, 0, 0)),
                  pl.BlockSpec(memory_space=pl.ANY), pl.BlockSpec(memory_space=pl.ANY)],
        out_specs=pl.BlockSpec((1, heads, kvl), lambda b, pt: (b, 0, 0)),
        scratch_shapes=[pltpu.VMEM((nbuf, gpages, page, kvl), F32),
                        pltpu.VMEM((nbuf, gpages, rope, page), F32),
                        pltpu.SemaphoreType.DMA((nbuf,))],
    )
    return pl.pallas_call(
        functools.partial(_dec_kernel, nchunk=nchunk, gpages=gpages, nbuf=nbuf, total=total),
        grid_spec=grid_spec,
        out_shape=jax.ShapeDtypeStruct((db, heads, kvl), F32),
        compiler_params=_cparams(("arbitrary",)),
        name="mla_decode_attention",
    )(page_table, ql_s, qr_s, kvn, krn, cache_kv, cache_krt)


def _dec_out_kernel(o_ref, wuv_ref, g_ref, out_ref):
    heads = o_ref.shape[0]
    out_ref[...] = _uv_norm([o_ref[hh] for hh in range(heads)], wuv_ref, g_ref[...])


def _decode_out(o_hm, wuv, g):
    heads, rows, kvl = o_hm.shape
    vdim = wuv.shape[2]
    return pl.pallas_call(
        _dec_out_kernel,
        out_shape=jax.ShapeDtypeStruct((rows, heads * vdim), BF16),
        name="mla_decode_out",
    )(o_hm, wuv, g)


def _head_norm_gate(o, gn, rg):
    mu = jnp.mean(o, axis=-1, keepdims=True)
    var = jnp.mean(jnp.square(o - mu), axis=-1, keepdims=True)
    y = (o - mu) * lax.rsqrt(var + EPS) * gn
    return y * jax.nn.silu(rg)


def _ret_kernel(rq_ref, rk_ref, rv_ref, rg_ref, rkm_ref, rvm_ref, dec_ref, qd_ref, kd_ref, kdm_ref, gc_ref,
                gn_ref, o_ref, st_ref, *, chunk, nchunks):
    dec = dec_ref[0]
    qd = qd_ref[0]
    kd = kd_ref[0]
    gc = gc_ref[0]
    gn = gn_ref[...]
    s0 = _dot_tn((rkm_ref[...].astype(F32) * kdm_ref[0]).astype(BF16), rvm_ref[...])

    def body(c, state):
        r0 = pl.multiple_of(c * chunk, chunk)
        q = rq_ref[pl.ds(r0, chunk), :]
        k = rk_ref[pl.ds(r0, chunk), :]
        v = rv_ref[pl.ds(r0, chunk), :]
        rg = rg_ref[pl.ds(r0, chunk), :].astype(F32)
        scores = _dot_nt(q, k) * dec
        o = _dot(scores.astype(BF16), v) + _dot(q, state.astype(BF16)) * qd
        new_state = state * gc + _dot_tn((k.astype(F32) * kd).astype(BF16), v)
        o_ref[pl.ds(r0, chunk), :] = _head_norm_gate(o, gn, rg).astype(BF16)
        return new_state

    st_ref[0, 0, 0] = lax.fori_loop(0, nchunks, body, s0, unroll=2 if nchunks % 2 == 0 else 1)


def _retention_prompt(rb, rkm, rvm, tabs, gn, *, batch, seq, heads, dk, chunk):
    dec, qd, kd, kdm, gc = tabs
    nmeta = rkm.shape[0]
    col = lambda off: (lambda b, h: (b, off * heads + h))
    tab = lambda b, h: (h, 0, 0)
    return pl.pallas_call(
        functools.partial(_ret_kernel, chunk=chunk, nchunks=seq // chunk),
        grid=(batch, heads),
        in_specs=[pl.BlockSpec((seq, dk), col(0)), pl.BlockSpec((seq, dk), col(1)),
                  pl.BlockSpec((seq, dk), col(2)), pl.BlockSpec((seq, dk), col(3)),
                  pl.BlockSpec((nmeta, dk), lambda b, h: (0, h)), pl.BlockSpec((nmeta, dk), lambda b, h: (0, h)),
                  pl.BlockSpec((1, chunk, chunk), tab), pl.BlockSpec((1, chunk, dk), tab),
                  pl.BlockSpec((1, chunk, dk), tab), pl.BlockSpec((1, nmeta, dk), tab),
                  pl.BlockSpec((1, dk, dk), tab),
                  pl.BlockSpec((1, dk), lambda b, h: (0, h))],
        out_specs=[pl.BlockSpec((seq, dk), lambda b, h: (b, h)),
                   pl.BlockSpec((1, 1, 1, dk, dk), lambda b, h: (0, b, h, 0, 0))],
        out_shape=[jax.ShapeDtypeStruct((batch * seq, heads * dk), BF16),
                   jax.ShapeDtypeStruct((1, batch, heads, dk, dk), F32)],
        compiler_params=_cparams(("parallel", "parallel")),
        name="retention_prompt",
    )(rb, rb, rb, rb, rkm, rvm, dec, qd, kd, kdm, gc, gn)


def _ret_s_kernel(q_ref, k_ref, v_ref, g_ref, gam_ref, gn_ref, st_ref, o_ref, ns_ref, *, heads, dk, bs):
    rowid = lax.broadcasted_iota(jnp.int32, (8, dk), 0)
    for hh in range(heads):
        sl = slice(hh * dk, (hh + 1) * dk)
        gam = gam_ref[hh]
        outs = []
        for i in range(bs):
            q = q_ref[i:i + 1, sl]
            k = k_ref[i:i + 1, sl]
            v = v_ref[i:i + 1, sl]
            state = st_ref[0, i, hh]
            k8 = jnp.where(rowid == 0, jnp.broadcast_to(k, (8, dk)), 0.0).astype(BF16)
            v8 = jnp.broadcast_to(v, (8, dk)).astype(BF16)
            ns_ref[0, i, hh] = state * gam[0:1, :] + _dot_tn(k8, v8)
            q8 = jnp.broadcast_to(q, (8, dk)).astype(BF16)
            qs = _dot(q8, state.astype(BF16))[0:1, :]
            qk = jnp.sum(q.astype(BF16).astype(F32) * k.astype(BF16).astype(F32), axis=1, keepdims=True)
            outs.append(qk * v.astype(BF16).astype(F32) + qs * gam[0:1, :])
        o = jnp.concatenate(outs, axis=0)
        o_ref[:, sl] = _head_norm_gate(o, gn_ref[:, sl], g_ref[:, sl]).astype(BF16)


def _retention_sample(rq, rk, rv, rg, gam, gn, state, *, heads, dk, bs):
    db = rq.shape[0]
    n = heads * dk
    row = lambda i: (i, 0)
    st_spec = pl.BlockSpec((1, bs, heads, dk, dk), lambda i: (0, i, 0, 0, 0))
    return pl.pallas_call(
        functools.partial(_ret_s_kernel, heads=heads, dk=dk, bs=bs),
        grid=(db // bs,),
        in_specs=[pl.BlockSpec((bs, n), row), pl.BlockSpec((bs, n), row), pl.BlockSpec((bs, n), row),
                  pl.BlockSpec((bs, n), row), pl.BlockSpec((heads, 8, dk), lambda i: (0, 0, 0)),
                  pl.BlockSpec((1, n), lambda i: (0, 0)), st_spec],
        out_specs=[pl.BlockSpec((bs, n), row), st_spec],
        out_shape=[jax.ShapeDtypeStruct((db, n), BF16), jax.ShapeDtypeStruct(state.shape, F32)],
        compiler_params=_cparams(("parallel",)),
        name="retention_sample",
    )(rq, rk, rv, rg, gam, gn, state)


def _out_kernel(om_ref, or_ref, x_ref, wo1_ref, wo2_ref, g_ref, wg_ref, bg_ref, wr_ref, br_ref,
                x1_ref, h2_ref, rt_ref, *, nsteps, ngroups, nexperts):
    i = pl.program_id(0)

    @pl.when(i < nsteps)
    def _():
        _out_rows(om_ref, or_ref, x_ref, wo1_ref, wo2_ref, g_ref, wg_ref, bg_ref, wr_ref, br_ref,
                  x1_ref, h2_ref, rt_ref, ngroups=ngroups, nexperts=nexperts)

    @pl.when(i >= nsteps)
    def _():
        h2_ref[...] = jnp.zeros_like(h2_ref)


def _out_rows(om_ref, or_ref, x_ref, wo1_ref, wo2_ref, g_ref, wg_ref, bg_ref, wr_ref, br_ref,
              x1_ref, h2_ref, rt_ref, *, ngroups, nexperts):
    x1 = x_ref[...] + _dot(om_ref[...], wo1_ref[...]) + _dot(or_ref[...], wo2_ref[...])
    x1_ref[...] = x1
    hb = _rms(x1, g_ref[...]).astype(BF16)
    _store_chunked(h2_ref, hb.astype(F32), x1.shape[1] // LANES)
    tm = x1.shape[0]
    lane = lax.broadcasted_iota(jnp.int32, (tm, LANES), 1)
    gl = jnp.where(lane < ngroups, _dot(hb, wg_ref[...]) + bg_ref[...], NEG)
    gmax = jnp.max(gl, axis=1, keepdims=True)
    gsum = jnp.sum(jnp.exp(gl - gmax), axis=1, keepdims=True)
    g_w = 1.0 / gsum
    g_idx = jnp.min(jnp.where(gl == gmax, lane, LANES), axis=1, keepdims=True)
    epg = nexperts // ngroups
    in_group = (lane >= g_idx * epg) & (lane < (g_idx + 1) * epg)
    el = jnp.where(in_group, _dot(hb, wr_ref[...]) + br_ref[...], NEG)
    emax = jnp.max(el, axis=1, keepdims=True)
    esum = jnp.sum(jnp.exp(el - emax), axis=1, keepdims=True)
    idx1 = jnp.min(jnp.where(el == emax, lane, LANES), axis=1, keepdims=True)
    el2 = jnp.where(lane == idx1, NEG, el)
    emax2 = jnp.max(el2, axis=1, keepdims=True)
    idx2 = jnp.min(jnp.where(el2 == emax2, lane, LANES), axis=1, keepdims=True)
    p1 = 1.0 / esum
    p2 = jnp.exp(emax2 - emax) / esum
    w1 = p1 / (p1 + p2) * g_w
    w2 = p2 / (p1 + p2) * g_w
    rt_ref[...] = jnp.where(lane == 0, w1, jnp.where(lane == 1, w2, jnp.where(
        lane == 2, idx1.astype(F32), jnp.where(lane == 3, idx2.astype(F32), 0.0))))


def _out_kernel_into(om_ref, or_ref, x_ref, wo1_ref, wo2_ref, g_ref, wg_ref, bg_ref, wr_ref, br_ref,
                     h2_prev_ref, x1_ref, h2_ref, rt_ref, **kw):
    del h2_prev_ref
    _out_kernel(om_ref, or_ref, x_ref, wo1_ref, wo2_ref, g_ref, wg_ref, bg_ref, wr_ref, br_ref,
                x1_ref, h2_ref, rt_ref, **kw)


def _out_proj(om, orr, x, wo1, wo2, g, wg, bg, wr, br, *, rows, tm, h2_rows, h2_row0, h2_prev,
              ngroups, nexperts):
    d = x.shape[1]
    n1, n2 = om.shape[1], orr.shape[1]
    nsteps = rows // tm
    ntail = (h2_rows - rows) // tm if h2_prev is None else 0
    row = lambda i: (jnp.minimum(i, nsteps - 1), 0)
    fix = lambda i: (0, 0)
    in_specs = [pl.BlockSpec((tm, n1), row), pl.BlockSpec((tm, n2), row), pl.BlockSpec((tm, d), row),
                pl.BlockSpec((n1, d), fix), pl.BlockSpec((n2, d), fix), pl.BlockSpec((1, d), fix),
                pl.BlockSpec((d, LANES), fix), pl.BlockSpec((1, LANES), fix),
                pl.BlockSpec((d, LANES), fix), pl.BlockSpec((1, LANES), fix)]
    args = [om, orr, x, wo1, wo2, g, wg, bg, wr, br]
    body, aliases = _out_kernel, {}
    if h2_prev is not None:
        in_specs.append(pl.BlockSpec(memory_space=pl.ANY))
        args.append(h2_prev)
        body, aliases = _out_kernel_into, {len(args) - 1: 1}
    return pl.pallas_call(
        functools.partial(body, nsteps=nsteps, ngroups=ngroups, nexperts=nexperts),
        grid=(nsteps + ntail,),
        in_specs=in_specs,
        out_specs=[pl.BlockSpec((tm, d), row),
                   pl.BlockSpec((tm * (d // LANES), LANES), lambda i: (h2_row0 // tm + i, 0)),
                   pl.BlockSpec((tm, LANES), row)],
        out_shape=[jax.ShapeDtypeStruct((rows, d), F32),
                   jax.ShapeDtypeStruct((h2_rows * (d // LANES), LANES), F32),
                   jax.ShapeDtypeStruct((rows, LANES), F32)],
        input_output_aliases=aliases,
        compiler_params=_cparams(("arbitrary",)),
        name="out_proj_router",
    )(*args)


NWBUF = 3
NXBUF = 3
NYBUF = 3


def _expert_tile(npair, nexperts):
    share = npair / nexperts
    return max(8, -(-int(math.ceil((share + 2.5 * math.sqrt(share)) / 2)) // 8) * 8)


def _moe_kernel(tt_ref, ms_ref, sp_ref, h2_hbm, wg_hbm, wu_hbm, wd_hbm, ys_hbm,
                xbuf, ybuf, wgb, wub, wdb, gsem, ssem, wsem, *, tm, ntok, nck, pitch, ntiles):
    i = pl.program_id(0)
    nused = ms_ref[0]
    nue = ms_ref[1]
    npair = 2 * ntok

    def hbm_row(ref, t):
        return ref.at[pl.ds(t * nck, nck), :]

    def buf_row(buf, slot, r):
        return buf.at[slot, pl.ds(r * pitch, nck), :]

    def weight_copies(q):
        e = ms_ref[2 + q]
        w = q % NWBUF
        return [pltpu.make_async_copy(src.at[e], dst.at[w], wsem.at[w])
                for src, dst in ((wg_hbm, wgb), (wu_hbm, wub), (wd_hbm, wdb))]

    def gather(tile, slot):
        src0 = tt_ref[0, tile]
        last = src0 + tt_ref[1, tile] - 1
        for r in range(tm):
            pair = sp_ref[jnp.minimum(src0 + r, last)]
            tok = pair - jnp.where(pair >= ntok, ntok, 0)
            pltpu.make_async_copy(hbm_row(h2_hbm, tok), buf_row(xbuf, slot, r), gsem.at[slot]).start(priority=1)

    def scatter(tile, slot, n):
        src0 = tt_ref[0, tile]
        for r in range(tm):
            pair = sp_ref[jnp.minimum(src0 + r, npair - 1)]
            dst = jnp.where(r < n, pair, npair + slot * tm + r)
            pltpu.make_async_copy(buf_row(ybuf, slot, r), hbm_row(ys_hbm, dst), ssem.at[slot]).start()

    def wait_tile(buf, sem, slot):
        pltpu.make_async_copy(h2_hbm.at[pl.ds(0, tm * nck), :], buf.at[slot, pl.ds(0, tm * nck), :],
                              sem.at[slot]).wait()

    @pl.when(i == 0)
    def _():
        for cp in weight_copies(0):
            cp.start()

        @pl.when(nue > 1)
        def _():
            for cp in weight_copies(1):
                cp.start()

        for t in range(NXBUF - 1):
            gather(jnp.minimum(t, nused - 1), t)
        ybuf[...] = jnp.zeros(ybuf.shape, F32)
        for s in range(NYBUF):
            scatter(0, s, 0)

    @pl.when(i < nused)
    def _():
        xs = i % NXBUF
        ys = i % NYBUF
        gather(jnp.minimum(i + NXBUF - 1, nused - 1), (i + NXBUF - 1) % NXBUF)
        wait_tile(xbuf, gsem, xs)
        q = tt_ref[2, i]

        @pl.when((i == 0) | (q != tt_ref[2, jnp.maximum(i - 1, 0)]))
        def _():
            for cp in weight_copies(q):
                cp.wait()

            @pl.when(q + 2 < nue)
            def _():
                for cp in weight_copies(q + 2):
                    cp.start()

        w = q % NWBUF
        x = _load_chunked(xbuf.at[xs], tm, nck, pitch)
        a = jax.nn.silu(_dot(x, wgb[w])) * _dot(x, wub[w])
        wait_tile(ybuf, ssem, ys)
        _store_chunked(ybuf.at[ys], _dot(a, wdb[w]), pitch)
        scatter(i, ys, tt_ref[1, i])

    @pl.when(i == ntiles - 1)
    def _():
        for t in range(1, NXBUF):
            wait_tile(xbuf, gsem, (nused - 1 + t) % NXBUF)
        for s in range(NYBUF):
            wait_tile(ybuf, ssem, s)


def _moe(tile_tab, misc, sorted_pairs, h2, w_gate, w_up, w_down, *, tm):
    ntiles = tile_tab.shape[1]
    ntok = sorted_pairs.shape[0] // 2
    d, f = w_gate.shape[1:]
    nck = d // LANES
    pitch = nck + 1
    anyspec = pl.BlockSpec(memory_space=pl.ANY)
    grid_spec = pltpu.PrefetchScalarGridSpec(
        num_scalar_prefetch=3,
        grid=(ntiles,),
        in_specs=[anyspec, anyspec, anyspec, anyspec],
        out_specs=anyspec,
        scratch_shapes=[pltpu.VMEM((NXBUF, tm * pitch, LANES), F32), pltpu.VMEM((NYBUF, tm * pitch, LANES), F32),
                        pltpu.VMEM((NWBUF, d, f), F32), pltpu.VMEM((NWBUF, d, f), F32),
                        pltpu.VMEM((NWBUF, f, d), F32),
                        pltpu.SemaphoreType.DMA((NXBUF,)), pltpu.SemaphoreType.DMA((NYBUF,)),
                        pltpu.SemaphoreType.DMA((NWBUF,))],
    )
    return pl.pallas_call(
        functools.partial(_moe_kernel, tm=tm, ntok=ntok, nck=nck, pitch=pitch, ntiles=ntiles),
        grid_spec=grid_spec,
        out_shape=jax.ShapeDtypeStruct(((2 * ntok + NYBUF * tm) * nck, LANES), F32),
        compiler_params=_cparams(("arbitrary",)),
        name="moe_experts",
    )(tile_tab, misc, sorted_pairs, h2, w_gate, w_up, w_down)


def _comb_kernel(x1_ref, ya_ref, yb_ref, rt_ref, g_ref, y_ref):
    tm, d = x1_ref.shape
    nck = d // LANES
    rt = rt_ref[...]
    x = (x1_ref[...] + rt[:, 0:1] * _load_chunked(ya_ref, tm, nck, nck)
         + rt[:, 1:2] * _load_chunked(yb_ref, tm, nck, nck))
    y_ref[...] = _rms(x, g_ref[...])


def _combine(x1, ys, rt, g, *, tm, row0, ntok, rows):
    d = x1.shape[1]
    nck = d // LANES
    return pl.pallas_call(
        _comb_kernel,
        grid=(rows // tm,),
        in_specs=[pl.BlockSpec((tm, d), lambda i: (i, 0)),
                  pl.BlockSpec((tm * nck, LANES), lambda i: (row0 // tm + i, 0)),
                  pl.BlockSpec((tm * nck, LANES), lambda i: ((ntok + row0) // tm + i, 0)),
                  pl.BlockSpec((tm, LANES), lambda i: (i, 0)), pl.BlockSpec((1, d), lambda i: (0, 0))],
        out_specs=pl.BlockSpec((tm, d), lambda i: (i, 0)),
        out_shape=jax.ShapeDtypeStruct((rows, d), F32),
        compiler_params=_cparams(("parallel",)),
        name="moe_combine_norm",
    )(x1, ys, ys, rt, g)


def _rope_tables(pos, dim):
    half = dim // 2
    inv = jnp.exp(jnp.arange(half, dtype=F32) * (-2.0 * math.log(ROPE_BASE) / dim))
    ang = pos.astype(F32)[:, None] * inv[None, :]
    return jnp.cos(ang), jnp.sin(ang)


def _rot_cols(w):
    half = w.shape[-1] // 2
    return jnp.concatenate([-w[..., half:], w[..., :half]], axis=-1)


def _pick_tile(rows, target):
    t = min(rows, target)
    while rows % t:
        t //= 2
    return t


def _dispatch(route, ntok, nexperts, tm, ntiles):
    i32 = jnp.int32
    ids = jnp.concatenate([route[:, 2], route[:, 3]]).astype(i32)
    npair = 2 * ntok
    _, sorted_pairs = lax.sort((ids, jnp.arange(npair, dtype=i32)), num_keys=1, is_stable=True)
    ex = jnp.arange(nexperts, dtype=i32)
    counts = jnp.sum((ids[:, None] == ex[None, :]).astype(i32), axis=0)
    starts = jnp.cumsum(counts) - counts
    tcount = (counts + tm - 1) // tm
    tend = jnp.cumsum(tcount)
    tstart = tend - tcount
    nused = tend[-1]
    t = jnp.arange(ntiles, dtype=i32)
    tile_expert = jnp.minimum(jnp.sum((tend[None, :] <= t[:, None]).astype(i32), axis=1), nexperts - 1)
    pick = lambda tab: jnp.sum(jnp.where(tile_expert[:, None] == ex[None, :], tab[None, :], 0), axis=1)
    local = t - pick(tstart)
    tile_src0 = jnp.clip(pick(starts) + local * tm, 0, npair - 1)
    tile_rows = jnp.where(t < nused, jnp.clip(pick(counts) - local * tm, 0, tm), 0)
    used = counts > 0
    tile_q = jnp.sum((used[None, :] & (ex[None, :] < tile_expert[:, None])).astype(i32), axis=1)
    used_ids = jnp.sort(jnp.where(used, ex, nexperts))
    tile_tab = jnp.stack([tile_src0, tile_rows, tile_q]).astype(i32)
    misc = jnp.concatenate([nused.reshape(1), jnp.sum(used.astype(i32)).reshape(1), used_ids]).astype(i32)
    return tile_tab, misc, sorted_pairs


def kernel(x_prompt, x_sample, cache_kv_latent, cache_k_rope, state_retention, page_table, meta_tokens,
           ln_mix_g, w_in, q_norm_g, w_uq, w_uk, kv_norm_g, w_uv, mla_out_g, ret_gn_g, w_o, ln_ffn_g,
           w_group, b_group, w_router, b_router, w_gate, w_up, w_down, final_g):
    batch, seq, d = x_prompt.shape
    db, dec_seq, _ = x_sample.shape
    depth = w_in.shape[0]
    assert depth == 1 and dec_seq == 1
    ql = q_norm_g.shape[1]
    kvl = kv_norm_g.shape[1]
    mh, nr = w_uq.shape[2], w_uq.shape[3]
    nope = w_uk.shape[3]
    rope = nr - nope
    vdim = w_uv.shape[3]
    rdim = ret_gn_g.shape[1]
    dk = state_retention.shape[3]
    rh = rdim // dk
    ngroups = w_group.shape[2]
    nexperts = w_router.shape[2]
    page = cache_kv_latent.shape[2]
    npages = page_table.shape[1]
    past_len = npages * page
    chunk = min(2 * ROW_TILE, seq)
    tk = min(4 * ROW_TILE, seq)
    nmain = batch * seq
    naux = -(-(db + N_META) // LANES) * LANES
    ts = -(-db // ROW_TILE) * ROW_TILE
    assert seq % 256 == 0 and seq % tk == 0 and ts <= naux and db % 8 == 0 and rh == mh

    wit = jnp.swapaxes(w_in[0], 0, 1)
    o_kr = ql + kvl
    o_ret = o_kr + rope
    w_a = jnp.concatenate([wit[:o_ret], jnp.swapaxes(_rot_cols(jnp.swapaxes(wit[o_kr:o_ret], 0, 1)), 0, 1)],
                          axis=0).astype(BF16)
    w_b = wit[o_ret:].astype(BF16)
    wq = w_uq[0]
    wq_rope = wq[:, :, nope:]
    wuq = jnp.concatenate([wq[:, :, :nope].reshape(ql, mh * nope), wq_rope.reshape(ql, mh * rope),
                           _rot_cols(wq_rope).reshape(ql, mh * rope)], axis=1).astype(BF16)
    wukt = jnp.transpose(w_uk[0], (1, 2, 0)).astype(BF16)
    wuv = jnp.transpose(w_uv[0], (1, 0, 2)).astype(BF16)
    wo = w_o[0].astype(BF16)
    wo1, wo2 = wo[:mh * vdim], wo[mh * vdim:]
    wg_pad = jnp.zeros((d, LANES), F32).at[:, :ngroups].set(w_group[0]).astype(BF16)
    bg_pad = jnp.zeros((1, LANES), F32).at[0, :ngroups].set(b_group[0])
    wr_pad = jnp.zeros((d, LANES), F32).at[:, :nexperts].set(w_router[0]).astype(BF16)
    br_pad = jnp.zeros((1, LANES), F32).at[0, :nexperts].set(b_router[0])

    pos_main = N_META + jnp.arange(seq, dtype=jnp.int32)
    pos_aux = jnp.concatenate([jnp.full((db,), past_len, jnp.int32), jnp.arange(N_META, dtype=jnp.int32),
                               jnp.zeros((naux - db - N_META,), jnp.int32)])

    def tables(pos):
        c64, s64 = _rope_tables(pos, rope)
        c128, s128 = _rope_tables(pos, dk)
        cs = jnp.concatenate([c64, c64], axis=1)
        sn = jnp.concatenate([s64, s64], axis=1)
        return (cs, sn, jnp.tile(cs, (1, 2)), jnp.tile(sn, (1, 2)),
                jnp.concatenate([c128, c128], axis=1), jnp.concatenate([-s128, s128], axis=1))

    log_g = jnp.log1p(-jnp.exp2(-5.0 - jnp.arange(rh, dtype=F32)))
    n = jnp.arange(chunk, dtype=F32)
    diff = n[:, None] - n[None, :]
    t_dec = jnp.where(diff >= 0, jnp.exp(log_g[:, None, None] * jnp.maximum(diff, 0.0)), 0.0)
    t_qd = jnp.broadcast_to(jnp.exp((n[None, :] + 1.0) * log_g[:, None])[:, :, None], (rh, chunk, dk))
    k_decay = jnp.exp((chunk - 1.0 - n)[None, :] * log_g[:, None])
    t_kd = jnp.broadcast_to(k_decay[:, :, None], (rh, chunk, dk))
    t_kdm = jnp.broadcast_to(k_decay[:, chunk - N_META:, None], (rh, N_META, dk))
    t_gc = jnp.broadcast_to(jnp.exp(chunk * log_g)[:, None, None], (rh, dk, dk))
    t_gam = jnp.broadcast_to(jnp.exp(log_g)[:, None, None], (rh, 8, dk))

    x_main = x_prompt.reshape(nmain, d)
    x_aux = jnp.concatenate([x_sample.reshape(db, d), meta_tokens.astype(F32),
                             jnp.zeros((naux - db - N_META, d), F32)], axis=0)
    g_mix = ln_mix_g[0][None]
    scale = float(nope + rope) ** -0.5
    tm_main = _pick_tile(seq, 512)
    tm_aux = _pick_tile(naux, 512)

    def token_front(x, pos, tm):
        cs, sn, cs8, sn8, cos_r, sin_r = tables(pos)
        cq, kv, kvb, kr, krb = _in_a(x, g_mix, w_a, q_norm_g, kv_norm_g, cs, sn,
                                     tm=tm, ql=ql, kvl=kvl, rope=rope)
        rb = _in_b(x, g_mix, w_b, cos_r, sin_r, tm=tm, heads=rh, dk=dk)
        q_lat, q_rope = _q_proj(cq, wuq, wukt, cs8, sn8, tm=tm, heads=mh, nope=nope, rope=rope, scale=scale)
        return kv, kvb, kr, krb, rb, q_lat, q_rope

    kv_m, kvb_m, kr_m, krb_m, rb_m, ql_m, qr_m = token_front(x_main, pos_main, tm_main)
    kv_a, kvb_a, kr_a, krb_a, rb_a, ql_a, qr_a = token_front(x_aux, pos_aux, tm_aux)
    meta = slice(db, db + N_META)

    om_m = _attention(ql_m, qr_m, kvb_m, krb_m, kvb_a[meta], krb_a[meta], wuv, mla_out_g,
                      batch=batch, seq=seq, tq=ROW_TILE, tk=tk)
    rdk = rh * dk
    or_m, st_p = _retention_prompt(rb_m, rb_a[meta, rdk:2 * rdk], rb_a[meta, 2 * rdk:3 * rdk],
                                   (t_dec, t_qd, t_kd, t_kdm, t_gc), ret_gn_g,
                                   batch=batch, seq=seq, heads=rh, dk=dk, chunk=chunk)

    ql_s = jnp.transpose(ql_a[:, :db], (1, 0, 2))
    qr_s = jnp.transpose(qr_a[:, :db], (1, 0, 2))
    o_s = _decode_attention(page_table, ql_s, qr_s, kv_a[:db, None, :], kr_a[:db, None, :],
                            cache_kv_latent[0], jnp.swapaxes(cache_k_rope[0], 1, 2),
                            gpages=min(16, npages), nbuf=3)
    om_s = _decode_out(jnp.transpose(o_s, (1, 0, 2)), wuv, mla_out_g)
    rs = rb_a[:db].astype(F32)
    or_s, st_s = _retention_sample(rs[:, :rdk], rs[:, rdk:2 * rdk], rs[:, 2 * rdk:3 * rdk], rs[:, 3 * rdk:],
                                   t_gam, ret_gn_g, state_retention, heads=rh, dk=dk, bs=8)
    pad_rows = lambda a: jnp.concatenate([a, jnp.zeros((ts - db, a.shape[1]), a.dtype)], axis=0)
    om_a, or_a = pad_rows(om_s), pad_rows(or_s)

    g_ffn = ln_ffn_g[0][None]
    ntok = nmain + ts
    tm_o = _pick_tile(nmain, 256)
    outp = functools.partial(_out_proj, h2_rows=nmain + -(-ts // tm_o) * tm_o, ngroups=ngroups, nexperts=nexperts)
    x1_m, h2, rt_m = outp(om_m, or_m, x_main, wo1, wo2, g_ffn, wg_pad, bg_pad, wr_pad, br_pad,
                          rows=nmain, tm=tm_o, h2_row0=0, h2_prev=None)
    x1_a, h2, rt_a = outp(om_a, or_a, x_aux, wo1, wo2, g_ffn, wg_pad, bg_pad, wr_pad, br_pad,
                          rows=ts, tm=ROW_TILE, h2_row0=nmain, h2_prev=h2)

    route = jnp.concatenate([rt_m[:, :4], rt_a[:, :4]], axis=0)
    tm_e = _expert_tile(2 * ntok, nexperts)
    ntiles =-(-(2 * ntok + nexperts * (tm_e - 1)) // tm_e)
    ys = _moe(*_dispatch(route, ntok, nexperts, tm_e, ntiles), h2, w_gate[0], w_up[0], w_down[0], tm=tm_e)
    fg = final_g[None]
    y_m = _combine(x1_m, ys, rt_m, fg, tm=ROW_TILE, row0=0, ntok=ntok, rows=nmain)
    y_a = _combine(x1_a, ys, rt_a, fg, tm=ROW_TILE, row0=nmain, ntok=ntok, rows=ts)

    def with_meta(a_main, a_aux):
        w = a_main.shape[1]
        m = jnp.broadcast_to(a_aux[meta][None], (batch, N_META, w))
        return jnp.concatenate([m, a_main.reshape(batch, seq, w)], axis=1)[None]

    return (y_m.reshape(batch, seq, d), y_a[:db].reshape(db, 1, d),
            with_meta(kv_m, kv_a), with_meta(kr_m, kr_a), st_p,
            kv_a[:db].reshape(1, db, 1, kvl), kr_a[:db].reshape(1, db, 1, rope), st_s)
```

```python
import functools
import math

import jax
import jax.numpy as jnp
from jax import lax
from jax.experimental import pallas as pl
from jax.experimental.pallas import tpu as pltpu

F32 = jnp.float32
BF16 = jnp.bfloat16

N_META = 16
ROPE_BASE = 10000.0
EPS = 1e-6
NEG = -1e30
EXPERTS_PER_GROUP = 8
LANES = 128
ROW_TILE = 128
VMEM_LIMIT = 56 * 1024 * 1024


def _cparams(sem, vmem=VMEM_LIMIT):
    return pltpu.CompilerParams(dimension_semantics=sem, vmem_limit_bytes=vmem)


def _rms(x, g):
    return x * lax.rsqrt(jnp.mean(x * x, axis=-1, keepdims=True) + EPS) * g


def _dot(a, b):
    return jnp.dot(a, b, preferred_element_type=F32)


def _dot_nt(a, b):
    return lax.dot_general(a, b, (((1,), (1,)), ((), ())), preferred_element_type=F32)


def _dot_tn(a, b):
    return lax.dot_general(a, b, (((0,), (0,)), ((), ())), preferred_element_type=F32)


def _store_chunked(ref, val, pitch, base=0):
    rows, d = val.shape
    for c in range(d // LANES):
        ref[pl.ds(base + c, rows, stride=pitch), :] = val[:, c * LANES:(c + 1) * LANES]


def _load_chunked(ref, rows, nchunk, pitch, base=0):
    return jnp.concatenate([ref[pl.ds(base + c, rows, stride=pitch), :] for c in range(nchunk)], axis=1)


def _in_a_kernel(x_ref, g_ref, w_ref, qg_ref, kvg_ref, cs_ref, sn_ref,
                 cq_ref, kv_ref, kvb_ref, kr_ref, krb_ref, *, ql, kvl, rope):
    h = _rms(x_ref[...], g_ref[...]).astype(BF16)
    c = _dot_nt(h, w_ref[...])
    cq_ref[...] = _rms(c[:, :ql], qg_ref[...]).astype(BF16)
    kv = _rms(c[:, ql:ql + kvl], kvg_ref[...])
    kv_ref[...] = kv
    kvb_ref[...] = kv.astype(BF16)
    a = c[:, ql + kvl:ql + kvl + rope]
    b = c[:, ql + kvl + rope:]
    kr = a * cs_ref[...] + b * sn_ref[...]
    kr_ref[...] = kr
    krb_ref[...] = kr.astype(BF16)


def _in_a(x, g, w_a, qg, kvg, cs, sn, *, tm, ql, kvl, rope):
    rows, d = x.shape
    n = w_a.shape[0]
    row = lambda i: (i, 0)
    fix = lambda i: (0, 0)
    tab = lambda i: (i % (cs.shape[0] // tm), 0)
    return pl.pallas_call(
        functools.partial(_in_a_kernel, ql=ql, kvl=kvl, rope=rope),
        grid=(rows // tm,),
        in_specs=[pl.BlockSpec((tm, d), row), pl.BlockSpec((1, d), fix), pl.BlockSpec((n, d), fix),
                  pl.BlockSpec((1, ql), fix), pl.BlockSpec((1, kvl), fix),
                  pl.BlockSpec((tm, rope), tab), pl.BlockSpec((tm, rope), tab)],
        out_specs=[pl.BlockSpec((tm, ql), row), pl.BlockSpec((tm, kvl), row), pl.BlockSpec((tm, kvl), row),
                   pl.BlockSpec((tm, rope), row), pl.BlockSpec((tm, rope), row)],
        out_shape=[jax.ShapeDtypeStruct((rows, ql), BF16), jax.ShapeDtypeStruct((rows, kvl), F32),
                   jax.ShapeDtypeStruct((rows, kvl), BF16), jax.ShapeDtypeStruct((rows, rope), F32),
                   jax.ShapeDtypeStruct((rows, rope), BF16)],
        compiler_params=_cparams(("parallel",)),
        name="in_proj_a",
    )(x, g, w_a, qg, kvg, cs, sn)


def _in_b_kernel(x_ref, g_ref, w_ref, cos_ref, sin_ref, out_ref, h_scr, *, heads, dk):
    j = pl.program_id(1)

    @pl.when(j == 0)
    def _():
        h_scr[...] = _rms(x_ref[...], g_ref[...]).astype(BF16)

    @pl.when(j < 2)
    def _():
        c = _dot_nt(h_scr[...], w_ref[...])
        scale = jnp.where(j == 1, dk ** -0.5, 1.0).astype(F32)
        cos = cos_ref[...]
        sin = sin_ref[...]
        for hh in range(heads):
            blk = c[:, hh * dk:(hh + 1) * dk]
            r = blk * cos + pltpu.roll(blk, dk // 2, 1) * sin
            out_ref[:, hh * dk:(hh + 1) * dk] = (r * scale).astype(BF16)

    @pl.when(j >= 2)
    def _():
        out_ref[...] = _dot_nt(h_scr[...], w_ref[...]).astype(BF16)


def _in_b(x, g, w_b, cos, sin, *, tm, heads, dk):
    rows, d = x.shape
    n = heads * dk
    tab = lambda i, j: (i % (cos.shape[0] // tm), 0)
    return pl.pallas_call(
        functools.partial(_in_b_kernel, heads=heads, dk=dk),
        grid=(rows // tm, 4),
        in_specs=[pl.BlockSpec((tm, d), lambda i, j: (i, 0)), pl.BlockSpec((1, d), lambda i, j: (0, 0)),
                  pl.BlockSpec((n, d), lambda i, j: (j, 0)),
                  pl.BlockSpec((tm, dk), tab), pl.BlockSpec((tm, dk), tab)],
        out_specs=pl.BlockSpec((tm, n), lambda i, j: (i, j)),
        out_shape=jax.ShapeDtypeStruct((rows, 4 * n), BF16),
        scratch_shapes=[pltpu.VMEM((tm, d), BF16)],
        compiler_params=_cparams(("parallel", "arbitrary")),
        name="in_proj_b",
    )(x, g, w_b, cos, sin)


def _q_kernel(cq_ref, wuq_ref, wukt_ref, cos_ref, sin_ref, ql_ref, qr_ref, *, heads, nope, rope, scale):
    qf = _dot(cq_ref[...], wuq_ref[...])
    o1 = heads * nope
    o2 = o1 + heads * rope
    cos = cos_ref[...]
    sin = sin_ref[...]
    for hp in range(heads // 2):
        lo, hi = hp * 2 * rope, (hp + 1) * 2 * rope
        qr = (qf[:, o1 + lo:o1 + hi] * cos + qf[:, o2 + lo:o2 + hi] * sin) * scale
        qr_ref[2 * hp] = qr[:, :rope].astype(BF16)
        qr_ref[2 * hp + 1] = qr[:, rope:].astype(BF16)
    for hh in range(heads):
        qn = qf[:, hh * nope:(hh + 1) * nope].astype(BF16)
        ql_ref[hh] = (_dot(qn, wukt_ref[hh]) * scale).astype(BF16)


def _q_proj(cq, wuq, wukt, cos, sin, *, tm, heads, nope, rope, scale):
    rows, ql = cq.shape
    kvl = wukt.shape[2]
    nq = wuq.shape[1]
    tab = lambda i: (i % (cos.shape[0] // tm), 0)
    return pl.pallas_call(
        functools.partial(_q_kernel, heads=heads, nope=nope, rope=rope, scale=scale),
        grid=(rows // tm,),
        in_specs=[pl.BlockSpec((tm, ql), lambda i: (i, 0)), pl.BlockSpec((ql, nq), lambda i: (0, 0)),
                  pl.BlockSpec((heads, nope, kvl), lambda i: (0, 0, 0)),
                  pl.BlockSpec((tm, 2 * rope), tab), pl.BlockSpec((tm, 2 * rope), tab)],
        out_specs=[pl.BlockSpec((heads, tm, kvl), lambda i: (0, i, 0)),
                   pl.BlockSpec((heads, tm, rope), lambda i: (0, i, 0))],
        out_shape=[jax.ShapeDtypeStruct((heads, rows, kvl), BF16),
                   jax.ShapeDtypeStruct((heads, rows, rope), BF16)],
        compiler_params=_cparams(("parallel",)),
        name="q_proj",
    )(cq, wuq, wukt, cos, sin)


def _uv_norm(o_heads, wuv_ref, g):
    parts = [_dot(o.astype(BF16), wuv_ref[hh]) for hh, o in enumerate(o_heads)]
    return _rms(jnp.concatenate(parts, axis=1), g).astype(BF16)


def _widen(x, n):
    if n <= LANES:
        return x[:, :n]
    return jnp.concatenate([x] * (n // LANES), axis=1)


def _att_kernel(qi_ref, kj_ref, ql_ref, qr_ref, kv_ref, kr_ref, kvm_ref, krm_ref, wuv_ref, g_ref, o_ref,
                m_scr, l_scr, acc_scr, *, heads, tq, tk):
    qi = qi_ref[pl.program_id(1)]
    kj = kj_ref[pl.program_id(1)]
    kvl = ql_ref.shape[2]
    rows = heads * tq
    ql = ql_ref[...].reshape(rows, kvl)
    qr = qr_ref[...].reshape(rows, qr_ref.shape[2])

    def update(s, kv, first):
        m_cur = jnp.max(s, axis=1, keepdims=True)
        if first:
            m_new = jnp.broadcast_to(m_cur, (rows, LANES))
        else:
            m_prev = m_scr[...]
            m_new = jnp.maximum(m_prev, m_cur)
        p = jnp.exp(s - _widen(m_new, s.shape[1]))
        p_sum = jnp.sum(p, axis=1, keepdims=True)
        pv = _dot(p.astype(BF16), kv)
        if first:
            l_scr[...] = jnp.broadcast_to(p_sum, (rows, LANES))
            acc_scr[...] = pv
        else:
            alpha = jnp.exp(m_prev - m_new)
            l_scr[...] = alpha * l_scr[...] + p_sum
            acc_scr[...] = acc_scr[...] * _widen(alpha, kvl) + pv
        m_scr[...] = m_new

    @pl.when(kj == 0)
    def _():
        kvm = kvm_ref[...]
        update(_dot_nt(ql, kvm) + _dot_nt(qr, krm_ref[...]), kvm, True)

    last = (qi * tq + tq - 1) // tk

    @pl.when(kj < last)
    def _():
        kv = kv_ref[...]
        update(_dot_nt(ql, kv) + _dot_nt(qr, kr_ref[...]), kv, False)

    @pl.when(kj == last)
    def _():
        kv = kv_ref[...]
        s = _dot_nt(ql, kv) + _dot_nt(qr, kr_ref[...])
        qpos = qi * tq + lax.broadcasted_iota(jnp.int32, (rows, tk), 0) % tq
        kpos = kj * tk + lax.broadcasted_iota(jnp.int32, (rows, tk), 1)
        update(jnp.where(kpos <= qpos, s, NEG), kv, False)
        o = acc_scr[...] / _widen(l_scr[...], kvl)
        o_ref[...] = _uv_norm([o[hh * tq:(hh + 1) * tq] for hh in range(heads)], wuv_ref, g_ref[...])


def _attention(ql, qr, kvb, krb, kvm, krm, wuv, g, *, batch, seq, tq, tk):
    heads, rows, kvl = ql.shape
    rope = qr.shape[2]
    vdim = wuv.shape[2]
    nq, nk = seq // tq, seq // tk
    nmeta = kvm.shape[0]
    pairs = [(qi, kj) for qi in range(nq) for kj in range((qi * tq + tq - 1) // tk + 1)]
    qi_tab = jnp.asarray([p[0] for p in pairs], jnp.int32)
    kj_tab = jnp.asarray([p[1] for p in pairs], jnp.int32)
    qidx = lambda b, s, qt, kt: (0, b * nq + qt[s], 0)
    kidx = lambda b, s, qt, kt: (b * nk + kt[s], 0)
    fix2 = lambda b, s, qt, kt: (0, 0)
    grid_spec = pltpu.PrefetchScalarGridSpec(
        num_scalar_prefetch=2,
        grid=(batch, len(pairs)),
        in_specs=[pl.BlockSpec((heads, tq, kvl), qidx), pl.BlockSpec((heads, tq, rope), qidx),
                  pl.BlockSpec((tk, kvl), kidx), pl.BlockSpec((tk, rope), kidx),
                  pl.BlockSpec((nmeta, kvl), fix2), pl.BlockSpec((nmeta, rope), fix2),
                  pl.BlockSpec((heads, kvl, vdim), lambda b, s, qt, kt: (0, 0, 0)),
                  pl.BlockSpec((1, heads * vdim), fix2)],
        out_specs=pl.BlockSpec((tq, heads * vdim), lambda b, s, qt, kt: (b * nq + qt[s], 0)),
        scratch_shapes=[pltpu.VMEM((heads * tq, LANES), F32), pltpu.VMEM((heads * tq, LANES), F32),
                        pltpu.VMEM((heads * tq, kvl), F32)],
    )
    return pl.pallas_call(
        functools.partial(_att_kernel, heads=heads, tq=tq, tk=tk),
        grid_spec=grid_spec,
        out_shape=jax.ShapeDtypeStruct((rows, heads * vdim), BF16),
        compiler_params=_cparams(("parallel", "arbitrary")),
        name="mla_prompt_attention",
    )(qi_tab, kj_tab, ql, qr, kvb, krb, kvm, krm, wuv, g)


def _dec_kernel(pt_ref, ql_ref, qr_ref, kvn_ref, krn_ref, ckv_hbm, ckr_hbm, o_ref,
                kvbuf, krbuf, sem, *, nchunk, gpages, nbuf, total):
    b = pl.program_id(0)
    page = kvbuf.shape[2]
    kvl = kvbuf.shape[3]

    def copies(g, slot):
        bb = g // nchunk
        c0 = (g % nchunk) * gpages
        out = []
        for p in range(gpages):
            pg = pt_ref[bb, c0 + p]
            out.append(pltpu.make_async_copy(ckv_hbm.at[pg], kvbuf.at[slot, p], sem.at[slot]))
            out.append(pltpu.make_async_copy(ckr_hbm.at[pg], krbuf.at[slot, p], sem.at[slot]))
        return out

    def start(g, slot):
        for k, cp in enumerate(copies(g, slot)):
            cp.start(priority=(k // 2) % 2)

    @pl.when(b == 0)
    def _():
        for g0 in range(min(nbuf - 1, total)):
            start(g0, g0 % nbuf)

    ql = ql_ref[0].astype(F32)
    qr = qr_ref[0].astype(F32)
    kvn = kvn_ref[0]
    krn = krn_ref[0]
    heads = ql.shape[0]
    m0 = jnp.sum(ql * kvn, axis=1, keepdims=True) + jnp.sum(qr * krn, axis=1, keepdims=True)
    l0 = jnp.ones((heads, 1), F32)
    acc0 = jnp.broadcast_to(kvn, (heads, kvl))

    def body(c, carry):
        m, l, acc = carry
        g = b * nchunk + c
        slot = g % nbuf
        for cp in copies(g, slot):
            cp.wait()
        nxt = g + nbuf - 1

        @pl.when(nxt < total)
        def _():
            start(nxt, nxt % nbuf)

        kvc = kvbuf[slot].reshape(gpages * page, kvl)
        krt = jnp.concatenate([krbuf[slot, p] for p in range(gpages)], axis=1)
        s = _dot_nt(ql, kvc) + _dot(qr, krt)
        m_new = jnp.maximum(m, jnp.max(s, axis=1, keepdims=True))
        alpha = jnp.exp(m - m_new)
        p = jnp.exp(s - m_new)
        l = l * alpha + jnp.sum(p, axis=1, keepdims=True)
        acc = acc * alpha + _dot(p, kvc)
        return m_new, l, acc

    m, l, acc = lax.fori_loop(0, nchunk, body, (m0, l0, acc0))
    o_ref[0] = acc / l


def _decode_attention(page_table, ql_s, qr_s, kvn, krn, cache_kv, cache_krt, *, gpages, nbuf):
    db, npages = page_table.shape
    heads, kvl = ql_s.shape[1:]
    rope = qr_s.shape[2]
    page = cache_kv.shape[1]
    nchunk = npages // gpages
    total = db * nchunk
    grid_spec = pltpu.PrefetchScalarGridSpec(
        num_scalar_prefetch=1,
        grid=(db,),
        in_specs=[pl.BlockSpec((1, heads, kvl), lambda b, pt: (b, 0, 0)),
                  pl.BlockSpec((1, heads, rope), lambda b, pt: (b, 0, 0)),
                  pl.BlockSpec((1, 1, kvl), lambda b, pt: (b, 0, 0)),
                  pl.BlockSpec((1, 1, rope), lambda b, pt: (b, 0, 0)),
                  pl.BlockSpec(memory_space=pl.ANY), pl.BlockSpec(memory_space=pl.ANY)],
        out_specs=pl.BlockSpec((1, heads, kvl), lambda b, pt: (b, 0, 0)),
        scratch_shapes=[pltpu.VMEM((nbuf, gpages, page, kvl), F32),
                        pltpu.VMEM((nbuf, gpages, rope, page), F32),
                        pltpu.SemaphoreType.DMA((nbuf,))],
    )
    return pl.pallas_call(
        functools.partial(_dec_kernel, nchunk=nchunk, gpages=gpages, nbuf=nbuf, total=total),
        grid_spec=grid_spec,
        out_shape=jax.ShapeDtypeStruct((db, heads, kvl), F32),
        compiler_params=_cparams(("arbitrary",)),
        name="mla_decode_attention",
    )(page_table, ql_s, qr_s, kvn, krn, cache_kv, cache_krt)


def _dec_out_kernel(o_ref, wuv_ref, g_ref, out_ref):
    heads = o_ref.shape[0]
    out_ref[...] = _uv_norm([o_ref[hh] for hh in range(heads)], wuv_ref, g_ref[...])


def _decode_out(o_hm, wuv, g):
    heads, rows, kvl = o_hm.shape
    vdim = wuv.shape[2]
    return pl.pallas_call(
        _dec_out_kernel,
        out_shape=jax.ShapeDtypeStruct((rows, heads * vdim), BF16),
        name="mla_decode_out",
    )(o_hm, wuv, g)


def _head_norm_gate(o, gn, rg):
    mu = jnp.mean(o, axis=-1, keepdims=True)
    var = jnp.mean(jnp.square(o - mu), axis=-1, keepdims=True)
    y = (o - mu) * lax.rsqrt(var + EPS) * gn
    return y * jax.nn.silu(rg)


def _ret_kernel(rq_ref, rk_ref, rv_ref, rg_ref, rkm_ref, rvm_ref, dec_ref, qd_ref, kd_ref, kdm_ref, gc_ref,
                gn_ref, o_ref, st_ref, *, chunk, nchunks):
    dec = dec_ref[0]
    qd = qd_ref[0]
    kd = kd_ref[0]
    gc = gc_ref[0]
    gn = gn_ref[...]
    s0 = _dot_tn((rkm_ref[...].astype(F32) * kdm_ref[0]).astype(BF16), rvm_ref[...])

    def body(c, state):
        r0 = pl.multiple_of(c * chunk, chunk)
        q = rq_ref[pl.ds(r0, chunk), :]
        k = rk_ref[pl.ds(r0, chunk), :]
        v = rv_ref[pl.ds(r0, chunk), :]
        rg = rg_ref[pl.ds(r0, chunk), :].astype(F32)
        scores = _dot_nt(q, k) * dec
        o = _dot(scores.astype(BF16), v) + _dot(q, state.astype(BF16)) * qd
        new_state = state * gc + _dot_tn((k.astype(F32) * kd).astype(BF16), v)
        o_ref[pl.ds(r0, chunk), :] = _head_norm_gate(o, gn, rg).astype(BF16)
        return new_state

    st_ref[0, 0, 0] = lax.fori_loop(0, nchunks, body, s0, unroll=2 if nchunks % 2 == 0 else 1)


def _retention_prompt(rb, rkm, rvm, tabs, gn, *, batch, seq, heads, dk, chunk):
    dec, qd, kd, kdm, gc = tabs
    nmeta = rkm.shape[0]
    col = lambda off: (lambda b, h: (b, off * heads + h))
    tab = lambda b, h: (h, 0, 0)
    return pl.pallas_call(
        functools.partial(_ret_kernel, chunk=chunk, nchunks=seq // chunk),
        grid=(batch, heads),
        in_specs=[pl.BlockSpec((seq, dk), col(0)), pl.BlockSpec((seq, dk), col(1)),
                  pl.BlockSpec((seq, dk), col(2)), pl.BlockSpec((seq, dk), col(3)),
                  pl.BlockSpec((nmeta, dk), lambda b, h: (0, h)), pl.BlockSpec((nmeta, dk), lambda b, h: (0, h)),
                  pl.BlockSpec((1, chunk, chunk), tab), pl.BlockSpec((1, chunk, dk), tab),
                  pl.BlockSpec((1, chunk, dk), tab), pl.BlockSpec((1, nmeta, dk), tab),
                  pl.BlockSpec((1, dk, dk), tab),
                  pl.BlockSpec((1, dk), lambda b, h: (0, h))],
        out_specs=[pl.BlockSpec((seq, dk), lambda b, h: (b, h)),
                   pl.BlockSpec((1, 1, 1, dk, dk), lambda b, h: (0, b, h, 0, 0))],
        out_shape=[jax.ShapeDtypeStruct((batch * seq, heads * dk), BF16),
                   jax.ShapeDtypeStruct((1, batch, heads, dk, dk), F32)],
        compiler_params=_cparams(("parallel", "parallel")),
        name="retention_prompt",
    )(rb, rb, rb, rb, rkm, rvm, dec, qd, kd, kdm, gc, gn)


def _ret_s_kernel(q_ref, k_ref, v_ref, g_ref, gam_ref, gn_ref, st_ref, o_ref, ns_ref, *, heads, dk, bs):
    rowid = lax.broadcasted_iota(jnp.int32, (8, dk), 0)
    for hh in range(heads):
        sl = slice(hh * dk, (hh + 1) * dk)
        gam = gam_ref[hh]
        outs = []
        for i in range(bs):
            q = q_ref[i:i + 1, sl]
            k = k_ref[i:i + 1, sl]
            v = v_ref[i:i + 1, sl]
            state = st_ref[0, i, hh]
            k8 = jnp.where(rowid == 0, jnp.broadcast_to(k, (8, dk)), 0.0).astype(BF16)
            v8 = jnp.broadcast_to(v, (8, dk)).astype(BF16)
            ns_ref[0, i, hh] = state * gam[0:1, :] + _dot_tn(k8, v8)
            q8 = jnp.broadcast_to(q, (8, dk)).astype(BF16)
            qs = _dot(q8, state.astype(BF16))[0:1, :]
            qk = jnp.sum(q.astype(BF16).astype(F32) * k.astype(BF16).astype(F32), axis=1, keepdims=True)
            outs.append(qk * v.astype(BF16).astype(F32) + qs * gam[0:1, :])
        o = jnp.concatenate(outs, axis=0)
        o_ref[:, sl] = _head_norm_gate(o, gn_ref[:, sl], g_ref[:, sl]).astype(BF16)


def _retention_sample(rq, rk, rv, rg, gam, gn, state, *, heads, dk, bs):
    db = rq.shape[0]
    n = heads * dk
    row = lambda i: (i, 0)
    st_spec = pl.BlockSpec((1, bs, heads, dk, dk), lambda i: (0, i, 0, 0, 0))
    return pl.pallas_call(
        functools.partial(_ret_s_kernel, heads=heads, dk=dk, bs=bs),
        grid=(db // bs,),
        in_specs=[pl.BlockSpec((bs, n), row), pl.BlockSpec((bs, n), row), pl.BlockSpec((bs, n), row),
                  pl.BlockSpec((bs, n), row), pl.BlockSpec((heads, 8, dk), lambda i: (0, 0, 0)),
                  pl.BlockSpec((1, n), lambda i: (0, 0)), st_spec],
        out_specs=[pl.BlockSpec((bs, n), row), st_spec],
        out_shape=[jax.ShapeDtypeStruct((db, n), BF16), jax.ShapeDtypeStruct(state.shape, F32)],
        compiler_params=_cparams(("parallel",)),
        name="retention_sample",
    )(rq, rk, rv, rg, gam, gn, state)


def _out_kernel(om_ref, or_ref, x_ref, wo1_ref, wo2_ref, g_ref, wg_ref, bg_ref, wr_ref, br_ref,
                x1_ref, h2_ref, rt_ref, *, nsteps, ngroups, nexperts):
    i = pl.program_id(0)

    @pl.when(i < nsteps)
    def _():
        _out_rows(om_ref, or_ref, x_ref, wo1_ref, wo2_ref, g_ref, wg_ref, bg_ref, wr_ref, br_ref,
                  x1_ref, h2_ref, rt_ref, ngroups=ngroups, nexperts=nexperts)

    @pl.when(i >= nsteps)
    def _():
        h2_ref[...] = jnp.zeros_like(h2_ref)


def _out_rows(om_ref, or_ref, x_ref, wo1_ref, wo2_ref, g_ref, wg_ref, bg_ref, wr_ref, br_ref,
              x1_ref, h2_ref, rt_ref, *, ngroups, nexperts):
    x1 = x_ref[...] + _dot(om_ref[...], wo1_ref[...]) + _dot(or_ref[...], wo2_ref[...])
    x1_ref[...] = x1
    hb = _rms(x1, g_ref[...]).astype(BF16)
    _store_chunked(h2_ref, hb.astype(F32), x1.shape[1] // LANES)
    tm = x1.shape[0]
    lane = lax.broadcasted_iota(jnp.int32, (tm, LANES), 1)
    gl = jnp.where(lane < ngroups, _dot(hb, wg_ref[...]) + bg_ref[...], NEG)
    gmax = jnp.max(gl, axis=1, keepdims=True)
    gsum = jnp.sum(jnp.exp(gl - gmax), axis=1, keepdims=True)
    g_w = 1.0 / gsum
    g_idx = jnp.min(jnp.where(gl == gmax, lane, LANES), axis=1, keepdims=True)
    epg = nexperts // ngroups
    in_group = (lane >= g_idx * epg) & (lane < (g_idx + 1) * epg)
    el = jnp.where(in_group, _dot(hb, wr_ref[...]) + br_ref[...], NEG)
    emax = jnp.max(el, axis=1, keepdims=True)
    esum = jnp.sum(jnp.exp(el - emax), axis=1, keepdims=True)
    idx1 = jnp.min(jnp.where(el == emax, lane, LANES), axis=1, keepdims=True)
    el2 = jnp.where(lane == idx1, NEG, el)
    emax2 = jnp.max(el2, axis=1, keepdims=True)
    idx2 = jnp.min(jnp.where(el2 == emax2, lane, LANES), axis=1, keepdims=True)
    p1 = 1.0 / esum
    p2 = jnp.exp(emax2 - emax) / esum
    w1 = p1 / (p1 + p2) * g_w
    w2 = p2 / (p1 + p2) * g_w
    rt_ref[...] = jnp.where(lane == 0, w1, jnp.where(lane == 1, w2, jnp.where(
        lane == 2, idx1.astype(F32), jnp.where(lane == 3, idx2.astype(F32), 0.0))))


def _out_kernel_into(om_ref, or_ref, x_ref, wo1_ref, wo2_ref, g_ref, wg_ref, bg_ref, wr_ref, br_ref,
                     h2_prev_ref, x1_ref, h2_ref, rt_ref, **kw):
    del h2_prev_ref
    _out_kernel(om_ref, or_ref, x_ref, wo1_ref, wo2_ref, g_ref, wg_ref, bg_ref, wr_ref, br_ref,
                x1_ref, h2_ref, rt_ref, **kw)


def _out_proj(om, orr, x, wo1, wo2, g, wg, bg, wr, br, *, rows, tm, h2_rows, h2_row0, h2_prev,
              ngroups, nexperts):
    d = x.shape[1]
    n1, n2 = om.shape[1], orr.shape[1]
    nsteps = rows // tm
    ntail = (h2_rows - rows) // tm if h2_prev is None else 0
    row = lambda i: (jnp.minimum(i, nsteps - 1), 0)
    fix = lambda i: (0, 0)
    in_specs = [pl.BlockSpec((tm, n1), row), pl.BlockSpec((tm, n2), row), pl.BlockSpec((tm, d), row),
                pl.BlockSpec((n1, d), fix), pl.BlockSpec((n2, d), fix), pl.BlockSpec((1, d), fix),
                pl.BlockSpec((d, LANES), fix), pl.BlockSpec((1, LANES), fix),
                pl.BlockSpec((d, LANES), fix), pl.BlockSpec((1, LANES), fix)]
    args = [om, orr, x, wo1, wo2, g, wg, bg, wr, br]
    body, aliases = _out_kernel, {}
    if h2_prev is not None:
        in_specs.append(pl.BlockSpec(memory_space=pl.ANY))
        args.append(h2_prev)
        body, aliases = _out_kernel_into, {len(args) - 1: 1}
    return pl.pallas_call(
        functools.partial(body, nsteps=nsteps, ngroups=ngroups, nexperts=nexperts),
        grid=(nsteps + ntail,),
        in_specs=in_specs,
        out_specs=[pl.BlockSpec((tm, d), row),
                   pl.BlockSpec((tm * (d // LANES), LANES), lambda i: (h2_row0 // tm + i, 0)),
                   pl.BlockSpec((tm, LANES), row)],
        out_shape=[jax.ShapeDtypeStruct((rows, d), F32),
                   jax.ShapeDtypeStruct((h2_rows * (d // LANES), LANES), F32),
                   jax.ShapeDtypeStruct((rows, LANES), F32)],
        input_output_aliases=aliases,
        compiler_params=_cparams(("arbitrary",)),
        name="out_proj_router",
    )(*args)


NWBUF = 3
NXBUF = 3
NYBUF = 3


def _expert_tile(npair, nexperts):
    share = npair / nexperts
    return max(8, -(-int(math.ceil((share + 2.5 * math.sqrt(share)) / 2)) // 8) * 8)


def _moe_kernel(tt_ref, ms_ref, sp_ref, h2_hbm, wg_hbm, wu_hbm, wd_hbm, ys_hbm,
                xbuf, ybuf, wgb, wub, wdb, gsem, ssem, wsem, *, tm, ntok, nck, pitch, ntiles):
    i = pl.program_id(0)
    nused = ms_ref[0]
    nue = ms_ref[1]
    npair = 2 * ntok

    def hbm_row(ref, t):
        return ref.at[pl.ds(t * nck, nck), :]

    def buf_row(buf, slot, r):
        return buf.at[slot, pl.ds(r * pitch, nck), :]

    def weight_copies(q):
        e = ms_ref[2 + q]
        w = q % NWBUF
        return [pltpu.make_async_copy(src.at[e], dst.at[w], wsem.at[w])
                for src, dst in ((wg_hbm, wgb), (wu_hbm, wub), (wd_hbm, wdb))]

    def gather(tile, slot):
        src0 = tt_ref[0, tile]
        last = src0 + tt_ref[1, tile] - 1
        for r in range(tm):
            pair = sp_ref[jnp.minimum(src0 + r, last)]
            tok = pair - jnp.where(pair >= ntok, ntok, 0)
            pltpu.make_async_copy(hbm_row(h2_hbm, tok), buf_row(xbuf, slot, r), gsem.at[slot]).start(priority=1)

    def scatter(tile, slot, n):
        src0 = tt_ref[0, tile]
        for r in range(tm):
            pair = sp_ref[jnp.minimum(src0 + r, npair - 1)]
            dst = jnp.where(r < n, pair, npair + slot * tm + r)
            pltpu.make_async_copy(buf_row(ybuf, slot, r), hbm_row(ys_hbm, dst), ssem.at[slot]).start()

    def wait_tile(buf, sem, slot):
        pltpu.make_async_copy(h2_hbm.at[pl.ds(0, tm * nck), :], buf.at[slot, pl.ds(0, tm * nck), :],
                              sem.at[slot]).wait()

    @pl.when(i == 0)
    def _():
        for cp in weight_copies(0):
            cp.start()

        @pl.when(nue > 1)
        def _():
            for cp in weight_copies(1):
                cp.start()

        for t in range(NXBUF - 1):
            gather(jnp.minimum(t, nused - 1), t)
        ybuf[...] = jnp.zeros(ybuf.shape, F32)
        for s in range(NYBUF):
            scatter(0, s, 0)

    @pl.when(i < nused)
    def _():
        xs = i % NXBUF
        ys = i % NYBUF
        gather(jnp.minimum(i + NXBUF - 1, nused - 1), (i + NXBUF - 1) % NXBUF)
        wait_tile(xbuf, gsem, xs)
        q = tt_ref[2, i]

        @pl.when((i == 0) | (q != tt_ref[2, jnp.maximum(i - 1, 0)]))
        def _():
            for cp in weight_copies(q):
                cp.wait()

            @pl.when(q + 2 < nue)
            def _():
                for cp in weight_copies(q + 2):
                    cp.start()

        w = q % NWBUF
        x = _load_chunked(xbuf.at[xs], tm, nck, pitch)
        a = jax.nn.silu(_dot(x, wgb[w])) * _dot(x, wub[w])
        wait_tile(ybuf, ssem, ys)
        _store_chunked(ybuf.at[ys], _dot(a, wdb[w]), pitch)
        scatter(i, ys, tt_ref[1, i])

    @pl.when(i == ntiles - 1)
    def _():
        for t in range(1, NXBUF):
            wait_tile(xbuf, gsem, (nused - 1 + t) % NXBUF)
        for s in range(NYBUF):
            wait_tile(ybuf, ssem, s)


def _moe(tile_tab, misc, sorted_pairs, h2, w_gate, w_up, w_down, *, tm):
    ntiles = tile_tab.shape[1]
    ntok = sorted_pairs.shape[0] // 2
    d, f = w_gate.shape[1:]
    nck = d // LANES
    pitch = nck + 1
    anyspec = pl.BlockSpec(memory_space=pl.ANY)
    grid_spec = pltpu.PrefetchScalarGridSpec(
        num_scalar_prefetch=3,
        grid=(ntiles,),
        in_specs=[anyspec, anyspec, anyspec, anyspec],
        out_specs=anyspec,
        scratch_shapes=[pltpu.VMEM((NXBUF, tm * pitch, LANES), F32), pltpu.VMEM((NYBUF, tm * pitch, LANES), F32),
                        pltpu.VMEM((NWBUF, d, f), F32), pltpu.VMEM((NWBUF, d, f), F32),
                        pltpu.VMEM((NWBUF, f, d), F32),
                        pltpu.SemaphoreType.DMA((NXBUF,)), pltpu.SemaphoreType.DMA((NYBUF,)),
                        pltpu.SemaphoreType.DMA((NWBUF,))],
    )
    return pl.pallas_call(
        functools.partial(_moe_kernel, tm=tm, ntok=ntok, nck=nck, pitch=pitch, ntiles=ntiles),
        grid_spec=grid_spec,
        out_shape=jax.ShapeDtypeStruct(((2 * ntok + NYBUF * tm) * nck, LANES), F32),
        compiler_params=_cparams(("arbitrary",)),
        name="moe_experts",
    )(tile_tab, misc, sorted_pairs, h2, w_gate, w_up, w_down)


def _comb_kernel(x1_ref, ya_ref, yb_ref, rt_ref, g_ref, y_ref):
    tm, d = x1_ref.shape
    nck = d // LANES
    rt = rt_ref[...]
    x = (x1_ref[...] + rt[:, 0:1] * _load_chunked(ya_ref, tm, nck, nck)
         + rt[:, 1:2] * _load_chunked(yb_ref, tm, nck, nck))
    y_ref[...] = _rms(x, g_ref[...])


def _combine(x1, ys, rt, g, *, tm, row0, ntok, rows):
    d = x1.shape[1]
    nck = d // LANES
    return pl.pallas_call(
        _comb_kernel,
        grid=(rows // tm,),
        in_specs=[pl.BlockSpec((tm, d), lambda i: (i, 0)),
                  pl.BlockSpec((tm * nck, LANES), lambda i: (row0 // tm + i, 0)),
                  pl.BlockSpec((tm * nck, LANES), lambda i: ((ntok + row0) // tm + i, 0)),
                  pl.BlockSpec((tm, LANES), lambda i: (i, 0)), pl.BlockSpec((1, d), lambda i: (0, 0))],
        out_specs=pl.BlockSpec((tm, d), lambda i: (i, 0)),
        out_shape=jax.ShapeDtypeStruct((rows, d), F32),
        compiler_params=_cparams(("parallel",)),
        name="moe_combine_norm",
    )(x1, ys, ys, rt, g)


def _rope_tables(pos, dim):
    half = dim // 2
    inv = jnp.exp(jnp.arange(half, dtype=F32) * (-2.0 * math.log(ROPE_BASE) / dim))
    ang = pos.astype(F32)[:, None] * inv[None, :]
    return jnp.cos(ang), jnp.sin(ang)


def _rot_cols(w):
    half = w.shape[-1] // 2
    return jnp.concatenate([-w[..., half:], w[..., :half]], axis=-1)


def _pick_tile(rows, target):
    t = min(rows, target)
    while rows % t:
        t //= 2
    return t


def _dispatch(route, ntok, nexperts, tm, ntiles):
    i32 = jnp.int32
    ids = jnp.concatenate([route[:, 2], route[:, 3]]).astype(i32)
    npair = 2 * ntok
    _, sorted_pairs = lax.sort((ids, jnp.arange(npair, dtype=i32)), num_keys=1, is_stable=True)
    ex = jnp.arange(nexperts, dtype=i32)
    counts = jnp.sum((ids[:, None] == ex[None, :]).astype(i32), axis=0)
    starts = jnp.cumsum(counts) - counts
    tcount = (counts + tm - 1) // tm
    tend = jnp.cumsum(tcount)
    tstart = tend - tcount
    nused = tend[-1]
    t = jnp.arange(ntiles, dtype=i32)
    tile_expert = jnp.minimum(jnp.sum((tend[None, :] <= t[:, None]).astype(i32), axis=1), nexperts - 1)
    pick = lambda tab: jnp.sum(jnp.where(tile_expert[:, None] == ex[None, :], tab[None, :], 0), axis=1)
    local = t - pick(tstart)
    tile_src0 = jnp.clip(pick(starts) + local * tm, 0, npair - 1)
    tile_rows = jnp.where(t < nused, jnp.clip(pick(counts) - local * tm, 0, tm), 0)
    used = counts > 0
    tile_q = jnp.sum((used[None, :] & (ex[None, :] < tile_expert[:, None])).astype(i32), axis=1)
    used_ids = jnp.sort(jnp.where(used, ex, nexperts))
    tile_tab = jnp.stack([tile_src0, tile_rows, tile_q]).astype(i32)
    misc = jnp.concatenate([nused.reshape(1), jnp.sum(used.astype(i32)).reshape(1), used_ids]).astype(i32)
    return tile_tab, misc, sorted_pairs


def kernel(x_prompt, x_sample, cache_kv_latent, cache_k_rope, state_retention, page_table, meta_tokens,
           ln_mix_g, w_in, q_norm_g, w_uq, w_uk, kv_norm_g, w_uv, mla_out_g, ret_gn_g, w_o, ln_ffn_g,
           w_group, b_group, w_router, b_router, w_gate, w_up, w_down, final_g):
    batch, seq, d = x_prompt.shape
    db, dec_seq, _ = x_sample.shape
    depth = w_in.shape[0]
    assert depth == 1 and dec_seq == 1
    ql = q_norm_g.shape[1]
    kvl = kv_norm_g.shape[1]
    mh, nr = w_uq.shape[2], w_uq.shape[3]
    nope = w_uk.shape[3]
    rope = nr - nope
    vdim = w_uv.shape[3]
    rdim = ret_gn_g.shape[1]
    dk = state_retention.shape[3]
    rh = rdim // dk
    ngroups = w_group.shape[2]
    nexperts = w_router.shape[2]
    page = cache_kv_latent.shape[2]
    npages = page_table.shape[1]
    past_len = npages * page
    chunk = min(2 * ROW_TILE, seq)
    tk = min(4 * ROW_TILE, seq)
    nmain = batch * seq
    naux = -(-(db + N_META) // LANES) * LANES
    ts = -(-db // ROW_TILE) * ROW_TILE
    assert seq % 256 == 0 and seq % tk == 0 and ts <= naux and db % 8 == 0 and rh == mh

    wit = jnp.swapaxes(w_in[0], 0, 1)
    o_kr = ql + kvl
    o_ret = o_kr + rope
    w_a = jnp.concatenate([wit[:o_ret], jnp.swapaxes(_rot_cols(jnp.swapaxes(wit[o_kr:o_ret], 0, 1)), 0, 1)],
                          axis=0).astype(BF16)
    w_b = wit[o_ret:].astype(BF16)
    wq = w_uq[0]
    wq_rope = wq[:, :, nope:]
    wuq = jnp.concatenate([wq[:, :, :nope].reshape(ql, mh * nope), wq_rope.reshape(ql, mh * rope),
                           _rot_cols(wq_rope).reshape(ql, mh * rope)], axis=1).astype(BF16)
    wukt = jnp.transpose(w_uk[0], (1, 2, 0)).astype(BF16)
    wuv = jnp.transpose(w_uv[0], (1, 0, 2)).astype(BF16)
    wo = w_o[0].astype(BF16)
    wo1, wo2 = wo[:mh * vdim], wo[mh * vdim:]
    wg_pad = jnp.zeros((d, LANES), F32).at[:, :ngroups].set(w_group[0]).astype(BF16)
    bg_pad = jnp.zeros((1, LANES), F32).at[0, :ngroups].set(b_group[0])
    wr_pad = jnp.zeros((d, LANES), F32).at[:, :nexperts].set(w_router[0]).astype(BF16)
    br_pad = jnp.zeros((1, LANES), F32).at[0, :nexperts].set(b_router[0])

    pos_main = N_META + jnp.arange(seq, dtype=jnp.int32)
    pos_aux = jnp.concatenate([jnp.full((db,), past_len, jnp.int32), jnp.arange(N_META, dtype=jnp.int32),
                               jnp.zeros((naux - db - N_META,), jnp.int32)])

    def tables(pos):
        c64, s64 = _rope_tables(pos, rope)
        c128, s128 = _rope_tables(pos, dk)
        cs = jnp.concatenate([c64, c64], axis=1)
        sn = jnp.concatenate([s64, s64], axis=1)
        return (cs, sn, jnp.tile(cs, (1, 2)), jnp.tile(sn, (1, 2)),
                jnp.concatenate([c128, c128], axis=1), jnp.concatenate([-s128, s128], axis=1))

    log_g = jnp.log1p(-jnp.exp2(-5.0 - jnp.arange(rh, dtype=F32)))
    n = jnp.arange(chunk, dtype=F32)
    diff = n[:, None] - n[None, :]
    t_dec = jnp.where(diff >= 0, jnp.exp(log_g[:, None, None] * jnp.maximum(diff, 0.0)), 0.0)
    t_qd = jnp.broadcast_to(jnp.exp((n[None, :] + 1.0) * log_g[:, None])[:, :, None], (rh, chunk, dk))
    k_decay = jnp.exp((chunk - 1.0 - n)[None, :] * log_g[:, None])
    t_kd = jnp.broadcast_to(k_decay[:, :, None], (rh, chunk, dk))
    t_kdm = jnp.broadcast_to(k_decay[:, chunk - N_META:, None], (rh, N_META, dk))
    t_gc = jnp.broadcast_to(jnp.exp(chunk * log_g)[:, None, None], (rh, dk, dk))
    t_gam = jnp.broadcast_to(jnp.exp(log_g)[:, None, None], (rh, 8, dk))

    x_main = x_prompt.reshape(nmain, d)
    x_aux = jnp.concatenate([x_sample.reshape(db, d), meta_tokens.astype(F32),
                             jnp.zeros((naux - db - N_META, d), F32)], axis=0)
    g_mix = ln_mix_g[0][None]
    scale = float(nope + rope) ** -0.5
    tm_main = _pick_tile(seq, 512)
    tm_aux = _pick_tile(naux, 512)

    def token_front(x, pos, tm):
        cs, sn, cs8, sn8, cos_r, sin_r = tables(pos)
        cq, kv, kvb, kr, krb = _in_a(x, g_mix, w_a, q_norm_g, kv_norm_g, cs, sn,
                                     tm=tm, ql=ql, kvl=kvl, rope=rope)
        rb = _in_b(x, g_mix, w_b, cos_r, sin_r, tm=tm, heads=rh, dk=dk)
        q_lat, q_rope = _q_proj(cq, wuq, wukt, cs8, sn8, tm=tm, heads=mh, nope=nope, rope=rope, scale=scale)
        return kv, kvb, kr, krb, rb, q_lat, q_rope

    kv_m, kvb_m, kr_m, krb_m, rb_m, ql_m, qr_m = token_front(x_main, pos_main, tm_main)
    kv_a, kvb_a, kr_a, krb_a, rb_a, ql_a, qr_a = token_front(x_aux, pos_aux, tm_aux)
    meta = slice(db, db + N_META)

    om_m = _attention(ql_m, qr_m, kvb_m, krb_m, kvb_a[meta], krb_a[meta], wuv, mla_out_g,
                      batch=batch, seq=seq, tq=ROW_TILE, tk=tk)
    rdk = rh * dk
    or_m, st_p = _retention_prompt(rb_m, rb_a[meta, rdk:2 * rdk], rb_a[meta, 2 * rdk:3 * rdk],
                                   (t_dec, t_qd, t_kd, t_kdm, t_gc), ret_gn_g,
                                   batch=batch, seq=seq, heads=rh, dk=dk, chunk=chunk)

    ql_s = jnp.transpose(ql_a[:, :db], (1, 0, 2))
    qr_s = jnp.transpose(qr_a[:, :db], (1, 0, 2))
    o_s = _decode_attention(page_table, ql_s, qr_s, kv_a[:db, None, :], kr_a[:db, None, :],
                            cache_kv_latent[0], jnp.swapaxes(cache_k_rope[0], 1, 2),
                            gpages=min(16, npages), nbuf=3)
    om_s = _decode_out(jnp.transpose(o_s, (1, 0, 2)), wuv, mla_out_g)
    rs = rb_a[:db].astype(F32)
    or_s, st_s = _retention_sample(rs[:, :rdk], rs[:, rdk:2 * rdk], rs[:, 2 * rdk:3 * rdk], rs[:, 3 * rdk:],
                                   t_gam, ret_gn_g, state_retention, heads=rh, dk=dk, bs=8)
    pad_rows = lambda a: jnp.concatenate([a, jnp.zeros((ts - db, a.shape[1]), a.dtype)], axis=0)
    om_a, or_a = pad_rows(om_s), pad_rows(or_s)

    g_ffn = ln_ffn_g[0][None]
    ntok = nmain + ts
    tm_o = _pick_tile(nmain, 256)
    outp = functools.partial(_out_proj, h2_rows=nmain + -(-ts // tm_o) * tm_o, ngroups=ngroups, nexperts=nexperts)
    x1_m, h2, rt_m = outp(om_m, or_m, x_main, wo1, wo2, g_ffn, wg_pad, bg_pad, wr_pad, br_pad,
                          rows=nmain, tm=tm_o, h2_row0=0, h2_prev=None)
    x1_a, h2, rt_a = outp(om_a, or_a, x_aux, wo1, wo2, g_ffn, wg_pad, bg_pad, wr_pad, br_pad,
                          rows=ts, tm=ROW_TILE, h2_row0=nmain, h2_prev=h2)

    route = jnp.concatenate([rt_m[:, :4], rt_a[:, :4]], axis=0)
    tm_e = _expert_tile(2 * ntok, nexperts)
    ntiles =-(-(2 * ntok + nexperts * (tm_e - 1)) // tm_e)
    ys = _moe(*_dispatch(route, ntok, nexperts, tm_e, ntiles), h2, w_gate[0], w_up[0], w_down[0], tm=tm_e)
    fg = final_g[None]
    y_m = _combine(x1_m, ys, rt_m, fg, tm=ROW_TILE, row0=0, ntok=ntok, rows=nmain)
    y_a = _combine(x1_a, ys, rt_a, fg, tm=ROW_TILE, row0=nmain, ntok=ntok, rows=ts)

    def with_meta(a_main, a_aux):
        w = a_main.shape[1]
        m = jnp.broadcast_to(a_aux[meta][None], (batch, N_META, w))
        return jnp.concatenate([m, a_main.reshape(batch, seq, w)], axis=1)[None]

    return (y_m.reshape(batch, seq, d), y_a[:db].reshape(db, 1, d),
            with_meta(kv_m, kv_a), with_meta(kr_m, kr_a), st_p,
            kv_a[:db].reshape(1, db, 1, kvl), kr_a[:db].reshape(1, db, 1, rope), st_s)
```

```python
import functools
import math

import jax
import jax.numpy as jnp
from jax import lax
from jax.experimental import pallas as pl
from jax.experimental.pallas import tpu as pltpu

F32 = jnp.float32
BF16 = jnp.bfloat16

N_META = 16
ROPE_BASE = 10000.0
EPS = 1e-6
NEG = -1e30
EXPERTS_PER_GROUP = 8
LANES = 128
ROW_TILE = 128
VMEM_LIMIT = 56 * 1024 * 1024


def _cparams(sem, vmem=VMEM_LIMIT):
    return pltpu.CompilerParams(dimension_semantics=sem, vmem_limit_bytes=vmem)


def _rms(x, g):
    return x * lax.rsqrt(jnp.mean(x * x, axis=-1, keepdims=True) + EPS) * g


def _dot(a, b):
    return jnp.dot(a, b, preferred_element_type=F32)


def _dot_nt(a, b):
    return lax.dot_general(a, b, (((1,), (1,)), ((), ())), preferred_element_type=F32)


def _dot_tn(a, b):
    return lax.dot_general(a, b, (((0,), (0,)), ((), ())), preferred_element_type=F32)


def _store_chunked(ref, val, pitch, base=0):
    rows, d = val.shape
    for c in range(d // LANES):
        ref[pl.ds(base + c, rows, stride=pitch), :] = val[:, c * LANES:(c + 1) * LANES]


def _load_chunked(ref, rows, nchunk, pitch, base=0):
    return jnp.concatenate([ref[pl.ds(base + c, rows, stride=pitch), :] for c in range(nchunk)], axis=1)


def _in_a_kernel(x_ref, g_ref, w_ref, qg_ref, kvg_ref, cs_ref, sn_ref,
                 cq_ref, kv_ref, kvb_ref, kr_ref, krb_ref, *, ql, kvl, rope):
    h = _rms(x_ref[...], g_ref[...]).astype(BF16)
    c = _dot_nt(h, w_ref[...])
    cq_ref[...] = _rms(c[:, :ql], qg_ref[...]).astype(BF16)
    kv = _rms(c[:, ql:ql + kvl], kvg_ref[...])
    kv_ref[...] = kv
    kvb_ref[...] = kv.astype(BF16)
    a = c[:, ql + kvl:ql + kvl + rope]
    b = c[:, ql + kvl + rope:]
    kr = a * cs_ref[...] + b * sn_ref[...]
    kr_ref[...] = kr
    krb_ref[...] = kr.astype(BF16)


def _in_a(x, g, w_a, qg, kvg, cs, sn, *, tm, ql, kvl, rope):
    rows, d = x.shape
    n = w_a.shape[0]
    row = lambda i: (i, 0)
    fix = lambda i: (0, 0)
    tab = lambda i: (i % (cs.shape[0] // tm), 0)
    return pl.pallas_call(
        functools.partial(_in_a_kernel, ql=ql, kvl=kvl, rope=rope),
        grid=(rows // tm,),
        in_specs=[pl.BlockSpec((tm, d), row), pl.BlockSpec((1, d), fix), pl.BlockSpec((n, d), fix),
                  pl.BlockSpec((1, ql), fix), pl.BlockSpec((1, kvl), fix),
                  pl.BlockSpec((tm, rope), tab), pl.BlockSpec((tm, rope), tab)],
        out_specs=[pl.BlockSpec((tm, ql), row), pl.BlockSpec((tm, kvl), row), pl.BlockSpec((tm, kvl), row),
                   pl.BlockSpec((tm, rope), row), pl.BlockSpec((tm, rope), row)],
        out_shape=[jax.ShapeDtypeStruct((rows, ql), BF16), jax.ShapeDtypeStruct((rows, kvl), F32),
                   jax.ShapeDtypeStruct((rows, kvl), BF16), jax.ShapeDtypeStruct((rows, rope), F32),
                   jax.ShapeDtypeStruct((rows, rope), BF16)],
        compiler_params=_cparams(("parallel",)),
        name="in_proj_a",
    )(x, g, w_a, qg, kvg, cs, sn)


def _in_b_kernel(x_ref, g_ref, w_ref, cos_ref, sin_ref, out_ref, h_scr, *, heads, dk):
    j = pl.program_id(1)

    @pl.when(j == 0)
    def _():
        h_scr[...] = _rms(x_ref[...], g_ref[...]).astype(BF16)

    @pl.when(j < 2)
    def _():
        c = _dot_nt(h_scr[...], w_ref[...])
        scale = jnp.where(j == 1, dk ** -0.5, 1.0).astype(F32)
        cos = cos_ref[...]
        sin = sin_ref[...]
        for hh in range(heads):
            blk = c[:, hh * dk:(hh + 1) * dk]
            r = blk * cos + pltpu.roll(blk, dk // 2, 1) * sin
            out_ref[:, hh * dk:(hh + 1) * dk] = (r * scale).astype(BF16)

    @pl.when(j >= 2)
    def _():
        out_ref[...] = _dot_nt(h_scr[...], w_ref[...]).astype(BF16)


def _in_b(x, g, w_b, cos, sin, *, tm, heads, dk):
    rows, d = x.shape
    n = heads * dk
    tab = lambda i, j: (i % (cos.shape[0] // tm), 0)
    return pl.pallas_call(
        functools.partial(_in_b_kernel, heads=heads, dk=dk),
        grid=(rows // tm, 4),
        in_specs=[pl.BlockSpec((tm, d), lambda i, j: (i, 0)), pl.BlockSpec((1, d), lambda i, j: (0, 0)),
                  pl.BlockSpec((n, d), lambda i, j: (j, 0)),
                  pl.BlockSpec((tm, dk), tab), pl.BlockSpec((tm, dk), tab)],
        out_specs=pl.BlockSpec((tm, n), lambda i, j: (i, j)),
        out_shape=jax.ShapeDtypeStruct((rows, 4 * n), BF16),
        scratch_shapes=[pltpu.VMEM((tm, d), BF16)],
        compiler_params=_cparams(("parallel", "arbitrary")),
        name="in_proj_b",
    )(x, g, w_b, cos, sin)


def _q_kernel(cq_ref, wuq_ref, wukt_ref, cos_ref, sin_ref, ql_ref, qr_ref, *, heads, nope, rope, scale):
    qf = _dot(cq_ref[...], wuq_ref[...])
    o1 = heads * nope
    o2 = o1 + heads * rope
    cos = cos_ref[...]
    sin = sin_ref[...]
    for hp in range(heads // 2):
        lo, hi = hp * 2 * rope, (hp + 1) * 2 * rope
        qr = (qf[:, o1 + lo:o1 + hi] * cos + qf[:, o2 + lo:o2 + hi] * sin) * scale
        qr_ref[2 * hp] = qr[:, :rope].astype(BF16)
        qr_ref[2 * hp + 1] = qr[:, rope:].astype(BF16)
    for hh in range(heads):
        qn = qf[:, hh * nope:(hh + 1) * nope].astype(BF16)
        ql_ref[hh] = (_dot(qn, wukt_ref[hh]) * scale).astype(BF16)


def _q_proj(cq, wuq, wukt, cos, sin, *, tm, heads, nope, rope, scale):
    rows, ql = cq.shape
    kvl = wukt.shape[2]
    nq = wuq.shape[1]
    tab = lambda i: (i % (cos.shape[0] // tm), 0)
    return pl.pallas_call(
        functools.partial(_q_kernel, heads=heads, nope=nope, rope=rope, scale=scale),
        grid=(rows // tm,),
        in_specs=[pl.BlockSpec((tm, ql), lambda i: (i, 0)), pl.BlockSpec((ql, nq), lambda i: (0, 0)),
                  pl.BlockSpec((heads, nope, kvl), lambda i: (0, 0, 0)),
                  pl.BlockSpec((tm, 2 * rope), tab), pl.BlockSpec((tm, 2 * rope), tab)],
        out_specs=[pl.BlockSpec((heads, tm, kvl), lambda i: (0, i, 0)),
                   pl.BlockSpec((heads, tm, rope), lambda i: (0, i, 0))],
        out_shape=[jax.ShapeDtypeStruct((heads, rows, kvl), BF16),
                   jax.ShapeDtypeStruct((heads, rows, rope), BF16)],
        compiler_params=_cparams(("parallel",)),
        name="q_proj",
    )(cq, wuq, wukt, cos, sin)


def _uv_norm(o_heads, wuv_ref, g):
    parts = [_dot(o.astype(BF16), wuv_ref[hh]) for hh, o in enumerate(o_heads)]
    return _rms(jnp.concatenate(parts, axis=1), g).astype(BF16)


def _widen(x, n):
    if n <= LANES:
        return x[:, :n]
    return jnp.concatenate([x] * (n // LANES), axis=1)


def _att_kernel(qi_ref, kj_ref, ql_ref, qr_ref, kv_ref, kr_ref, kvm_ref, krm_ref, wuv_ref, g_ref, o_ref,
                m_scr, l_scr, acc_scr, *, heads, tq, tk):
    qi = qi_ref[pl.program_id(1)]
    kj = kj_ref[pl.program_id(1)]
    kvl = ql_ref.shape[2]
    rows = heads * tq
    ql = ql_ref[...].reshape(rows, kvl)
    qr = qr_ref[...].reshape(rows, qr_ref.shape[2])

    def update(s, kv, first):
        m_cur = jnp.max(s, axis=1, keepdims=True)
        if first:
            m_new = jnp.broadcast_to(m_cur, (rows, LANES))
        else:
            m_prev = m_scr[...]
            m_new = jnp.maximum(m_prev, m_cur)
        p = jnp.exp(s - _widen(m_new, s.shape[1]))
        p_sum = jnp.sum(p, axis=1, keepdims=True)
        pv = _dot(p.astype(BF16), kv)
        if first:
            l_scr[...] = jnp.broadcast_to(p_sum, (rows, LANES))
            acc_scr[...] = pv
        else:
            alpha = jnp.exp(m_prev - m_new)
            l_scr[...] = alpha * l_scr[...] + p_sum
            acc_scr[...] = acc_scr[...] * _widen(alpha, kvl) + pv
        m_scr[...] = m_new

    @pl.when(kj == 0)
    def _():
        kvm = kvm_ref[...]
        update(_dot_nt(ql, kvm) + _dot_nt(qr, krm_ref[...]), kvm, True)

    last = (qi * tq + tq - 1) // tk

    @pl.when(kj < last)
    def _():
        kv = kv_ref[...]
        update(_dot_nt(ql, kv) + _dot_nt(qr, kr_ref[...]), kv, False)

    @pl.when(kj == last)
    def _():
        kv = kv_ref[...]
        s = _dot_nt(ql, kv) + _dot_nt(qr, kr_ref[...])
        qpos = qi * tq + lax.broadcasted_iota(jnp.int32, (rows, tk), 0) % tq
        kpos = kj * tk + lax.broadcasted_iota(jnp.int32, (rows, tk), 1)
        update(jnp.where(kpos <= qpos, s, NEG), kv, False)
        o = acc_scr[...] / _widen(l_scr[...], kvl)
        o_ref[...] = _uv_norm([o[hh * tq:(hh + 1) * tq] for hh in range(heads)], wuv_ref, g_ref[...])


def _attention(ql, qr, kvb, krb, kvm, krm, wuv, g, *, batch, seq, tq, tk):
    heads, rows, kvl = ql.shape
    rope = qr.shape[2]
    vdim = wuv.shape[2]
    nq, nk = seq // tq, seq // tk
    nmeta = kvm.shape[0]
    pairs = [(qi, kj) for qi in range(nq) for kj in range((qi * tq + tq - 1) // tk + 1)]
    qi_tab = jnp.asarray([p[0] for p in pairs], jnp.int32)
    kj_tab = jnp.asarray([p[1] for p in pairs], jnp.int32)
    qidx = lambda b, s, qt, kt: (0, b * nq + qt[s], 0)
    kidx = lambda b, s, qt, kt: (b * nk + kt[s], 0)
    fix2 = lambda b, s, qt, kt: (0, 0)
    grid_spec = pltpu.PrefetchScalarGridSpec(
        num_scalar_prefetch=2,
        grid=(batch, len(pairs)),
        in_specs=[pl.BlockSpec((heads, tq, kvl), qidx), pl.BlockSpec((heads, tq, rope), qidx),
                  pl.BlockSpec((tk, kvl), kidx), pl.BlockSpec((tk, rope), kidx),
                  pl.BlockSpec((nmeta, kvl), fix2), pl.BlockSpec((nmeta, rope), fix2),
                  pl.BlockSpec((heads, kvl, vdim), lambda b, s, qt, kt: (0, 0, 0)),
                  pl.BlockSpec((1, heads * vdim), fix2)],
        out_specs=pl.BlockSpec((tq, heads * vdim), lambda b, s, qt, kt: (b * nq + qt[s], 0)),
        scratch_shapes=[pltpu.VMEM((heads * tq, LANES), F32), pltpu.VMEM((heads * tq, LANES), F32),
                        pltpu.VMEM((heads * tq, kvl), F32)],
    )
    return pl.pallas_call(
        functools.partial(_att_kernel, heads=heads, tq=tq, tk=tk),
        grid_spec=grid_spec,
        out_shape=jax.ShapeDtypeStruct((rows, heads * vdim), BF16),
        compiler_params=_cparams(("parallel", "arbitrary")),
        name="mla_prompt_attention",
    )(qi_tab, kj_tab, ql, qr, kvb, krb, kvm, krm, wuv, g)


def _dec_kernel(pt_ref, ql_ref, qr_ref, qln_ref, qrn_ref, kvn_ref, krn_ref, ckv_hbm, ckr_hbm, o_ref,
                kvbuf, krbuf, s_scr, sem, *, nchunk, gpages, nbuf, total):
    b = pl.program_id(0)
    page = kvbuf.shape[2]
    kvl = kvbuf.shape[3]

    def copies(g, slot):
        bb = g // nchunk
        c0 = (g % nchunk) * gpages
        out = []
        for p in range(gpages):
            pg = pt_ref[bb, c0 + p]
            out.append(pltpu.make_async_copy(ckv_hbm.at[pg], kvbuf.at[slot, p], sem.at[slot]))
            out.append(pltpu.make_async_copy(ckr_hbm.at[pg], krbuf.at[slot, p], sem.at[slot]))
        return out

    def start(g, slot):
        for k, cp in enumerate(copies(g, slot)):
            cp.start(priority=(k // 2) % 2)

    ql = ql_ref[0].astype(F32)
    qr = qr_ref[0].astype(F32)
    qln = qln_ref[0].astype(F32)
    qrn = qrn_ref[0].astype(F32)

    def scores(q_l, q_r, slot):
        kvc = kvbuf[slot].reshape(gpages * page, kvl)
        krt = jnp.concatenate([krbuf[slot, p] for p in range(gpages)], axis=1)
        return _dot_nt(q_l, kvc) + _dot(q_r, krt)

    @pl.when(b == 0)
    def _():
        for k in range(min(nbuf - 1, total)):
            start(k, k % nbuf)
        for cp in copies(0, 0):
            cp.wait()
        s_scr[...] = scores(ql, qr, 0)

    kvn = kvn_ref[0]
    krn = krn_ref[0]
    heads = ql.shape[0]
    m0 = jnp.sum(ql * kvn, axis=1, keepdims=True) + jnp.sum(qr * krn, axis=1, keepdims=True)
    l0 = jnp.ones((heads, 1), F32)
    acc0 = jnp.broadcast_to(kvn, (heads, kvl))

    g0 = b * nchunk

    def body(c, carry):
        m, l, acc, s = carry
        g = g0 + c
        slot = g % nbuf
        nxt = g + nbuf - 1

        @pl.when(nxt < total)
        def _():
            start(nxt, nxt % nbuf)

        @pl.when(g + 1 < total)
        def _():
            for cp in copies(g + 1, (g + 1) % nbuf):
                cp.wait()

        own = c + 1 < nchunk
        s_next = scores(jnp.where(own, ql, qln), jnp.where(own, qr, qrn), jnp.minimum(g + 1, total - 1) % nbuf)
        kvc = kvbuf[slot].reshape(gpages * page, kvl)
        m_new = jnp.maximum(m, jnp.max(s, axis=1, keepdims=True))
        alpha = jnp.exp(m - m_new)
        p = jnp.exp(s - m_new)
        l = l * alpha + jnp.sum(p, axis=1, keepdims=True)
        acc = acc * alpha + _dot(p, kvc)
        return m_new, l, acc, s_next

    m, l, acc, s_first_of_next = lax.fori_loop(0, nchunk, body, (m0, l0, acc0, s_scr[...]))
    s_scr[...] = s_first_of_next
    o_ref[0] = acc / l


def _decode_attention(page_table, ql_s, qr_s, kvn, krn, cache_kv, cache_krt, *, gpages, nbuf):
    db, npages = page_table.shape
    heads, kvl = ql_s.shape[1:]
    rope = qr_s.shape[2]
    page = cache_kv.shape[1]
    nchunk = npages // gpages
    total = db * nchunk
    grid_spec = pltpu.PrefetchScalarGridSpec(
        num_scalar_prefetch=1,
        grid=(db,),
        in_specs=[pl.BlockSpec((1, heads, kvl), lambda b, pt: (b, 0, 0)),
                  pl.BlockSpec((1, heads, rope), lambda b, pt: (b, 0, 0)),
                  pl.BlockSpec((1, heads, kvl), lambda b, pt: (jnp.minimum(b + 1, db - 1), 0, 0)),
                  pl.BlockSpec((1, heads, rope), lambda b, pt: (jnp.minimum(b + 1, db - 1), 0, 0)),
                  pl.BlockSpec((1, 1, kvl), lambda b, pt: (b, 0, 0)),
                  pl.BlockSpec((1, 1, rope), lambda b, pt: (b, 0, 0)),
                  pl.BlockSpec(memory_space=pl.ANY), pl.BlockSpec(memory_space=pl.ANY)],
        out_specs=pl.BlockSpec((1, heads, kvl), lambda b, pt: (b, 0, 0)),
        scratch_shapes=[pltpu.VMEM((nbuf, gpages, page, kvl), F32),
                        pltpu.VMEM((nbuf, gpages, rope, page), F32),
                        pltpu.VMEM((heads, gpages * page), F32),
                        pltpu.SemaphoreType.DMA((nbuf,))],
    )
    return pl.pallas_call(
        functools.partial(_dec_kernel, nchunk=nchunk, gpages=gpages, nbuf=nbuf, total=total),
        grid_spec=grid_spec,
        out_shape=jax.ShapeDtypeStruct((db, heads, kvl), F32),
        compiler_params=_cparams(("arbitrary",)),
        name="mla_decode_attention",
    )(page_table, ql_s, qr_s, ql_s, qr_s, kvn, krn, cache_kv, cache_krt)


def _dec_out_kernel(o_ref, wuv_ref, g_ref, out_ref):
    heads = o_ref.shape[0]
    out_ref[...] = _uv_norm([o_ref[hh] for hh in range(heads)], wuv_ref, g_ref[...])


def _decode_out(o_hm, wuv, g):
    heads, rows, kvl = o_hm.shape
    vdim = wuv.shape[2]
    return pl.pallas_call(
        _dec_out_kernel,
        out_shape=jax.ShapeDtypeStruct((rows, heads * vdim), BF16),
        name="mla_decode_out",
    )(o_hm, wuv, g)


def _head_norm_gate(o, gn, rg):
    mu = jnp.mean(o, axis=-1, keepdims=True)
    var = jnp.mean(jnp.square(o - mu), axis=-1, keepdims=True)
    y = (o - mu) * lax.rsqrt(var + EPS) * gn
    return y * jax.nn.silu(rg)


def _ret_kernel(rq_ref, rk_ref, rv_ref, rg_ref, rkm_ref, rvm_ref, dec_ref, qd_ref, kd_ref, kdm_ref, gc_ref,
                gn_ref, o_ref, st_ref, *, chunk, nchunks):
    dec = dec_ref[0]
    qd = qd_ref[0]
    kd = kd_ref[0]
    gc = gc_ref[0]
    gn = gn_ref[...]
    s0 = _dot_tn((rkm_ref[...].astype(F32) * kdm_ref[0]).astype(BF16), rvm_ref[...])

    def body(c, state):
        r0 = pl.multiple_of(c * chunk, chunk)
        q = rq_ref[pl.ds(r0, chunk), :]
        k = rk_ref[pl.ds(r0, chunk), :]
        v = rv_ref[pl.ds(r0, chunk), :]
        rg = rg_ref[pl.ds(r0, chunk), :].astype(F32)
        scores = _dot_nt(q, k) * dec
        o = _dot(scores.astype(BF16), v) + _dot(q, state.astype(BF16)) * qd
        new_state = state * gc + _dot_tn((k.astype(F32) * kd).astype(BF16), v)
        o_ref[pl.ds(r0, chunk), :] = _head_norm_gate(o, gn, rg).astype(BF16)
        return new_state

    st_ref[0, 0, 0] = lax.fori_loop(0, nchunks, body, s0, unroll=2 if nchunks % 2 == 0 else 1)


def _retention_prompt(rb, rkm, rvm, tabs, gn, *, batch, seq, heads, dk, chunk):
    dec, qd, kd, kdm, gc = tabs
    nmeta = rkm.shape[0]
    col = lambda off: (lambda b, h: (b, off * heads + h))
    tab = lambda b, h: (h, 0, 0)
    return pl.pallas_call(
        functools.partial(_ret_kernel, chunk=chunk, nchunks=seq // chunk),
        grid=(batch, heads),
        in_specs=[pl.BlockSpec((seq, dk), col(0)), pl.BlockSpec((seq, dk), col(1)),
                  pl.BlockSpec((seq, dk), col(2)), pl.BlockSpec((seq, dk), col(3)),
                  pl.BlockSpec((nmeta, dk), lambda b, h: (0, h)), pl.BlockSpec((nmeta, dk), lambda b, h: (0, h)),
                  pl.BlockSpec((1, chunk, chunk), tab), pl.BlockSpec((1, chunk, dk), tab),
                  pl.BlockSpec((1, chunk, dk), tab), pl.BlockSpec((1, nmeta, dk), tab),
                  pl.BlockSpec((1, dk, dk), tab),
                  pl.BlockSpec((1, dk), lambda b, h: (0, h))],
        out_specs=[pl.BlockSpec((seq, dk), lambda b, h: (b, h)),
                   pl.BlockSpec((1, 1, 1, dk, dk), lambda b, h: (0, b, h, 0, 0))],
        out_shape=[jax.ShapeDtypeStruct((batch * seq, heads * dk), BF16),
                   jax.ShapeDtypeStruct((1, batch, heads, dk, dk), F32)],
        compiler_params=_cparams(("parallel", "parallel")),
        name="retention_prompt",
    )(rb, rb, rb, rb, rkm, rvm, dec, qd, kd, kdm, gc, gn)


def _ret_s_kernel(q_ref, k_ref, v_ref, g_ref, gam_ref, gn_ref, st_ref, o_ref, ns_ref, *, heads, dk, bs):
    rowid = lax.broadcasted_iota(jnp.int32, (8, dk), 0)
    for hh in range(heads):
        sl = slice(hh * dk, (hh + 1) * dk)
        gam = gam_ref[hh]
        outs = []
        for i in range(bs):
            q = q_ref[i:i + 1, sl]
            k = k_ref[i:i + 1, sl]
            v = v_ref[i:i + 1, sl]
            state = st_ref[0, i, hh]
            k8 = jnp.where(rowid == 0, jnp.broadcast_to(k, (8, dk)), 0.0).astype(BF16)
            v8 = jnp.broadcast_to(v, (8, dk)).astype(BF16)
            ns_ref[0, i, hh] = state * gam[0:1, :] + _dot_tn(k8, v8)
            q8 = jnp.broadcast_to(q, (8, dk)).astype(BF16)
            qs = _dot(q8, state.astype(BF16))[0:1, :]
            qk = jnp.sum(q.astype(BF16).astype(F32) * k.astype(BF16).astype(F32), axis=1, keepdims=True)
            outs.append(qk * v.astype(BF16).astype(F32) + qs * gam[0:1, :])
        o = jnp.concatenate(outs, axis=0)
        o_ref[:, sl] = _head_norm_gate(o, gn_ref[:, sl], g_ref[:, sl]).astype(BF16)


def _retention_sample(rq, rk, rv, rg, gam, gn, state, *, heads, dk, bs):
    db = rq.shape[0]
    n = heads * dk
    row = lambda i: (i, 0)
    st_spec = pl.BlockSpec((1, bs, heads, dk, dk), lambda i: (0, i, 0, 0, 0))
    return pl.pallas_call(
        functools.partial(_ret_s_kernel, heads=heads, dk=dk, bs=bs),
        grid=(db // bs,),
        in_specs=[pl.BlockSpec((bs, n), row), pl.BlockSpec((bs, n), row), pl.BlockSpec((bs, n), row),
                  pl.BlockSpec((bs, n), row), pl.BlockSpec((heads, 8, dk), lambda i: (0, 0, 0)),
                  pl.BlockSpec((1, n), lambda i: (0, 0)), st_spec],
        out_specs=[pl.BlockSpec((bs, n), row), st_spec],
        out_shape=[jax.ShapeDtypeStruct((db, n), BF16), jax.ShapeDtypeStruct(state.shape, F32)],
        compiler_params=_cparams(("parallel",)),
        name="retention_sample",
    )(rq, rk, rv, rg, gam, gn, state)


def _out_kernel(om_ref, or_ref, x_ref, wo1_ref, wo2_ref, g_ref, wg_ref, bg_ref, wr_ref, br_ref,
                x1_ref, h2_ref, rt_ref, *, nsteps, ngroups, nexperts):
    i = pl.program_id(0)

    @pl.when(i < nsteps)
    def _():
        _out_rows(om_ref, or_ref, x_ref, wo1_ref, wo2_ref, g_ref, wg_ref, bg_ref, wr_ref, br_ref,
                  x1_ref, h2_ref, rt_ref, ngroups=ngroups, nexperts=nexperts)

    @pl.when(i >= nsteps)
    def _():
        h2_ref[...] = jnp.zeros_like(h2_ref)


def _out_rows(om_ref, or_ref, x_ref, wo1_ref, wo2_ref, g_ref, wg_ref, bg_ref, wr_ref, br_ref,
              x1_ref, h2_ref, rt_ref, *, ngroups, nexperts):
    x1 = x_ref[...] + _dot(om_ref[...], wo1_ref[...]) + _dot(or_ref[...], wo2_ref[...])
    x1_ref[...] = x1
    hb = _rms(x1, g_ref[...]).astype(BF16)
    _store_chunked(h2_ref, hb.astype(F32), x1.shape[1] // LANES)
    tm = x1.shape[0]
    lane = lax.broadcasted_iota(jnp.int32, (tm, LANES), 1)
    gl = jnp.where(lane < ngroups, _dot(hb, wg_ref[...]) + bg_ref[...], NEG)
    gmax = jnp.max(gl, axis=1, keepdims=True)
    gsum = jnp.sum(jnp.exp(gl - gmax), axis=1, keepdims=True)
    g_w = 1.0 / gsum
    g_idx = jnp.min(jnp.where(gl == gmax, lane, LANES), axis=1, keepdims=True)
    epg = nexperts // ngroups
    in_group = (lane >= g_idx * epg) & (lane < (g_idx + 1) * epg)
    el = jnp.where(in_group, _dot(hb, wr_ref[...]) + br_ref[...], NEG)
    emax = jnp.max(el, axis=1, keepdims=True)
    esum = jnp.sum(jnp.exp(el - emax), axis=1, keepdims=True)
    idx1 = jnp.min(jnp.where(el == emax, lane, LANES), axis=1, keepdims=True)
    el2 = jnp.where(lane == idx1, NEG, el)
    emax2 = jnp.max(el2, axis=1, keepdims=True)
    idx2 = jnp.min(jnp.where(el2 == emax2, lane, LANES), axis=1, keepdims=True)
    p1 = 1.0 / esum
    p2 = jnp.exp(emax2 - emax) / esum
    w1 = p1 / (p1 + p2) * g_w
    w2 = p2 / (p1 + p2) * g_w
    rt_ref[...] = jnp.where(lane == 0, w1, jnp.where(lane == 1, w2, jnp.where(
        lane == 2, idx1.astype(F32), jnp.where(lane == 3, idx2.astype(F32), 0.0))))


def _out_kernel_into(om_ref, or_ref, x_ref, wo1_ref, wo2_ref, g_ref, wg_ref, bg_ref, wr_ref, br_ref,
                     h2_prev_ref, x1_ref, h2_ref, rt_ref, **kw):
    del h2_prev_ref
    _out_kernel(om_ref, or_ref, x_ref, wo1_ref, wo2_ref, g_ref, wg_ref, bg_ref, wr_ref, br_ref,
                x1_ref, h2_ref, rt_ref, **kw)


def _out_proj(om, orr, x, wo1, wo2, g, wg, bg, wr, br, *, rows, tm, h2_rows, h2_row0, h2_prev,
              ngroups, nexperts):
    d = x.shape[1]
    n1, n2 = om.shape[1], orr.shape[1]
    nsteps = rows // tm
    ntail = (h2_rows - rows) // tm if h2_prev is None else 0
    row = lambda i: (jnp.minimum(i, nsteps - 1), 0)
    fix = lambda i: (0, 0)
    in_specs = [pl.BlockSpec((tm, n1), row), pl.BlockSpec((tm, n2), row), pl.BlockSpec((tm, d), row),
                pl.BlockSpec((n1, d), fix), pl.BlockSpec((n2, d), fix), pl.BlockSpec((1, d), fix),
                pl.BlockSpec((d, LANES), fix), pl.BlockSpec((1, LANES), fix),
                pl.BlockSpec((d, LANES), fix), pl.BlockSpec((1, LANES), fix)]
    args = [om, orr, x, wo1, wo2, g, wg, bg, wr, br]
    body, aliases = _out_kernel, {}
    if h2_prev is not None:
        in_specs.append(pl.BlockSpec(memory_space=pl.ANY))
        args.append(h2_prev)
        body, aliases = _out_kernel_into, {len(args) - 1: 1}
    return pl.pallas_call(
        functools.partial(body, nsteps=nsteps, ngroups=ngroups, nexperts=nexperts),
        grid=(nsteps + ntail,),
        in_specs=in_specs,
        out_specs=[pl.BlockSpec((tm, d), row),
                   pl.BlockSpec((tm * (d // LANES), LANES), lambda i: (h2_row0 // tm + i, 0)),
                   pl.BlockSpec((tm, LANES), row)],
        out_shape=[jax.ShapeDtypeStruct((rows, d), F32),
                   jax.ShapeDtypeStruct((h2_rows * (d // LANES), LANES), F32),
                   jax.ShapeDtypeStruct((rows, LANES), F32)],
        input_output_aliases=aliases,
        compiler_params=_cparams(("arbitrary",)),
        name="out_proj_router",
    )(*args)


NWBUF = 3
NXBUF = 3
NYBUF = 3


def _expert_tile(npair, nexperts):
    share = npair / nexperts
    return max(8, -(-int(math.ceil((share + 2.5 * math.sqrt(share)) / 2)) // 8) * 8)


def _moe_kernel(tt_ref, ms_ref, sp_ref, h2_hbm, wg_hbm, wu_hbm, wd_hbm, ys_hbm,
                xbuf, ybuf, wgb, wub, wdb, gsem, ssem, wsem, *, tm, ntok, nck, pitch, ntiles):
    i = pl.program_id(0)
    nused = ms_ref[0]
    nue = ms_ref[1]
    npair = 2 * ntok

    def hbm_row(ref, t):
        return ref.at[pl.ds(t * nck, nck), :]

    def buf_row(buf, slot, r):
        return buf.at[slot, pl.ds(r * pitch, nck), :]

    def weight_copies(q):
        e = ms_ref[2 + q]
        w = q % NWBUF
        return [pltpu.make_async_copy(src.at[e], dst.at[w], wsem.at[w])
                for src, dst in ((wg_hbm, wgb), (wu_hbm, wub), (wd_hbm, wdb))]

    def gather(tile, slot):
        src0 = tt_ref[0, tile]
        last = src0 + tt_ref[1, tile] - 1
        for r in range(tm):
            pair = sp_ref[jnp.minimum(src0 + r, last)]
            tok = pair - jnp.where(pair >= ntok, ntok, 0)
            pltpu.make_async_copy(hbm_row(h2_hbm, tok), buf_row(xbuf, slot, r), gsem.at[slot]).start(priority=1)

    def scatter(tile, slot, n):
        src0 = tt_ref[0, tile]
        for r in range(tm):
            pair = sp_ref[jnp.minimum(src0 + r, npair - 1)]
            dst = jnp.where(r < n, pair, npair + slot * tm + r)
            pltpu.make_async_copy(buf_row(ybuf, slot, r), hbm_row(ys_hbm, dst), ssem.at[slot]).start()

    def wait_tile(buf, sem, slot):
        pltpu.make_async_copy(h2_hbm.at[pl.ds(0, tm * nck), :], buf.at[slot, pl.ds(0, tm * nck), :],
                              sem.at[slot]).wait()

    @pl.when(i == 0)
    def _():
        for cp in weight_copies(0):
            cp.start()

        @pl.when(nue > 1)
        def _():
            for cp in weight_copies(1):
                cp.start()

        for t in range(NXBUF - 1):
            gather(jnp.minimum(t, nused - 1), t)
        ybuf[...] = jnp.zeros(ybuf.shape, F32)
        for s in range(NYBUF):
            scatter(0, s, 0)

    @pl.when(i < nused)
    def _():
        xs = i % NXBUF
        ys = i % NYBUF
        gather(jnp.minimum(i + NXBUF - 1, nused - 1), (i + NXBUF - 1) % NXBUF)
        wait_tile(xbuf, gsem, xs)
        q = tt_ref[2, i]

        @pl.when((i == 0) | (q != tt_ref[2, jnp.maximum(i - 1, 0)]))
        def _():
            for cp in weight_copies(q):
                cp.wait()

            @pl.when(q + 2 < nue)
            def _():
                for cp in weight_copies(q + 2):
                    cp.start()

        w = q % NWBUF
        x = _load_chunked(xbuf.at[xs], tm, nck, pitch)
        a = jax.nn.silu(_dot(x, wgb[w])) * _dot(x, wub[w])
        wait_tile(ybuf, ssem, ys)
        _store_chunked(ybuf.at[ys], _dot(a, wdb[w]), pitch)
        scatter(i, ys, tt_ref[1, i])

    @pl.when(i == ntiles - 1)
    def _():
        for t in range(1, NXBUF):
            wait_tile(xbuf, gsem, (nused - 1 + t) % NXBUF)
        for s in range(NYBUF):
            wait_tile(ybuf, ssem, s)


def _moe(tile_tab, misc, sorted_pairs, h2, w_gate, w_up, w_down, *, tm):
    ntiles = tile_tab.shape[1]
    ntok = sorted_pairs.shape[0] // 2
    d, f = w_gate.shape[1:]
    nck = d // LANES
    pitch = nck + 1
    anyspec = pl.BlockSpec(memory_space=pl.ANY)
    grid_spec = pltpu.PrefetchScalarGridSpec(
        num_scalar_prefetch=3,
        grid=(ntiles,),
        in_specs=[anyspec, anyspec, anyspec, anyspec],
        out_specs=anyspec,
        scratch_shapes=[pltpu.VMEM((NXBUF, tm * pitch, LANES), F32), pltpu.VMEM((NYBUF, tm * pitch, LANES), F32),
                        pltpu.VMEM((NWBUF, d, f), F32), pltpu.VMEM((NWBUF, d, f), F32),
                        pltpu.VMEM((NWBUF, f, d), F32),
                        pltpu.SemaphoreType.DMA((NXBUF,)), pltpu.SemaphoreType.DMA((NYBUF,)),
                        pltpu.SemaphoreType.DMA((NWBUF,))],
    )
    return pl.pallas_call(
        functools.partial(_moe_kernel, tm=tm, ntok=ntok, nck=nck, pitch=pitch, ntiles=ntiles),
        grid_spec=grid_spec,
        out_shape=jax.ShapeDtypeStruct(((2 * ntok + NYBUF * tm) * nck, LANES), F32),
        compiler_params=_cparams(("arbitrary",)),
        name="moe_experts",
    )(tile_tab, misc, sorted_pairs, h2, w_gate, w_up, w_down)


def _comb_kernel(x1_ref, ya_ref, yb_ref, rt_ref, g_ref, y_ref):
    tm, d = x1_ref.shape
    nck = d // LANES
    rt = rt_ref[...]
    x = (x1_ref[...] + rt[:, 0:1] * _load_chunked(ya_ref, tm, nck, nck)
         + rt[:, 1:2] * _load_chunked(yb_ref, tm, nck, nck))
    y_ref[...] = _rms(x, g_ref[...])


def _combine(x1, ys, rt, g, *, tm, row0, ntok, rows):
    d = x1.shape[1]
    nck = d // LANES
    return pl.pallas_call(
        _comb_kernel,
        grid=(rows // tm,),
        in_specs=[pl.BlockSpec((tm, d), lambda i: (i, 0)),
                  pl.BlockSpec((tm * nck, LANES), lambda i: (row0 // tm + i, 0)),
                  pl.BlockSpec((tm * nck, LANES), lambda i: ((ntok + row0) // tm + i, 0)),
                  pl.BlockSpec((tm, LANES), lambda i: (i, 0)), pl.BlockSpec((1, d), lambda i: (0, 0))],
        out_specs=pl.BlockSpec((tm, d), lambda i: (i, 0)),
        out_shape=jax.ShapeDtypeStruct((rows, d), F32),
        compiler_params=_cparams(("parallel",)),
        name="moe_combine_norm",
    )(x1, ys, ys, rt, g)


def _rope_tables(pos, dim):
    half = dim // 2
    inv = jnp.exp(jnp.arange(half, dtype=F32) * (-2.0 * math.log(ROPE_BASE) / dim))
    ang = pos.astype(F32)[:, None] * inv[None, :]
    return jnp.cos(ang), jnp.sin(ang)


def _rot_cols(w):
    half = w.shape[-1] // 2
    return jnp.concatenate([-w[..., half:], w[..., :half]], axis=-1)


def _pick_tile(rows, target):
    t = min(rows, target)
    while rows % t:
        t //= 2
    return t


def _dispatch(route, ntok, nexperts, tm, ntiles):
    i32 = jnp.int32
    ids = jnp.concatenate([route[:, 2], route[:, 3]]).astype(i32)
    npair = 2 * ntok
    _, sorted_pairs = lax.sort((ids, jnp.arange(npair, dtype=i32)), num_keys=1, is_stable=True)
    ex = jnp.arange(nexperts, dtype=i32)
    counts = jnp.sum((ids[:, None] == ex[None, :]).astype(i32), axis=0)
    starts = jnp.cumsum(counts) - counts
    tcount = (counts + tm - 1) // tm
    tend = jnp.cumsum(tcount)
    tstart = tend - tcount
    nused = tend[-1]
    t = jnp.arange(ntiles, dtype=i32)
    tile_expert = jnp.minimum(jnp.sum((tend[None, :] <= t[:, None]).astype(i32), axis=1), nexperts - 1)
    pick = lambda tab: jnp.sum(jnp.where(tile_expert[:, None] == ex[None, :], tab[None, :], 0), axis=1)
    local = t - pick(tstart)
    tile_src0 = jnp.clip(pick(starts) + local * tm, 0, npair - 1)
    tile_rows = jnp.where(t < nused, jnp.clip(pick(counts) - local * tm, 0, tm), 0)
    used = counts > 0
    tile_q = jnp.sum((used[None, :] & (ex[None, :] < tile_expert[:, None])).astype(i32), axis=1)
    used_ids = jnp.sort(jnp.where(used, ex, nexperts))
    tile_tab = jnp.stack([tile_src0, tile_rows, tile_q]).astype(i32)
    misc = jnp.concatenate([nused.reshape(1), jnp.sum(used.astype(i32)).reshape(1), used_ids]).astype(i32)
    return tile_tab, misc, sorted_pairs


def kernel(x_prompt, x_sample, cache_kv_latent, cache_k_rope, state_retention, page_table, meta_tokens,
           ln_mix_g, w_in, q_norm_g, w_uq, w_uk, kv_norm_g, w_uv, mla_out_g, ret_gn_g, w_o, ln_ffn_g,
           w_group, b_group, w_router, b_router, w_gate, w_up, w_down, final_g):
    batch, seq, d = x_prompt.shape
    db, dec_seq, _ = x_sample.shape
    depth = w_in.shape[0]
    assert depth == 1 and dec_seq == 1
    ql = q_norm_g.shape[1]
    kvl = kv_norm_g.shape[1]
    mh, nr = w_uq.shape[2], w_uq.shape[3]
    nope = w_uk.shape[3]
    rope = nr - nope
    vdim = w_uv.shape[3]
    rdim = ret_gn_g.shape[1]
    dk = state_retention.shape[3]
    rh = rdim // dk
    ngroups = w_group.shape[2]
    nexperts = w_router.shape[2]
    page = cache_kv_latent.shape[2]
    npages = page_table.shape[1]
    past_len = npages * page
    chunk = min(2 * ROW_TILE, seq)
    tk = min(4 * ROW_TILE, seq)
    nmain = batch * seq
    naux = -(-(db + N_META) // LANES) * LANES
    ts = -(-db // ROW_TILE) * ROW_TILE
    assert seq % 256 == 0 and seq % tk == 0 and ts <= naux and db % 8 == 0 and rh == mh

    wit = jnp.swapaxes(w_in[0], 0, 1)
    o_kr = ql + kvl
    o_ret = o_kr + rope
    w_a = jnp.concatenate([wit[:o_ret], jnp.swapaxes(_rot_cols(jnp.swapaxes(wit[o_kr:o_ret], 0, 1)), 0, 1)],
                          axis=0).astype(BF16)
    w_b = wit[o_ret:].astype(BF16)
    wq = w_uq[0]
    wq_rope = wq[:, :, nope:]
    wuq = jnp.concatenate([wq[:, :, :nope].reshape(ql, mh * nope), wq_rope.reshape(ql, mh * rope),
                           _rot_cols(wq_rope).reshape(ql, mh * rope)], axis=1).astype(BF16)
    wukt = jnp.transpose(w_uk[0], (1, 2, 0)).astype(BF16)
    wuv = jnp.transpose(w_uv[0], (1, 0, 2)).astype(BF16)
    wo = w_o[0].astype(BF16)
    wo1, wo2 = wo[:mh * vdim], wo[mh * vdim:]
    wg_pad = jnp.zeros((d, LANES), F32).at[:, :ngroups].set(w_group[0]).astype(BF16)
    bg_pad = jnp.zeros((1, LANES), F32).at[0, :ngroups].set(b_group[0])
    wr_pad = jnp.zeros((d, LANES), F32).at[:, :nexperts].set(w_router[0]).astype(BF16)
    br_pad = jnp.zeros((1, LANES), F32).at[0, :nexperts].set(b_router[0])

    pos_main = N_META + jnp.arange(seq, dtype=jnp.int32)
    pos_aux = jnp.concatenate([jnp.full((db,), past_len, jnp.int32), jnp.arange(N_META, dtype=jnp.int32),
                               jnp.zeros((naux - db - N_META,), jnp.int32)])

    def tables(pos):
        c64, s64 = _rope_tables(pos, rope)
        c128, s128 = _rope_tables(pos, dk)
        cs = jnp.concatenate([c64, c64], axis=1)
        sn = jnp.concatenate([s64, s64], axis=1)
        return (cs, sn, jnp.tile(cs, (1, 2)), jnp.tile(sn, (1, 2)),
                jnp.concatenate([c128, c128], axis=1), jnp.concatenate([-s128, s128], axis=1))

    log_g = jnp.log1p(-jnp.exp2(-5.0 - jnp.arange(rh, dtype=F32)))
    n = jnp.arange(chunk, dtype=F32)
    diff = n[:, None] - n[None, :]
    t_dec = jnp.where(diff >= 0, jnp.exp(log_g[:, None, None] * jnp.maximum(diff, 0.0)), 0.0)
    t_qd = jnp.broadcast_to(jnp.exp((n[None, :] + 1.0) * log_g[:, None])[:, :, None], (rh, chunk, dk))
    k_decay = jnp.exp((chunk - 1.0 - n)[None, :] * log_g[:, None])
    t_kd = jnp.broadcast_to(k_decay[:, :, None], (rh, chunk, dk))
    t_kdm = jnp.broadcast_to(k_decay[:, chunk - N_META:, None], (rh, N_META, dk))
    t_gc = jnp.broadcast_to(jnp.exp(chunk * log_g)[:, None, None], (rh, dk, dk))
    t_gam = jnp.broadcast_to(jnp.exp(log_g)[:, None, None], (rh, 8, dk))

    x_main = x_prompt.reshape(nmain, d)
    x_aux = jnp.concatenate([x_sample.reshape(db, d), meta_tokens.astype(F32),
                             jnp.zeros((naux - db - N_META, d), F32)], axis=0)
    g_mix = ln_mix_g[0][None]
    scale = float(nope + rope) ** -0.5
    tm_main = _pick_tile(seq, 512)
    tm_aux = _pick_tile(naux, 512)

    def token_front(x, pos, tm):
        cs, sn, cs8, sn8, cos_r, sin_r = tables(pos)
        cq, kv, kvb, kr, krb = _in_a(x, g_mix, w_a, q_norm_g, kv_norm_g, cs, sn,
                                     tm=tm, ql=ql, kvl=kvl, rope=rope)
        rb = _in_b(x, g_mix, w_b, cos_r, sin_r, tm=tm, heads=rh, dk=dk)
        q_lat, q_rope = _q_proj(cq, wuq, wukt, cs8, sn8, tm=tm, heads=mh, nope=nope, rope=rope, scale=scale)
        return kv, kvb, kr, krb, rb, q_lat, q_rope

    kv_m, kvb_m, kr_m, krb_m, rb_m, ql_m, qr_m = token_front(x_main, pos_main, tm_main)
    kv_a, kvb_a, kr_a, krb_a, rb_a, ql_a, qr_a = token_front(x_aux, pos_aux, tm_aux)
    meta = slice(db, db + N_META)

    om_m = _attention(ql_m, qr_m, kvb_m, krb_m, kvb_a[meta], krb_a[meta], wuv, mla_out_g,
                      batch=batch, seq=seq, tq=ROW_TILE, tk=tk)
    rdk = rh * dk
    or_m, st_p = _retention_prompt(rb_m, rb_a[meta, rdk:2 * rdk], rb_a[meta, 2 * rdk:3 * rdk],
                                   (t_dec, t_qd, t_kd, t_kdm, t_gc), ret_gn_g,
                                   batch=batch, seq=seq, heads=rh, dk=dk, chunk=chunk)

    ql_s = jnp.transpose(ql_a[:, :db], (1, 0, 2))
    qr_s = jnp.transpose(qr_a[:, :db], (1, 0, 2))
    o_s = _decode_attention(page_table, ql_s, qr_s, kv_a[:db, None, :], kr_a[:db, None, :],
                            cache_kv_latent[0], jnp.swapaxes(cache_k_rope[0], 1, 2),
                            gpages=min(16, npages), nbuf=4)
    om_s = _decode_out(jnp.transpose(o_s, (1, 0, 2)), wuv, mla_out_g)
    rs = rb_a[:db].astype(F32)
    or_s, st_s = _retention_sample(rs[:, :rdk], rs[:, rdk:2 * rdk], rs[:, 2 * rdk:3 * rdk], rs[:, 3 * rdk:],
                                   t_gam, ret_gn_g, state_retention, heads=rh, dk=dk, bs=8)
    pad_rows = lambda a: jnp.concatenate([a, jnp.zeros((ts - db, a.shape[1]), a.dtype)], axis=0)
    om_a, or_a = pad_rows(om_s), pad_rows(or_s)

    g_ffn = ln_ffn_g[0][None]
    ntok = nmain + ts
    tm_o = _pick_tile(nmain, 256)
    outp = functools.partial(_out_proj, h2_rows=nmain + -(-ts // tm_o) * tm_o, ngroups=ngroups, nexperts=nexperts)
    x1_m, h2, rt_m = outp(om_m, or_m, x_main, wo1, wo2, g_ffn, wg_pad, bg_pad, wr_pad, br_pad,
                          rows=nmain, tm=tm_o, h2_row0=0, h2_prev=None)
    x1_a, h2, rt_a = outp(om_a, or_a, x_aux, wo1, wo2, g_ffn, wg_pad, bg_pad, wr_pad, br_pad,
                          rows=ts, tm=ROW_TILE, h2_row0=nmain, h2_prev=h2)

    route = jnp.concatenate([rt_m[:, :4], rt_a[:, :4]], axis=0)
    tm_e = _expert_tile(2 * ntok, nexperts)
    ntiles =-(-(2 * ntok + nexperts * (tm_e - 1)) // tm_e)
    ys = _moe(*_dispatch(route, ntok, nexperts, tm_e, ntiles), h2, w_gate[0], w_up[0], w_down[0], tm=tm_e)
    fg = final_g[None]
    y_m = _combine(x1_m, ys, rt_m, fg, tm=ROW_TILE, row0=0, ntok=ntok, rows=nmain)
    y_a = _combine(x1_a, ys, rt_a, fg, tm=ROW_TILE, row0=nmain, ntok=ntok, rows=ts)

    def with_meta(a_main, a_aux):
        w = a_main.shape[1]
        m = jnp.broadcast_to(a_aux[meta][None], (batch, N_META, w))
        return jnp.concatenate([m, a_main.reshape(batch, seq, w)], axis=1)[None]

    return (y_m.reshape(batch, seq, d), y_a[:db].reshape(db, 1, d),
            with_meta(kv_m, kv_a), with_meta(kr_m, kr_a), st_p,
            kv_a[:db].reshape(1, db, 1, kvl), kr_a[:db].reshape(1, db, 1, rope), st_s)
```
